```python
import math
import jax, jax.numpy as jnp
from jax import lax
import numpy as np

D_MODEL = 1024
BATCH = 1
SEQ = 16384
DEPTH = 2
DEC_BATCH = 32
DEC_SEQ = 8
PAST_LEN = 16384
PAGE_SIZE = 128

D_BR = D_MODEL // 4
HEAD_DIM = 64
POOL_WINDOWS = (2, 4, 8, 16)
POOL_GW = D_BR // len(POOL_WINDOWS)
POOL_BUF = max(POOL_WINDOWS) - 1
RWKV_HEADS = D_BR // HEAD_DIM
RWKV_W_LORA = 64
RWKV_A_LORA = 64
RWKV_G_LORA = 128
RWKV_COLS = 3 * D_BR + RWKV_W_LORA + RWKV_A_LORA + RWKV_G_LORA
RWKV_LN_EPS = 64e-5
SB_HEADS = D_BR // HEAD_DIM
SB_BLOCK = 128
S5_GW = 16
S5_GROUPS = D_BR // S5_GW
S5_STATE = 64
N_BRANCH = 4
OFF_POOL = 0
OFF_RWKV = OFF_POOL + D_BR
OFF_SB = OFF_RWKV + RWKV_COLS
OFF_S5 = OFF_SB + 3 * D_BR
OFF_GATE = OFF_S5 + D_BR
N_IN = OFF_GATE + N_BRANCH * D_MODEL
XA_HEADS = 4
N_MEM = 256
D_FF = -(-8 * D_MODEL // (3 * 256)) * 256
RMS_EPS = 1e-6

kernel_name = 'hybrid_pool_rwkv7_stickbreak_s5_decode_step'


def rms_norm(x, g):
    xf = x.astype(jnp.float32)
    y = xf * lax.rsqrt(jnp.mean(xf * xf, axis=-1, keepdims=True) + RMS_EPS)
    return (y * g.astype(jnp.float32)).astype(x.dtype)


def pool_mixer(u, buf, pos0, w_pool, scale):
    b, t, _ = u.shape
    ext = jnp.concatenate([buf.astype(u.dtype), u], axis=1)
    cs = jnp.cumsum(ext.astype(jnp.float32), axis=1)
    cs = jnp.concatenate([jnp.zeros_like(cs[:, :1]), cs], axis=1)
    pos = pos0 + jnp.arange(t)
    uf = u.astype(jnp.float32)
    groups = []
    for gi, w in enumerate(POOL_WINDOWS):
        c0, c1 = gi * POOL_GW, (gi + 1) * POOL_GW
        win_sum = (cs[:, POOL_BUF + 1:POOL_BUF + 1 + t, c0:c1]
                   - cs[:, POOL_BUF + 1 - w:POOL_BUF + 1 - w + t, c0:c1])
        count = jnp.minimum(pos + 1, w).astype(jnp.float32)[None, :, None]
        groups.append(win_sum / count - uf[..., c0:c1])
    pooled = jnp.stack(groups, axis=2)
    mixed = jnp.einsum('btgc,gcd->btgd', pooled, w_pool.astype(jnp.float32)).reshape(b, t, D_BR)
    return mixed * scale.astype(jnp.float32), ext[:, -POOL_BUF:]


def rwkv_mixer(p, shift, wkv, lw):
    f32 = jnp.float32
    b, t, _ = p.shape
    pf = p.astype(f32)
    prev = jnp.concatenate([shift[:, None, :].astype(f32), pf[:, :-1]], axis=1)
    pm = pf + (prev - pf) * lw['rwkv_mu'].astype(f32)
    o1, o2, o3 = D_BR, 2 * D_BR, 3 * D_BR
    o4 = o3 + RWKV_W_LORA
    o5 = o4 + RWKV_A_LORA
    r, k, v = pm[..., :o1], pm[..., o1:o2], pm[..., o2:o3]
    w_lo, a_lo, g_lo = pm[..., o3:o4], pm[..., o4:o5], pm[..., o5:]
    w_log = -jax.nn.softplus(-(lw['rwkv_w0'] + jnp.tanh(w_lo) @ lw['rwkv_w_up'])) - 0.5
    decay = jnp.exp(-jnp.exp(w_log))
    a = jax.nn.sigmoid(lw['rwkv_a0'] + a_lo @ lw['rwkv_a_up'])
    g = jax.nn.sigmoid(g_lo) @ lw['rwkv_g_up']
    heads = lambda z: z.reshape(b, t, RWKV_HEADS, HEAD_DIM)
    kk = heads(k * lw['rwkv_k_k'])
    kk = kk * lax.rsqrt(jnp.maximum(jnp.sum(kk * kk, -1, keepdims=True), 1e-24))
    k = k * (1.0 + (a - 1.0) * lw['rwkv_k_a'])
    r, k, v, decay, a = heads(r), heads(k), heads(v), heads(decay), heads(a)

    def step(state, inp):
        r_t, k_t, v_t, d_t, kk_t, a_t = inp
        sa = jnp.einsum('bhvk,bhk->bhv', state, -kk_t)
        state = (state * d_t[:, :, None, :]
                 + sa[..., None] * (kk_t * a_t)[:, :, None, :]
                 + v_t[..., None] * k_t[:, :, None, :])
        return state, jnp.einsum('bhvk,bhk->bhv', state, r_t)

    xs = tuple(jnp.swapaxes(z, 0, 1) for z in (r, k, v, decay, kk, a))
    wkv_new, o = lax.scan(step, wkv.astype(f32), xs)
    o = jnp.swapaxes(o, 0, 1)
    o_mean = jnp.mean(o, -1, keepdims=True)
    o_var = jnp.mean(jnp.square(o - o_mean), -1, keepdims=True)
    o = ((o - o_mean) * lax.rsqrt(o_var + RWKV_LN_EPS)).reshape(b, t, D_BR) * lw['rwkv_ln_w'] + lw['rwkv_ln_b']
    bonus = jnp.sum(r * k * lw['rwkv_r_k'], -1, keepdims=True) * v
    o = (o + bonus.reshape(b, t, D_BR)) * g
    return o, p[:, -1], wkv_new


def stick_breaking(q, k, v, q_pos0, bias):
    f32 = jnp.float32
    b, tq, h, hd = q.shape
    blk = SB_BLOCK if tq % SB_BLOCK == 0 else tq
    nb = tq // blk
    kf, vf = k.astype(f32), v.astype(f32)
    key_pos = jnp.arange(k.shape[1])
    scale = hd ** -0.5
    bias_f = bias.astype(f32)[None, :, None, None]

    def one_block(args):
        qb, b_idx = args
        q_pos = q_pos0 + b_idx * blk + jnp.arange(blk)
        z = jnp.einsum('bqhd,bkhd->bhqk', qb.astype(f32), kf) * scale + bias_f
        mask = key_pos[None, :] < q_pos[:, None]
        log_1m = jnp.where(mask, jax.nn.log_sigmoid(-z), 0.0)
        after = lax.cumsum(log_1m, axis=3, reverse=True) - log_1m
        att = jnp.where(mask, jnp.exp(jax.nn.log_sigmoid(z) + after), 0.0)
        return jnp.einsum('bhqk,bkhd->bqhd', att, vf)

    qblocks = jnp.moveaxis(q.reshape(b, nb, blk, h, hd), 1, 0)
    out = lax.map(one_block, (qblocks, jnp.arange(nb)))
    return jnp.moveaxis(out, 0, 1).reshape(b, tq, h, hd).astype(q.dtype)


def complex_affine_combine(e1, e2):
    a1r, a1i, b1r, b1i = e1
    a2r, a2i, b2r, b2i = e2
    return (a1r * a2r - a1i * a2i, a1r * a2i + a1i * a2r,
            a2r * b1r - a2i * b1i + b2r, a2r * b1i + a2i * b1r + b2i)


def s5_mixer(u, h_re, h_im, lw):
    f32 = jnp.float32
    b, t, _ = u.shape
    uf = u.astype(f32)
    ug = uf.reshape(b, t, S5_GROUPS, S5_GW)
    a_re = lw['s5_a_re'].astype(f32)
    a_im = lw['s5_a_im'].astype(f32)
    dt_g = jnp.exp(lw['s5_log_dt'].astype(f32))[:, None]
    mag = jnp.exp(dt_g * a_re)
    ab_re, ab_im = mag * jnp.cos(dt_g * a_im), mag * jnp.sin(dt_g * a_im)
    den = a_re * a_re + a_im * a_im
    n_re = ab_re - 1.0
    f_re = (n_re * a_re + ab_im * a_im) / den
    f_im = (ab_im * a_re - n_re * a_im) / den
    b_re, b_im = lw['s5_b_re'], lw['s5_b_im']
    bb_re = f_re[..., None] * b_re - f_im[..., None] * b_im
    bb_im = f_re[..., None] * b_im + f_im[..., None] * b_re
    bu_re = jnp.einsum('gpc,btgc->btgp', bb_re, ug)
    bu_im = jnp.einsum('gpc,btgc->btgp', bb_im, ug)
    h_re, h_im = h_re.astype(f32), h_im.astype(f32)
    bu_re = bu_re.at[:, 0].add(ab_re * h_re - ab_im * h_im)
    bu_im = bu_im.at[:, 0].add(ab_re * h_im + ab_im * h_re)
    a_seq_re = jnp.broadcast_to(ab_re, bu_re.shape)
    a_seq_im = jnp.broadcast_to(ab_im, bu_im.shape)
    _, _, s_re, s_im = lax.associative_scan(complex_affine_combine, (a_seq_re, a_seq_im, bu_re, bu_im), axis=1)
    y = (jnp.einsum('gcp,btgp->btgc', lw['s5_c_re'], s_re)
         - jnp.einsum('gcp,btgp->btgc', lw['s5_c_im'], s_im))
    y = y.reshape(b, t, D_BR) + lw['s5_d'] * uf
    z = jax.nn.gelu(y) @ lw['s5_w_glu']
    return z[..., :D_BR] * jax.nn.sigmoid(z[..., D_BR:]), s_re[:, -1], s_im[:, -1]


def memory_kv(mem, g_mem, w_k, w_v, k_gain):
    b, m, _ = mem.shape
    mn = rms_norm(mem, g_mem)
    k = rms_norm((mn @ w_k).reshape(b, m, XA_HEADS, HEAD_DIM), k_gain)
    v = (mn @ w_v).reshape(b, m, XA_HEADS, HEAD_DIM)
    return k, v


def cross_attention(h, mem_k, mem_v, w_q, q_gain, w_o):
    b, t, _ = h.shape
    q = rms_norm((h @ w_q).reshape(b, t, XA_HEADS, HEAD_DIM), q_gain)
    s = jnp.einsum('bthd,bmhd->bhtm', q.astype(jnp.float32), mem_k.astype(jnp.float32)) * HEAD_DIM ** -0.5
    pr = jax.nn.softmax(s, axis=-1)
    o = jnp.einsum('bhtm,bmhd->bthd', pr, mem_v.astype(jnp.float32)).reshape(b, t, XA_HEADS * HEAD_DIM)
    return o.astype(h.dtype) @ w_o


def swiglu(h, w_gate, w_up, w_down):
    return (jax.nn.silu(h @ w_gate) * (h @ w_up)) @ w_down


def trunk_layer(x, pos0, mem_k, mem_v, past_k, past_v, pool_buf, shift, wkv, s5_re, s5_im, lw):
    b, t, _ = x.shape
    h = rms_norm(x, lw['norm_mix'])
    proj = h @ lw['w_in']
    o_pool, new_pool = pool_mixer(proj[..., OFF_POOL:OFF_RWKV], pool_buf, pos0, lw['pool_w'], lw['pool_scale'])
    o_rwkv, new_shift, new_wkv = rwkv_mixer(proj[..., OFF_RWKV:OFF_SB], shift, wkv, lw)
    q, k, v = jnp.split(proj[..., OFF_SB:OFF_S5], 3, axis=-1)
    heads = lambda z: z.reshape(b, t, SB_HEADS, HEAD_DIM)
    q = rms_norm(heads(q), lw['sb_q_norm'])
    k = rms_norm(heads(k), lw['sb_k_norm'])
    v = heads(v)
    k_all = jnp.concatenate([past_k.astype(k.dtype), k], axis=1)
    v_all = jnp.concatenate([past_v.astype(v.dtype), v], axis=1)
    o_sb = stick_breaking(q, k_all, v_all, pos0, lw['sb_bias']).reshape(b, t, D_BR)
    o_s5, new_s5_re, new_s5_im = s5_mixer(proj[..., OFF_S5:OFF_GATE], s5_re, s5_im, lw)
    gates = jax.nn.sigmoid(proj[..., OFF_GATE:].astype(jnp.float32)).reshape(b, t, N_BRANCH, D_MODEL)
    branches = jnp.stack([o_pool, o_rwkv, o_sb, o_s5], axis=2).astype(x.dtype)
    lifted = jnp.einsum('btnc,ncd->btnd', branches, lw['w_branch'])
    merged = jnp.sum(gates * lifted.astype(jnp.float32), axis=2).astype(x.dtype)
    x = x + merged @ lw['w_out']
    x = x + cross_attention(rms_norm(x, lw['norm_cross']), mem_k, mem_v, lw['xa_w_q'], lw['xa_q_norm'], lw['xa_w_o'])
    x = x + swiglu(rms_norm(x, lw['norm_ffn']), lw['ffn_w_gate'], lw['ffn_w_up'], lw['ffn_w_down'])
    return x, (k, v, new_pool, new_shift, new_wkv, new_s5_re, new_s5_im)


def setup_inputs(seed: int = 0) -> dict:
    key = jax.random.key(seed)
    ks = iter(jax.random.split(key, 80))
    f32 = jnp.float32
    L = DEPTH

    def nrm(shape, scale=1.0):
        return jax.random.normal(next(ks), shape, f32) * scale

    def gain(shape):
        return 1.0 + 0.02 * jax.random.normal(next(ks), shape, f32)

    def unif(shape, lo, hi):
        return jax.random.uniform(next(ks), shape, f32, lo, hi)

    n_pages = PAST_LEN // PAGE_SIZE
    n_used = DEC_BATCH * n_pages
    n_pool = n_used + max(1, n_used // 4)
    page_table = jax.random.permutation(next(ks), n_pool)[:n_used].reshape(DEC_BATCH, n_pages).astype(jnp.int32)
    a_im_init = jnp.broadcast_to(jnp.pi * jnp.arange(S5_STATE, dtype=f32), (L, S5_GROUPS, S5_STATE))
    return {
        'x_prompt': nrm((BATCH, SEQ, D_MODEL)),
        'x_sample': nrm((DEC_BATCH, DEC_SEQ, D_MODEL)),
        'cache_sb_k': nrm((L, n_pool, PAGE_SIZE, SB_HEADS, HEAD_DIM)),
        'cache_sb_v': nrm((L, n_pool, PAGE_SIZE, SB_HEADS, HEAD_DIM)),
        'cache_mem_k': nrm((L, DEC_BATCH, N_MEM, XA_HEADS, HEAD_DIM)),
        'cache_mem_v': nrm((L, DEC_BATCH, N_MEM, XA_HEADS, HEAD_DIM)),
        'state_pool': nrm((L, DEC_BATCH, POOL_BUF, D_BR)),
        'state_rwkv_shift': nrm((L, DEC_BATCH, RWKV_COLS)),
        'state_rwkv_wkv': nrm((L, DEC_BATCH, RWKV_HEADS, HEAD_DIM, HEAD_DIM), 0.3),
        'state_s5_re': nrm((L, DEC_BATCH, S5_GROUPS, S5_STATE), 0.1),
        'state_s5_im': nrm((L, DEC_BATCH, S5_GROUPS, S5_STATE), 0.1),
        'page_table': page_table,
        'mem_prompt': nrm((BATCH, N_MEM, D_MODEL)),
        'norm_mix': gain((L, D_MODEL)),
        'norm_cross': gain((L, D_MODEL)),
        'norm_mem': gain((L, D_MODEL)),
        'norm_ffn': gain((L, D_MODEL)),
        'w_in': nrm((L, D_MODEL, N_IN), D_MODEL ** -0.5),
        'pool_w': nrm((L, len(POOL_WINDOWS), POOL_GW, POOL_GW), POOL_GW ** -0.5),
        'pool_scale': gain((L, D_BR)),
        'rwkv_mu': unif((L, RWKV_COLS), 0.0, 1.0),
        'rwkv_w0': unif((L, D_BR), -6.0, 0.0),
        'rwkv_w_up': nrm((L, RWKV_W_LORA, D_BR), 0.1 * RWKV_W_LORA ** -0.5),
        'rwkv_a0': nrm((L, D_BR), 0.1),
        'rwkv_a_up': nrm((L, RWKV_A_LORA, D_BR), RWKV_A_LORA ** -0.5),
        'rwkv_g_up': nrm((L, RWKV_G_LORA, D_BR), RWKV_G_LORA ** -0.5),
        'rwkv_k_k': 0.85 + nrm((L, D_BR), 0.02),
        'rwkv_k_a': gain((L, D_BR)),
        'rwkv_r_k': nrm((L, RWKV_HEADS, HEAD_DIM), 0.1),
        'rwkv_ln_w': gain((L, D_BR)),
        'rwkv_ln_b': nrm((L, D_BR), 0.02),
        'sb_q_norm': gain((L, HEAD_DIM)),
        'sb_k_norm': gain((L, HEAD_DIM)),
        'sb_bias': unif((L, SB_HEADS), -9.0, -7.0),
        's5_a_re': -0.5 + nrm((L, S5_GROUPS, S5_STATE), 0.01),
        's5_a_im': a_im_init + nrm((L, S5_GROUPS, S5_STATE), 0.01),
        's5_log_dt': unif((L, S5_GROUPS), math.log(1e-3), math.log(1e-1)),
        's5_b_re': nrm((L, S5_GROUPS, S5_STATE, S5_GW), (2 * S5_GW) ** -0.5),
        's5_b_im': nrm((L, S5_GROUPS, S5_STATE, S5_GW), (2 * S5_GW) ** -0.5),
        's5_c_re': nrm((L, S5_GROUPS, S5_GW, S5_STATE), S5_STATE ** -0.5),
        's5_c_im': nrm((L, S5_GROUPS, S5_GW, S5_STATE), S5_STATE ** -0.5),
        's5_d': nrm((L, D_BR)),
        's5_w_glu': nrm((L, D_BR, 2 * D_BR), D_BR ** -0.5),
        'w_branch': nrm((L, N_BRANCH, D_BR, D_MODEL), D_BR ** -0.5),
        'w_out': nrm((L, D_MODEL, D_MODEL), D_MODEL ** -0.5),
        'xa_w_q': nrm((L, D_MODEL, XA_HEADS * HEAD_DIM), D_MODEL ** -0.5),
        'xa_w_k': nrm((L, D_MODEL, XA_HEADS * HEAD_DIM), D_MODEL ** -0.5),
        'xa_w_v': nrm((L, D_MODEL, XA_HEADS * HEAD_DIM), D_MODEL ** -0.5),
        'xa_q_norm': gain((L, HEAD_DIM)),
        'xa_k_norm': gain((L, HEAD_DIM)),
        'xa_w_o': nrm((L, XA_HEADS * HEAD_DIM, D_MODEL), (XA_HEADS * HEAD_DIM) ** -0.5),
        'ffn_w_gate': nrm((L, D_MODEL, D_FF), D_MODEL ** -0.5),
        'ffn_w_up': nrm((L, D_MODEL, D_FF), D_MODEL ** -0.5),
        'ffn_w_down': nrm((L, D_FF, D_MODEL), D_FF ** -0.5),
    }


def reference(x_prompt, x_sample, cache_sb_k, cache_sb_v, cache_mem_k, cache_mem_v,
              state_pool, state_rwkv_shift, state_rwkv_wkv, state_s5_re, state_s5_im,
              page_table, mem_prompt,
              norm_mix, norm_cross, norm_mem, norm_ffn, w_in, pool_w, pool_scale,
              rwkv_mu, rwkv_w0, rwkv_w_up, rwkv_a0, rwkv_a_up, rwkv_g_up, rwkv_k_k, rwkv_k_a,
              rwkv_r_k, rwkv_ln_w, rwkv_ln_b, sb_q_norm, sb_k_norm, sb_bias,
              s5_a_re, s5_a_im, s5_log_dt, s5_b_re, s5_b_im, s5_c_re, s5_c_im, s5_d, s5_w_glu,
              w_branch, w_out, xa_w_q, xa_w_k, xa_w_v, xa_q_norm, xa_k_norm, xa_w_o,
              ffn_w_gate, ffn_w_up, ffn_w_down):
    bp = x_prompt.shape[0]
    bs = x_sample.shape[0]
    past_len = page_table.shape[1] * PAGE_SIZE
    dt = x_prompt.dtype
    xp, xs = x_prompt, x_sample
    names_p = ('k', 'v', 'mk', 'mv', 'pool', 'shift', 'wkv', 's5r', 's5i')
    names_s = ('k', 'v', 'pool', 'shift', 'wkv', 's5r', 's5i')
    new_p = {n: [] for n in names_p}
    new_s = {n: [] for n in names_s}
    for l in range(DEPTH):
        lw = {
            'norm_mix': norm_mix[l], 'norm_cross': norm_cross[l], 'norm_ffn': norm_ffn[l],
            'w_in': w_in[l], 'pool_w': pool_w[l], 'pool_scale': pool_scale[l],
            'rwkv_mu': rwkv_mu[l], 'rwkv_w0': rwkv_w0[l], 'rwkv_w_up': rwkv_w_up[l],
            'rwkv_a0': rwkv_a0[l], 'rwkv_a_up': rwkv_a_up[l], 'rwkv_g_up': rwkv_g_up[l],
            'rwkv_k_k': rwkv_k_k[l], 'rwkv_k_a': rwkv_k_a[l], 'rwkv_r_k': rwkv_r_k[l],
            'rwkv_ln_w': rwkv_ln_w[l], 'rwkv_ln_b': rwkv_ln_b[l],
            'sb_q_norm': sb_q_norm[l], 'sb_k_norm': sb_k_norm[l], 'sb_bias': sb_bias[l],
            's5_a_re': s5_a_re[l], 's5_a_im': s5_a_im[l], 's5_log_dt': s5_log_dt[l],
            's5_b_re': s5_b_re[l], 's5_b_im': s5_b_im[l], 's5_c_re': s5_c_re[l], 's5_c_im': s5_c_im[l],
            's5_d': s5_d[l], 's5_w_glu': s5_w_glu[l],
            'w_branch': w_branch[l], 'w_out': w_out[l],
            'xa_w_q': xa_w_q[l], 'xa_q_norm': xa_q_norm[l], 'xa_w_o': xa_w_o[l],
            'ffn_w_gate': ffn_w_gate[l], 'ffn_w_up': ffn_w_up[l], 'ffn_w_down': ffn_w_down[l],
        }
        mk_p, mv_p = memory_kv(mem_prompt, norm_mem[l], xa_w_k[l], xa_w_v[l], xa_k_norm[l])
        empty = jnp.zeros((bp, 0, SB_HEADS, HEAD_DIM), dt)
        xp, st_p = trunk_layer(
            xp, 0, mk_p, mv_p, empty, empty,
            jnp.zeros((bp, POOL_BUF, D_BR), dt),
            jnp.zeros((bp, RWKV_COLS), dt),
            jnp.zeros((bp, RWKV_HEADS, HEAD_DIM, HEAD_DIM), jnp.float32),
            jnp.zeros((bp, S5_GROUPS, S5_STATE), jnp.float32),
            jnp.zeros((bp, S5_GROUPS, S5_STATE), jnp.float32), lw)
        for n, val in zip(('k', 'v', 'pool', 'shift', 'wkv', 's5r', 's5i'), st_p):
            new_p[n].append(val)
        new_p['mk'].append(mk_p)
        new_p['mv'].append(mv_p)
        past_k = cache_sb_k[l][page_table].reshape(bs, past_len, SB_HEADS, HEAD_DIM)
        past_v = cache_sb_v[l][page_table].reshape(bs, past_len, SB_HEADS, HEAD_DIM)
        xs, st_s = trunk_layer(
            xs, past_len, cache_mem_k[l], cache_mem_v[l], past_k, past_v,
            state_pool[l], state_rwkv_shift[l], state_rwkv_wkv[l],
            state_s5_re[l], state_s5_im[l], lw)
        for n, val in zip(names_s, st_s):
            new_s[n].append(val)
    sb_k_prompt = jnp.stack(new_p['k'], 0)
    sb_v_prompt = jnp.stack(new_p['v'], 0)
    mem_k_prompt = jnp.stack(new_p['mk'], 0)
    mem_v_prompt = jnp.stack(new_p['mv'], 0)
    pool_prompt = jnp.stack(new_p['pool'], 0)
    shift_prompt = jnp.stack(new_p['shift'], 0)
    wkv_prompt = jnp.stack(new_p['wkv'], 0)
    s5_re_prompt = jnp.stack(new_p['s5r'], 0)
    s5_im_prompt = jnp.stack(new_p['s5i'], 0)
    sb_k_sample = jnp.stack(new_s['k'], 0)
    sb_v_sample = jnp.stack(new_s['v'], 0)
    pool_sample = jnp.stack(new_s['pool'], 0)
    shift_sample = jnp.stack(new_s['shift'], 0)
    wkv_sample = jnp.stack(new_s['wkv'], 0)
    s5_re_sample = jnp.stack(new_s['s5r'], 0)
    s5_im_sample = jnp.stack(new_s['s5i'], 0)
    return (xp, xs,
            sb_k_prompt, sb_v_prompt, mem_k_prompt, mem_v_prompt,
            pool_prompt, shift_prompt, wkv_prompt, s5_re_prompt, s5_im_prompt,
            sb_k_sample, sb_v_sample, pool_sample, shift_sample, wkv_sample,
            s5_re_sample, s5_im_sample)
```

```python
import functools
import math

import jax
import jax.numpy as jnp
from jax import lax
from jax.experimental import pallas as pl
from jax.experimental.pallas import tpu as pltpu

F32 = jnp.float32
BF16 = jnp.bfloat16
HIGHEST = lax.Precision.HIGHEST

D_MODEL = 1024
D_BR = 256
HEAD_DIM = 64
N_HEADS = 4
POOL_WINDOWS = (2, 4, 8, 16)
POOL_BUF = 15
POOL_HDR = 16
RWKV_COLS = 1024
RWKV_LN_EPS = 64e-5
S5_GROUPS = 16
S5_STATE = 64
S5_N = S5_GROUPS * S5_STATE
PAGE_SIZE = 128
N_MEM = 256
D_FF = 2816
RMS_EPS = 1e-6
OFF_RWKV = 256
OFF_SB = 1280
OFF_S5 = 2048
OFF_GATE = 2304

VMEM_LIMIT = 56 * 1024 * 1024


def _params(sem):
    return pltpu.CompilerParams(dimension_semantics=sem, vmem_limit_bytes=VMEM_LIMIT)


def _dot(a, b):
    return jnp.dot(a.astype(BF16), b.astype(BF16), preferred_element_type=F32)


def _dot_hi(a, b):
    return jnp.dot(a, b, precision=HIGHEST, preferred_element_type=F32)


def _dot_nt(a, b, precision=None):
    return lax.dot_general(a, b, (((1,), (1,)), ((), ())), precision=precision,
                           preferred_element_type=F32)


def _dot_tn(a, b, precision=None):
    return lax.dot_general(a, b, (((0,), (0,)), ((), ())), precision=precision,
                           preferred_element_type=F32)


def _rms(x, g):
    ms = jnp.mean(x * x, axis=-1, keepdims=True)
    return x * lax.rsqrt(ms + RMS_EPS) * g


def _head_rms(x, gain_row):
    parts = []
    for h in range(N_HEADS):
        xs = x[:, h * HEAD_DIM:(h + 1) * HEAD_DIM]
        ms = jnp.mean(xs * xs, axis=-1, keepdims=True)
        parts.append(xs * lax.rsqrt(ms + RMS_EPS))
    return jnp.concatenate(parts, axis=-1) * gain_row


def _sigmoid(x):
    return 1.0 / (1.0 + jnp.exp(-x))


def _log_sigmoid(z):
    return jnp.minimum(z, 0.0) - jnp.log1p(jnp.exp(-jnp.abs(z)))


def _const_spec(shape):
    nd = len(shape)
    return pl.BlockSpec(shape, lambda *_: (0,) * nd)


def _in_proj_kernel(x_ref, g_ref, wp_ref, wr_ref, wsb_ref, ws5_ref, wg_ref, qg_ref, kg_ref,
                    pool_ref, rwkv_ref, q_ref, k_ref, v_ref, s5_ref, gate_ref):
    h = _rms(x_ref[...], g_ref[...]).astype(BF16)
    pool_ref[...] = jnp.dot(h, wp_ref[...], preferred_element_type=F32)
    rwkv_ref[...] = jnp.dot(h, wr_ref[...], preferred_element_type=F32)
    sb = jnp.dot(h, wsb_ref[...], preferred_element_type=F32)
    q_ref[...] = _head_rms(sb[:, :D_BR], qg_ref[...])
    k_ref[...] = _head_rms(sb[:, D_BR:2 * D_BR], kg_ref[...])
    v_ref[...] = sb[:, 2 * D_BR:]
    s5_ref[...] = jnp.dot(h, ws5_ref[...], preferred_element_type=F32)
    gate_ref[...] = jnp.dot(h, wg_ref[...], preferred_element_type=F32)


def in_proj(x, g, wp, wr, wsb, ws5, wg, qg, kg):
    m = x.shape[0]
    tm = min(m, 256)
    row = lambda n: pl.BlockSpec((tm, n), lambda i: (i, 0))
    widths = (D_BR, RWKV_COLS, D_BR, D_BR, D_BR, D_BR, 4 * D_MODEL)
    return pl.pallas_call(
        _in_proj_kernel,
        grid=(m // tm,),
        in_specs=[row(D_MODEL), _const_spec(g.shape), _const_spec(wp.shape), _const_spec(wr.shape),
                  _const_spec(wsb.shape), _const_spec(ws5.shape), _const_spec(wg.shape),
                  _const_spec(qg.shape), _const_spec(kg.shape)],
        out_specs=[row(n) for n in widths],
        out_shape=[jax.ShapeDtypeStruct((m, n), F32) for n in widths],
        compiler_params=_params(("arbitrary",)),
        name="in_proj",
    )(x, g, wp, wr, wsb, ws5, wg, qg, kg)


def _pool_kernel(u_ref, buf_ref, w_ref, scale_ref, o_ref, new_ref, ext_ref, *, tt, pos0):
    t = pl.program_id(1)

    @pl.when(t == 0)
    def _():
        ext_ref[0:POOL_HDR, :] = buf_ref[0]

    @pl.when(t > 0)
    def _():
        ext_ref[0:POOL_HDR, :] = ext_ref[tt:tt + POOL_HDR, :]

    u = u_ref[0]
    ext_ref[POOL_HDR:POOL_HDR + tt, :] = u
    lane = lax.broadcasted_iota(jnp.int32, (tt, D_BR), 1)
    pos = pos0 + t * tt + lax.broadcasted_iota(jnp.int32, (tt, D_BR), 0)
    s = u
    win_sum = jnp.zeros((tt, D_BR), F32)
    count = jnp.zeros((tt, D_BR), F32)
    for i in range(1, POOL_WINDOWS[-1]):
        s = s + ext_ref[POOL_HDR - i:POOL_HDR - i + tt, :]
        w = i + 1
        if w in POOL_WINDOWS:
            gi = POOL_WINDOWS.index(w)
            sel = (lane >= gi * HEAD_DIM) & (lane < (gi + 1) * HEAD_DIM)
            win_sum = jnp.where(sel, s, win_sum)
            count = jnp.where(sel, jnp.minimum(pos + 1, w).astype(F32), count)
    pooled = win_sum / count - u
    o_ref[0] = _dot_hi(pooled, w_ref[...]) * scale_ref[...]
    new_ref[0] = ext_ref[tt + 1:tt + POOL_HDR, :]


def pool_mixer(u, buf, w_bd, scale, pos0):
    b, t, _ = u.shape
    tt = min(t, 512)
    kern = functools.partial(_pool_kernel, tt=tt, pos0=pos0)
    return pl.pallas_call(
        kern,
        grid=(b, t // tt),
        in_specs=[pl.BlockSpec((1, tt, D_BR), lambda i, j: (i, j, 0)),
                  pl.BlockSpec((1, POOL_HDR, D_BR), lambda i, j: (i, 0, 0)),
                  _const_spec(w_bd.shape), _const_spec(scale.shape)],
        out_specs=[pl.BlockSpec((1, tt, D_BR), lambda i, j: (i, j, 0)),
                   pl.BlockSpec((1, POOL_BUF, D_BR), lambda i, j: (i, 0, 0))],
        out_shape=[jax.ShapeDtypeStruct((b, t, D_BR), F32),
                   jax.ShapeDtypeStruct((b, POOL_BUF, D_BR), F32)],
        scratch_shapes=[pltpu.VMEM((POOL_HDR + tt, D_BR), F32)],
        compiler_params=_params(("arbitrary", "arbitrary")),
        name="pool_mixer",
    )(u, buf, w_bd, scale)


def _rwkv_pre_kernel(p_ref, shift_ref, mu_ref, w0_ref, wup_ref, a0_ref, aup_ref, gup_ref,
                     kk_ref, ka_ref,
                     r_out, k_out, v_out, ld_out, kk_out, b_out, g_out, last_ref, *, tt):
    t = pl.program_id(1)

    @pl.when(t == 0)
    def _():
        last_ref[...] = shift_ref[0]

    p = p_ref[0]
    rowi = lax.broadcasted_iota(jnp.int32, (tt, RWKV_COLS), 0)
    prev = jnp.where(rowi == 0, last_ref[...], pltpu.roll(p, 1, 0))
    last_ref[...] = p[tt - 1:tt, :]
    pm = p + (prev - p) * mu_ref[...]
    r = pm[:, 0:D_BR]
    k = pm[:, D_BR:2 * D_BR]
    v = pm[:, 2 * D_BR:3 * D_BR]
    w_lo = pm[:, 768:832]
    a_lo = pm[:, 832:896]
    g_lo = pm[:, 896:1024]
    wz = w0_ref[...] + _dot(jnp.tanh(w_lo), wup_ref[...])
    w_log = _log_sigmoid(wz) - 0.5
    a = _sigmoid(a0_ref[...] + _dot(a_lo, aup_ref[...]))
    g = _dot(_sigmoid(g_lo), gup_ref[...])
    kk = k * kk_ref[...]
    parts = []
    for h in range(N_HEADS):
        ks = kk[:, h * HEAD_DIM:(h + 1) * HEAD_DIM]
        ss = jnp.sum(ks * ks, axis=-1, keepdims=True)
        parts.append(ks * lax.rsqrt(jnp.maximum(ss, 1e-24)))
    kk = jnp.concatenate(parts, axis=-1)
    r_out[0] = r
    k_out[0] = k * (1.0 + (a - 1.0) * ka_ref[...])
    v_out[0] = v
    ld_out[0] = -jnp.exp(w_log)
    kk_out[0] = kk
    b_out[0] = kk * a
    g_out[0] = g


def rwkv_pre(p, shift, mu, w0, wup, a0, aup, gup, k_k, k_a):
    b, t, _ = p.shape
    tt = min(t, 512)
    kern = functools.partial(_rwkv_pre_kernel, tt=tt)
    blk = pl.BlockSpec((1, tt, D_BR), lambda i, j: (i, j, 0))
    consts = (mu, w0, wup, a0, aup, gup, k_k, k_a)
    return pl.pallas_call(
        kern,
        grid=(b, t // tt),
        in_specs=[pl.BlockSpec((1, tt, RWKV_COLS), lambda i, j: (i, j, 0)),
                  pl.BlockSpec((1, 1, RWKV_COLS), lambda i, j: (i, 0, 0))]
                 + [_const_spec(c.shape) for c in consts],
        out_specs=[blk] * 7,
        out_shape=[jax.ShapeDtypeStruct((b, t, D_BR), F32)] * 7,
        scratch_shapes=[pltpu.VMEM((1, RWKV_COLS), F32)],
        compiler_params=_params(("arbitrary", "arbitrary")),
        name="rwkv_pre",
    )(p, shift, *consts)


def _rwkv_chunk_kernel(r_ref, k_ref, v_ref, ld_ref, kk_ref, b_ref, g_ref, s0_ref,
                       lnw_ref, lnb_ref, rk_ref, o_ref, sout_ref, s_ref, *, tt, c):
    t = pl.program_id(1)

    @pl.when(t == 0)
    def _():
        s_ref[...] = s0_ref[0]

    ri = lax.broadcasted_iota(jnp.int32, (c, c), 0)
    ci = lax.broadcasted_iota(jnp.int32, (c, c), 1)
    incl = (ri >= ci).astype(F32)
    strict = (ri > ci).astype(F32)
    eye = (ri == ci).astype(F32)
    n_sq = int(round(math.log2(c))) - 1

    for ch in range(tt // c):
        rows = slice(ch * c, (ch + 1) * c)
        ld_all = ld_ref[0, rows, :]
        l_inc_all = _dot_hi(incl, ld_all)
        outs = []
        for h in range(N_HEADS):
            cols = slice(h * HEAD_DIM, (h + 1) * HEAD_DIM)
            ld = ld_all[:, cols]
            l_inc = l_inc_all[:, cols]
            l_exc = l_inc - ld
            l_tot = l_inc[c - 1:c, :]
            r = r_ref[0, rows, cols]
            k = k_ref[0, rows, cols]
            v = v_ref[0, rows, cols]
            kk = kk_ref[0, rows, cols]
            bv = b_ref[0, rows, cols]
            e_neg = jnp.exp(-l_inc)
            e_end = jnp.exp(l_tot - l_inc)
            a_t = -kk * jnp.exp(l_exc)
            r_t = r * jnp.exp(l_inc)
            x1 = jnp.concatenate([a_t, r_t], axis=0)
            x2 = jnp.concatenate([bv * e_neg, k * e_neg], axis=0)
            gram = _dot_nt(x1, x2, HIGHEST)
            m_ab = gram[:c, :c] * strict
            m_ak = gram[:c, c:] * strict
            p_rb = gram[c:, :c] * incl
            p_rk = gram[c:, c:] * incl
            pw = m_ab
            tinv = eye + m_ab
            for _ in range(n_sq):
                pw = _dot_hi(pw, pw)
                tinv = tinv + _dot_hi(tinv, pw)
            wu = _dot_hi(tinv, jnp.concatenate([a_t, _dot_hi(m_ak, v)], axis=1))
            o0 = _dot_hi(p_rk, v)
            s_h = s_ref[h * HEAD_DIM:(h + 1) * HEAD_DIM, :]
            y = _dot_nt(jnp.concatenate([wu[:, :HEAD_DIM], r_t], axis=0), s_h, HIGHEST)
            u = y[:c] + wu[:, HEAD_DIM:]
            o = y[c:] + _dot_hi(p_rb, u) + o0
            s_new = s_h * jnp.exp(l_tot) + _dot_tn(
                jnp.concatenate([u, v], axis=0),
                jnp.concatenate([bv * e_end, k * e_end], axis=0), HIGHEST)
            s_ref[h * HEAD_DIM:(h + 1) * HEAD_DIM, :] = s_new
            mean = jnp.mean(o, axis=-1, keepdims=True)
            var = jnp.mean(jnp.square(o - mean), axis=-1, keepdims=True)
            o = (o - mean) * lax.rsqrt(var + RWKV_LN_EPS) * lnw_ref[:, cols] + lnb_ref[:, cols]
            bonus = jnp.sum(r * k * rk_ref[:, cols], axis=-1, keepdims=True) * v
            outs.append(o + bonus)
        o_ref[0, rows, :] = jnp.concatenate(outs, axis=-1) * g_ref[0, rows, :]
    sout_ref[0] = s_ref[...]


def rwkv_chunk(r, k, v, ld, kk, bv, g, s0, ln_w, ln_b, r_k):
    b, t, _ = r.shape
    c = min(t, 64)
    tt = min(t, 256)
    kern = functools.partial(_rwkv_chunk_kernel, tt=tt, c=c)
    blk = pl.BlockSpec((1, tt, D_BR), lambda i, j: (i, j, 0))
    sblk = pl.BlockSpec((1, D_BR, HEAD_DIM), lambda i, j: (i, 0, 0))
    return pl.pallas_call(
        kern,
        grid=(b, t // tt),
        in_specs=[blk] * 7 + [sblk, _const_spec(ln_w.shape), _const_spec(ln_b.shape),
                              _const_spec(r_k.shape)],
        out_specs=[blk, sblk],
        out_shape=[jax.ShapeDtypeStruct((b, t, D_BR), F32),
                   jax.ShapeDtypeStruct((b, D_BR, HEAD_DIM), F32)],
        scratch_shapes=[pltpu.VMEM((D_BR, HEAD_DIM), F32)],
        compiler_params=_params(("arbitrary", "arbitrary")),
        name="rwkv_chunk",
    )(r, k, v, ld, kk, bv, g, s0, ln_w, ln_b, r_k)


def _sb_prompt_kernel(bias_ref, q_ref, k_ref, v_ref, o_ref, *, tq):
    h = pl.program_id(0)
    i = pl.program_id(1)
    q = q_ref[0].astype(BF16)
    bias = bias_ref[h]
    ri = lax.broadcasted_iota(jnp.int32, (tq, tq), 0)
    ci = lax.broadcasted_iota(jnp.int32, (tq, tq), 1)
    later = (ri > ci).astype(BF16)
    scale = HEAD_DIM ** -0.5

    def body(jj, carry):
        after_c, acc = carry
        j = i - jj
        start = pl.multiple_of(j * tq, tq)
        kb = k_ref[0, pl.ds(start, tq), :]
        vb = v_ref[0, pl.ds(start, tq), :]
        z = _dot_nt(q, kb) * scale + bias
        mask = (j * tq + ci) < (i * tq + ri)
        ls = _log_sigmoid(z)
        l1m = jnp.where(mask, ls - z, 0.0)
        hi = l1m.astype(BF16)
        lo = (l1m - hi.astype(F32)).astype(BF16)
        after = (jnp.dot(hi, later, preferred_element_type=F32)
                 + jnp.dot(lo, later, preferred_element_type=F32) + after_c)
        att = jnp.where(mask, jnp.exp(ls + after), 0.0)
        acc = acc + jnp.dot(att.astype(BF16), vb, preferred_element_type=F32)
        after_c = after_c + jnp.sum(l1m, axis=-1, keepdims=True)
        return after_c, acc

    init = (jnp.zeros((tq, 1), F32), jnp.zeros((tq, HEAD_DIM), F32))
    _, acc = lax.fori_loop(0, i + 1, body, init)
    o_ref[0] = acc


def sb_prompt(q, k, v, bias):
    h, t, _ = q.shape
    tq = min(t, 256)
    kern = functools.partial(_sb_prompt_kernel, tq=tq)
    return pl.pallas_call(
        kern,
        grid=(h, t // tq),
        in_specs=[pl.BlockSpec(memory_space=pltpu.SMEM),
                  pl.BlockSpec((1, tq, HEAD_DIM), lambda a, i: (a, i, 0)),
                  pl.BlockSpec((1, t, HEAD_DIM), lambda a, i: (a, 0, 0)),
                  pl.BlockSpec((1, t, HEAD_DIM), lambda a, i: (a, 0, 0))],
        out_specs=pl.BlockSpec((1, tq, HEAD_DIM), lambda a, i: (a, i, 0)),
        out_shape=jax.ShapeDtypeStruct((h, t, HEAD_DIM), F32),
        compiler_params=_params(("arbitrary", "arbitrary")),
        name="sb_prompt",
    )(bias, q, k, v)


PAGES_PER_STEP = 8


def _sb_paged_kernel(pt_ref, q_ref, bias_ref, kn_ref, vn_ref, *rest, t_new, past_len):
    nk = PAGES_PER_STEP
    k_pages = rest[:nk]
    v_pages = rest[nk:2 * nk]
    o_ref = rest[2 * nk]
    acc_ref, car_ref = rest[2 * nk + 1:]
    s = pl.program_id(1)
    n_rows = N_HEADS * t_new
    rowi = lax.broadcasted_iota(jnp.int32, (n_rows, D_BR), 0)
    lanei = lax.broadcasted_iota(jnp.int32, (n_rows, D_BR), 1)
    head_mask = None
    for hh in range(N_HEADS):
        m_h = ((rowi >= hh * t_new) & (rowi < (hh + 1) * t_new)
               & (lanei >= hh * HEAD_DIM) & (lanei < (hh + 1) * HEAD_DIM))
        head_mask = m_h if head_mask is None else head_mask | m_h
    q = q_ref[0]
    qrows = jnp.where(head_mask, jnp.concatenate([q] * N_HEADS, axis=0), 0.0).astype(BF16)
    bias = bias_ref[...]
    ri = lax.broadcasted_iota(jnp.int32, (PAGE_SIZE, PAGE_SIZE), 0)
    ci = lax.broadcasted_iota(jnp.int32, (PAGE_SIZE, PAGE_SIZE), 1)
    later = (ri > ci).astype(BF16)
    q_pos = past_len + jnp.concatenate(
        [lax.broadcasted_iota(jnp.int32, (t_new, PAGE_SIZE), 0)] * N_HEADS, axis=0)
    coli = lax.broadcasted_iota(jnp.int32, (n_rows, PAGE_SIZE), 1)
    scale = HEAD_DIM ** -0.5

    def block(kb, vb, key_pos0):
        z = _dot_nt(qrows, kb.astype(BF16)) * scale + bias
        mask = (key_pos0 + coli) < q_pos
        ls = _log_sigmoid(z)
        l1m = jnp.where(mask, ls - z, 0.0)
        hi = l1m.astype(BF16)
        lo = (l1m - hi.astype(F32)).astype(BF16)
        after = (jnp.dot(hi, later, preferred_element_type=F32)
                 + jnp.dot(lo, later, preferred_element_type=F32) + car_ref[...])
        att = jnp.where(mask, jnp.exp(ls + after), 0.0)
        acc_ref[...] += jnp.dot(att.astype(BF16), vb.astype(BF16), preferred_element_type=F32)
        car_ref[...] += jnp.sum(l1m, axis=-1, keepdims=True)

    @pl.when(s == 0)
    def _():
        acc_ref[...] = jnp.zeros_like(acc_ref)
        car_ref[...] = jnp.zeros_like(car_ref)
        block(kn_ref[0], vn_ref[0], past_len)

    n_pages = past_len // PAGE_SIZE
    for rr in range(nk):
        page = n_pages - 1 - (s * nk + rr)
        block(k_pages[rr][0, 0], v_pages[rr][0, 0], page * PAGE_SIZE)

    @pl.when(s == pl.num_programs(1) - 1)
    def _():
        acc = jnp.where(head_mask, acc_ref[...], 0.0)
        out = acc[0:t_new]
        for hh in range(1, N_HEADS):
            out = out + acc[hh * t_new:(hh + 1) * t_new]
        o_ref[0] = out


def sb_paged(q, bias_rows, k_new, v_new, cache_k, cache_v, page_table, layer):
    b, t_new, _ = q.shape
    n_pages = page_table.shape[1]
    past_len = n_pages * PAGE_SIZE
    nk = PAGES_PER_STEP
    n_steps = n_pages // nk
    kern = functools.partial(_sb_paged_kernel, t_new=t_new, past_len=past_len)

    def page_spec(rr):
        return pl.BlockSpec(
            (1, 1, PAGE_SIZE, D_BR),
            lambda i, s, pt: (layer, pt[i, n_pages - 1 - (s * nk + rr)], 0, 0))

    n_rows = N_HEADS * t_new
    grid_spec = pltpu.PrefetchScalarGridSpec(
        num_scalar_prefetch=1,
        grid=(b, n_steps),
        in_specs=[pl.BlockSpec((1, t_new, D_BR), lambda i, s, pt: (i, 0, 0)),
                  pl.BlockSpec((n_rows, 1), lambda i, s, pt: (0, 0)),
                  pl.BlockSpec((1, PAGE_SIZE, D_BR), lambda i, s, pt: (i, 0, 0)),
                  pl.BlockSpec((1, PAGE_SIZE, D_BR), lambda i, s, pt: (i, 0, 0))]
                 + [page_spec(rr) for rr in range(nk)] * 2,
        out_specs=pl.BlockSpec((1, t_new, D_BR), lambda i, s, pt: (i, 0, 0)),
        scratch_shapes=[pltpu.VMEM((n_rows, D_BR), F32), pltpu.VMEM((n_rows, 1), F32)],
    )
    return pl.pallas_call(
        kern,
        grid_spec=grid_spec,
        out_shape=jax.ShapeDtypeStruct((b, t_new, D_BR), F32),
        compiler_params=_params(("arbitrary", "arbitrary")),
        name="sb_paged",
    )(page_table, q, bias_rows, k_new, v_new, *([cache_k] * nk), *([cache_v] * nk))


def _s5_kernel(u_ref, hre_ref, him_ref, bre_ref, bim_ref, cre_ref, cim_ref, d_ref, wglu_ref,
               powr_ref, powi_ref, lvr_ref, lvi_ref,
               o_ref, sre_ref, sim_ref, cr_ref, cim_s_ref, *, tt):
    t = pl.program_id(1)

    @pl.when(t == 0)
    def _():
        cr_ref[...] = hre_ref[0]
        cim_s_ref[...] = him_ref[0]

    u = u_ref[0]
    xr = _dot(u, bre_ref[...])
    xi = _dot(u, bim_ref[...])
    rowi = lax.broadcasted_iota(jnp.int32, (tt, S5_N), 0)
    off = 1
    lvl = 0
    while off < tt:
        lr = lvr_ref[lvl:lvl + 1, :]
        li = lvi_ref[lvl:lvl + 1, :]
        m = rowi >= off
        sr = jnp.where(m, pltpu.roll(xr, off, 0), 0.0)
        si = jnp.where(m, pltpu.roll(xi, off, 0), 0.0)
        xr, xi = xr + lr * sr - li * si, xi + lr * si + li * sr
        off *= 2
        lvl += 1
    h0r = cr_ref[...]
    h0i = cim_s_ref[...]
    pr = powr_ref[...]
    pi = powi_ref[...]
    xr, xi = xr + pr * h0r - pi * h0i, xi + pr * h0i + pi * h0r
    cr_ref[...] = xr[tt - 1:tt, :]
    cim_s_ref[...] = xi[tt - 1:tt, :]
    sre_ref[0] = xr[tt - 1:tt, :]
    sim_ref[0] = xi[tt - 1:tt, :]
    y = _dot(xr, cre_ref[...]) - _dot(xi, cim_ref[...]) + d_ref[...] * u
    z = _dot(jax.nn.gelu(y), wglu_ref[...])
    o_ref[0] = z[:, :D_BR] * _sigmoid(z[:, D_BR:])


def s5_mixer(u, h_re, h_im, b_re, b_im, c_re, c_im, d, w_glu, pow_re, pow_im, lv_re, lv_im):
    b, t, _ = u.shape
    tt = pow_re.shape[0]
    kern = functools.partial(_s5_kernel, tt=tt)
    consts = (b_re, b_im, c_re, c_im, d, w_glu, pow_re, pow_im, lv_re, lv_im)
    sblk = pl.BlockSpec((1, 1, S5_N), lambda i, j: (i, 0, 0))
    return pl.pallas_call(
        kern,
        grid=(b, t // tt),
        in_specs=[pl.BlockSpec((1, tt, D_BR), lambda i, j: (i, j, 0)), sblk, sblk]
                 + [_const_spec(c.shape) for c in consts],
        out_specs=[pl.BlockSpec((1, tt, D_BR), lambda i, j: (i, j, 0)), sblk, sblk],
        out_shape=[jax.ShapeDtypeStruct((b, t, D_BR), F32),
                   jax.ShapeDtypeStruct((b, 1, S5_N), F32),
                   jax.ShapeDtypeStruct((b, 1, S5_N), F32)],
        scratch_shapes=[pltpu.VMEM((1, S5_N), F32), pltpu.VMEM((1, S5_N), F32)],
        compiler_params=_params(("arbitrary", "arbitrary")),
        name="s5_mixer",
    )(u, h_re, h_im, *consts)


def _merge_kernel(x_ref, bp_ref, br_ref, bs_ref, b5_ref, gate_ref, wb_ref, wo_ref, o_ref):
    merged = None
    for n, b_ref in enumerate((bp_ref, br_ref, bs_ref, b5_ref)):
        lifted = _dot(b_ref[...], wb_ref[n])
        term = _sigmoid(gate_ref[:, n * D_MODEL:(n + 1) * D_MODEL]) * lifted
        merged = term if merged is None else merged + term
    o_ref[...] = x_ref[...] + _dot(merged, wo_ref[...])


def merge(x, o_pool, o_rwkv, o_sb, o_s5, gates, w_branch, w_out):
    m = x.shape[0]
    tm = min(m, 256)
    row = lambda n: pl.BlockSpec((tm, n), lambda i: (i, 0))
    return pl.pallas_call(
        _merge_kernel,
        grid=(m // tm,),
        in_specs=[row(D_MODEL), row(D_BR), row(D_BR), row(D_BR), row(D_BR), row(4 * D_MODEL),
                  _const_spec(w_branch.shape), _const_spec(w_out.shape)],
        out_specs=row(D_MODEL),
        out_shape=jax.ShapeDtypeStruct((m, D_MODEL), F32),
        compiler_params=_params(("arbitrary",)),
        name="merge",
    )(x, o_pool, o_rwkv, o_sb, o_s5, gates, w_branch, w_out)


def _xattn_kernel(x_ref, mk_ref, mv_ref, g_ref, wq_ref, qg_ref, wo_ref, o_ref):
    x = x_ref[0]
    hn = _rms(x, g_ref[...])
    q = _head_rms(_dot(hn, wq_ref[...]), qg_ref[...])
    mk = mk_ref[0]
    mv = mv_ref[0]
    scale = HEAD_DIM ** -0.5
    outs = []
    for h in range(N_HEADS):
        cols = slice(h * HEAD_DIM, (h + 1) * HEAD_DIM)
        s = _dot_nt(q[:, cols].astype(BF16), mk[:, cols].astype(BF16)) * scale
        s = s - jnp.max(s, axis=-1, keepdims=True)
        e = jnp.exp(s)
        pr = e / jnp.sum(e, axis=-1, keepdims=True)
        outs.append(_dot(pr, mv[:, cols]))
    o = jnp.concatenate(outs, axis=-1)
    o_ref[0] = x + _dot(o, wo_ref[...])


def cross_attention(x, mem_k, mem_v, g, wq, qg, wo):
    b, t, _ = x.shape
    tt = min(t, 512)
    return pl.pallas_call(
        _xattn_kernel,
        grid=(b, t // tt),
        in_specs=[pl.BlockSpec((1, tt, D_MODEL), lambda i, j: (i, j, 0)),
                  pl.BlockSpec((1, N_MEM, D_BR), lambda i, j: (i, 0, 0)),
                  pl.BlockSpec((1, N_MEM, D_BR), lambda i, j: (i, 0, 0)),
                  _const_spec(g.shape), _const_spec(wq.shape), _const_spec(qg.shape),
                  _const_spec(wo.shape)],
        out_specs=pl.BlockSpec((1, tt, D_MODEL), lambda i, j: (i, j, 0)),
        out_shape=jax.ShapeDtypeStruct((b, t, D_MODEL), F32),
        compiler_params=_params(("arbitrary", "arbitrary")),
        name="cross_attention",
    )(x, mem_k, mem_v, g, wq, qg, wo)


def _memkv_kernel(mem_ref, g_ref, wk_ref, wv_ref, kg_ref, k_ref, v_ref):
    mn = _rms(mem_ref[...], g_ref[...])
    k_ref[...] = _head_rms(_dot(mn, wk_ref[...]), kg_ref[...])
    v_ref[...] = _dot(mn, wv_ref[...])


def memory_kv(mem, g, wk, wv, kg):
    m = mem.shape[0]
    args = (mem, g, wk, wv, kg)
    return pl.pallas_call(
        _memkv_kernel,
        grid=(1,),
        in_specs=[_const_spec(a.shape) for a in args],
        out_specs=[_const_spec((m, D_BR))] * 2,
        out_shape=[jax.ShapeDtypeStruct((m, D_BR), F32)] * 2,
        compiler_params=_params(("arbitrary",)),
        name="memory_kv",
    )(*args)


def _ffn_kernel(x_ref, g_ref, wg_ref, wu_ref, wd_ref, o_ref):
    x = x_ref[...]
    hn = _rms(x, g_ref[...]).astype(BF16)
    a = jnp.dot(hn, wg_ref[...], preferred_element_type=F32)
    bq = jnp.dot(hn, wu_ref[...], preferred_element_type=F32)
    act = a * _sigmoid(a) * bq
    o_ref[...] = x + _dot(act, wd_ref[...])


def ffn(x, g, w_gate, w_up, w_down):
    m = x.shape[0]
    tm = min(m, 256)
    row = pl.BlockSpec((tm, D_MODEL), lambda i: (i, 0))
    return pl.pallas_call(
        _ffn_kernel,
        grid=(m // tm,),
        in_specs=[row, _const_spec(g.shape), _const_spec(w_gate.shape), _const_spec(w_up.shape),
                  _const_spec(w_down.shape)],
        out_specs=row,
        out_shape=jax.ShapeDtypeStruct((m, D_MODEL), F32),
        compiler_params=_params(("arbitrary",)),
        name="ffn",
    )(x, g, w_gate, w_up, w_down)


def _block_diag(blocks):
    g, m, n = blocks.shape
    eye = jnp.eye(g, dtype=blocks.dtype)
    return (eye[:, None, :, None] * blocks[:, :, None, :]).reshape(g * m, g * n)


def _tile_heads(v):
    return jnp.tile(v, N_HEADS).reshape(1, N_HEADS * v.shape[0])


def _cmul(ar, ai, br, bi):
    return ar * br - ai * bi, ar * bi + ai * br


def _s5_constants(a_re, a_im, log_dt, b_re, b_im, c_re, c_im, tt):
    dt_g = jnp.exp(log_dt)[:, None]
    mag = jnp.exp(dt_g * a_re)
    ab_re, ab_im = mag * jnp.cos(dt_g * a_im), mag * jnp.sin(dt_g * a_im)
    den = a_re * a_re + a_im * a_im
    n_re = ab_re - 1.0
    f_re = (n_re * a_re + ab_im * a_im) / den
    f_im = (ab_im * a_re - n_re * a_im) / den
    bb_re = f_re[..., None] * b_re - f_im[..., None] * b_im
    bb_im = f_re[..., None] * b_im + f_im[..., None] * b_re
    bmat_re = _block_diag(jnp.swapaxes(bb_re, 1, 2))
    bmat_im = _block_diag(jnp.swapaxes(bb_im, 1, 2))
    cmat_re = _block_diag(jnp.swapaxes(c_re, 1, 2))
    cmat_im = _block_diag(jnp.swapaxes(c_im, 1, 2))
    lam_re = ab_re.reshape(1, S5_N)
    lam_im = ab_im.reshape(1, S5_N)
    pow_re, pow_im = lam_re, lam_im
    lv_re, lv_im = [lam_re], [lam_im]
    cur_re, cur_im = lam_re, lam_im
    n = 1
    while n < tt:
        nr, ni = _cmul(pow_re, pow_im, cur_re, cur_im)
        pow_re = jnp.concatenate([pow_re, nr], axis=0)
        pow_im = jnp.concatenate([pow_im, ni], axis=0)
        cur_re, cur_im = _cmul(cur_re, cur_im, cur_re, cur_im)
        lv_re.append(cur_re)
        lv_im.append(cur_im)
        n *= 2
    n_lv = max(len(lv_re) - 1, 1)
    lv_re = jnp.concatenate(lv_re[:n_lv], axis=0)
    lv_im = jnp.concatenate(lv_im[:n_lv], axis=0)
    return (bmat_re.astype(BF16), bmat_im.astype(BF16), cmat_re.astype(BF16), cmat_im.astype(BF16),
            pow_re, pow_im, lv_re, lv_im)


def _layer_weights(l, w, tt_prompt, tt_sample):
    w_in = w['w_in'][l]
    lw = {
        'norm_mix': w['norm_mix'][l].reshape(1, D_MODEL),
        'norm_cross': w['norm_cross'][l].reshape(1, D_MODEL),
        'norm_mem': w['norm_mem'][l].reshape(1, D_MODEL),
        'norm_ffn': w['norm_ffn'][l].reshape(1, D_MODEL),
        'w_pool_in': w_in[:, :OFF_RWKV].astype(BF16),
        'w_rwkv_in': w_in[:, OFF_RWKV:OFF_SB].astype(BF16),
        'w_sb_in': w_in[:, OFF_SB:OFF_S5].astype(BF16),
        'w_s5_in': w_in[:, OFF_S5:OFF_GATE].astype(BF16),
        'w_gate_in': w_in[:, OFF_GATE:].astype(BF16),
        'sb_q_norm': _tile_heads(w['sb_q_norm'][l]),
        'sb_k_norm': _tile_heads(w['sb_k_norm'][l]),
        'sb_bias': w['sb_bias'][l],
        'pool_w': _block_diag(w['pool_w'][l]),
        'pool_scale': w['pool_scale'][l].reshape(1, D_BR),
        'rwkv_mu': w['rwkv_mu'][l].reshape(1, RWKV_COLS),
        'rwkv_w0': w['rwkv_w0'][l].reshape(1, D_BR),
        'rwkv_w_up': w['rwkv_w_up'][l].astype(BF16),
        'rwkv_a0': w['rwkv_a0'][l].reshape(1, D_BR),
        'rwkv_a_up': w['rwkv_a_up'][l].astype(BF16),
        'rwkv_g_up': w['rwkv_g_up'][l].astype(BF16),
        'rwkv_k_k': w['rwkv_k_k'][l].reshape(1, D_BR),
        'rwkv_k_a': w['rwkv_k_a'][l].reshape(1, D_BR),
        'rwkv_r_k': w['rwkv_r_k'][l].reshape(1, D_BR),
        'rwkv_ln_w': w['rwkv_ln_w'][l].reshape(1, D_BR),
        'rwkv_ln_b': w['rwkv_ln_b'][l].reshape(1, D_BR),
        's5_d': w['s5_d'][l].reshape(1, D_BR),
        's5_w_glu': w['s5_w_glu'][l].astype(BF16),
        'w_branch': w['w_branch'][l].astype(BF16),
        'w_out': w['w_out'][l].astype(BF16),
        'xa_w_q': w['xa_w_q'][l].astype(BF16),
        'xa_w_k': w['xa_w_k'][l].astype(BF16),
        'xa_w_v': w['xa_w_v'][l].astype(BF16),
        'xa_q_norm': _tile_heads(w['xa_q_norm'][l]),
        'xa_k_norm': _tile_heads(w['xa_k_norm'][l]),
        'xa_w_o': w['xa_w_o'][l].astype(BF16),
        'ffn_w_gate': w['ffn_w_gate'][l].astype(BF16),
        'ffn_w_up': w['ffn_w_up'][l].astype(BF16),
        'ffn_w_down': w['ffn_w_down'][l].astype(BF16),
    }
    s5_args = (w['s5_a_re'][l], w['s5_a_im'][l], w['s5_log_dt'][l], w['s5_b_re'][l], w['s5_b_im'][l],
               w['s5_c_re'][l], w['s5_c_im'][l])
    lw['s5_prompt'] = _s5_constants(*s5_args, tt_prompt)
    lw['s5_sample'] = _s5_constants(*s5_args, tt_sample)
    return lw


def _trunk_layer(x, pos0, mem_k, mem_v, sb_fn, pool_buf, shift, wkv, s5_re, s5_im, lw, s5c):
    b, t, _ = x.shape
    m = b * t
    xf = x.reshape(m, D_MODEL)
    u_pool, p_rwkv, q, k, v, u_s5, gates = in_proj(
        xf, lw['norm_mix'], lw['w_pool_in'], lw['w_rwkv_in'], lw['w_sb_in'], lw['w_s5_in'],
        lw['w_gate_in'], lw['sb_q_norm'], lw['sb_k_norm'])
    o_pool, new_pool = pool_mixer(u_pool.reshape(b, t, D_BR), pool_buf, lw['pool_w'],
                                  lw['pool_scale'], pos0)
    p3 = p_rwkv.reshape(b, t, RWKV_COLS)
    pre = rwkv_pre(p3, shift, lw['rwkv_mu'], lw['rwkv_w0'], lw['rwkv_w_up'], lw['rwkv_a0'],
                   lw['rwkv_a_up'], lw['rwkv_g_up'], lw['rwkv_k_k'], lw['rwkv_k_a'])
    o_rwkv, new_wkv = rwkv_chunk(*pre, wkv, lw['rwkv_ln_w'], lw['rwkv_ln_b'], lw['rwkv_r_k'])
    new_shift = p3[:, -1]
    q3, k3, v3 = (z.reshape(b, t, D_BR) for z in (q, k, v))
    o_sb = sb_fn(q3, k3, v3)
    o_s5, new_re, new_im = s5_mixer(u_s5.reshape(b, t, D_BR), s5_re, s5_im, s5c[0], s5c[1], s5c[2],
                                    s5c[3], lw['s5_d'], lw['s5_w_glu'], *s5c[4:])
    x1 = merge(xf, o_pool.reshape(m, D_BR), o_rwkv.reshape(m, D_BR), o_sb.reshape(m, D_BR),
               o_s5.reshape(m, D_BR), gates, lw['w_branch'], lw['w_out'])
    x2 = cross_attention(x1.reshape(b, t, D_MODEL), mem_k, mem_v, lw['norm_cross'], lw['xa_w_q'],
                         lw['xa_q_norm'], lw['xa_w_o'])
    x3 = ffn(x2.reshape(m, D_MODEL), lw['norm_ffn'], lw['ffn_w_gate'], lw['ffn_w_up'],
             lw['ffn_w_down'])
    return x3.reshape(b, t, D_MODEL), (k3, v3, new_pool, new_shift, new_wkv, new_re, new_im)


def kernel(x_prompt, x_sample, cache_sb_k, cache_sb_v, cache_mem_k, cache_mem_v, state_pool, state_rwkv_shift, state_rwkv_wkv, state_s5_re, state_s5_im, page_table, mem_prompt, norm_mix, norm_cross, norm_mem, norm_ffn, w_in, pool_w, pool_scale, rwkv_mu, rwkv_w0, rwkv_w_up, rwkv_a0, rwkv_a_up, rwkv_g_up, rwkv_k_k, rwkv_k_a, rwkv_r_k, rwkv_ln_w, rwkv_ln_b, sb_q_norm, sb_k_norm, sb_bias, s5_a_re, s5_a_im, s5_log_dt, s5_b_re, s5_b_im, s5_c_re, s5_c_im, s5_d, s5_w_glu, w_branch, w_out, xa_w_q, xa_w_k, xa_w_v, xa_q_norm, xa_k_norm, xa_w_o, ffn_w_gate, ffn_w_up, ffn_w_down):
    weights = dict(
        norm_mix=norm_mix, norm_cross=norm_cross, norm_mem=norm_mem, norm_ffn=norm_ffn, w_in=w_in,
        pool_w=pool_w, pool_scale=pool_scale, rwkv_mu=rwkv_mu, rwkv_w0=rwkv_w0, rwkv_w_up=rwkv_w_up,
        rwkv_a0=rwkv_a0, rwkv_a_up=rwkv_a_up, rwkv_g_up=rwkv_g_up, rwkv_k_k=rwkv_k_k,
        rwkv_k_a=rwkv_k_a, rwkv_r_k=rwkv_r_k, rwkv_ln_w=rwkv_ln_w, rwkv_ln_b=rwkv_ln_b,
        sb_q_norm=sb_q_norm, sb_k_norm=sb_k_norm, sb_bias=sb_bias, s5_a_re=s5_a_re, s5_a_im=s5_a_im,
        s5_log_dt=s5_log_dt, s5_b_re=s5_b_re, s5_b_im=s5_b_im, s5_c_re=s5_c_re, s5_c_im=s5_c_im,
        s5_d=s5_d, s5_w_glu=s5_w_glu, w_branch=w_branch, w_out=w_out, xa_w_q=xa_w_q, xa_w_k=xa_w_k,
        xa_w_v=xa_w_v, xa_q_norm=xa_q_norm, xa_k_norm=xa_k_norm, xa_w_o=xa_w_o,
        ffn_w_gate=ffn_w_gate, ffn_w_up=ffn_w_up, ffn_w_down=ffn_w_down)
    depth = w_in.shape[0]
    bp, tp, _ = x_prompt.shape
    bs, ts, _ = x_sample.shape
    n_pool = cache_sb_k.shape[1]
    cache_k = cache_sb_k.reshape(depth, n_pool, PAGE_SIZE, D_BR)
    cache_v = cache_sb_v.reshape(depth, n_pool, PAGE_SIZE, D_BR)
    past_len = page_table.shape[1] * PAGE_SIZE
    tt_s5_p = min(tp, 256)
    tt_s5_s = ts

    xp, xs = x_prompt, x_sample
    outs_p = [[] for _ in range(9)]
    outs_s = [[] for _ in range(7)]
    for l in range(depth):
        lw = _layer_weights(l, weights, tt_s5_p, tt_s5_s)
        mk_p, mv_p = memory_kv(mem_prompt.reshape(bp * N_MEM, D_MODEL), lw['norm_mem'], lw['xa_w_k'],
                               lw['xa_w_v'], lw['xa_k_norm'])
        mk_p = mk_p.reshape(bp, N_MEM, D_BR)
        mv_p = mv_p.reshape(bp, N_MEM, D_BR)

        def sb_prompt_fn(q, k, v, lw=lw):
            hm = lambda z: jnp.swapaxes(z.reshape(bp * tp, N_HEADS, HEAD_DIM), 0, 1)
            o = sb_prompt(hm(q), hm(k).astype(BF16), hm(v).astype(BF16), lw['sb_bias'])
            return jnp.swapaxes(o, 0, 1).reshape(bp, tp, D_BR)

        xp, st = _trunk_layer(
            xp, 0, mk_p, mv_p, sb_prompt_fn,
            jnp.zeros((bp, POOL_HDR, D_BR), F32), jnp.zeros((bp, 1, RWKV_COLS), F32),
            jnp.zeros((bp, D_BR, HEAD_DIM), F32), jnp.zeros((bp, 1, S5_N), F32),
            jnp.zeros((bp, 1, S5_N), F32), lw, lw['s5_prompt'])
        for lst, val in zip(outs_p, (st[0], st[1], mk_p, mv_p) + st[2:]):
            lst.append(val)

        bias_rows = jnp.repeat(lw['sb_bias'], ts).reshape(N_HEADS * ts, 1)

        def sb_sample_fn(q, k, v, l=l, bias_rows=bias_rows):
            pad = ((0, 0), (0, PAGE_SIZE - ts), (0, 0))
            return sb_paged(q, bias_rows, jnp.pad(k, pad), jnp.pad(v, pad), cache_k, cache_v,
                            page_table, l)

        pool_buf = jnp.pad(state_pool[l], ((0, 0), (1, 0), (0, 0)))
        xs, st = _trunk_layer(
            xs, past_len, cache_mem_k[l].reshape(bs, N_MEM, D_BR),
            cache_mem_v[l].reshape(bs, N_MEM, D_BR), sb_sample_fn,
            pool_buf, state_rwkv_shift[l].reshape(bs, 1, RWKV_COLS),
            state_rwkv_wkv[l].reshape(bs, D_BR, HEAD_DIM),
            state_s5_re[l].reshape(bs, 1, S5_N), state_s5_im[l].reshape(bs, 1, S5_N),
            lw, lw['s5_sample'])
        for lst, val in zip(outs_s, st):
            lst.append(val)

    def stack(lst, shape):
        return jnp.stack(lst, 0).reshape((depth,) + shape)

    kv_p = (bp, tp, N_HEADS, HEAD_DIM)
    kv_s = (bs, ts, N_HEADS, HEAD_DIM)
    mem_shape = (bp, N_MEM, N_HEADS, HEAD_DIM)
    return (xp, xs,
            stack(outs_p[0], kv_p), stack(outs_p[1], kv_p),
            stack(outs_p[2], mem_shape), stack(outs_p[3], mem_shape),
            stack(outs_p[4], (bp, POOL_BUF, D_BR)), stack(outs_p[5], (bp, RWKV_COLS)),
            stack(outs_p[6], (bp, N_HEADS, HEAD_DIM, HEAD_DIM)),
            stack(outs_p[7], (bp, S5_GROUPS, S5_STATE)), stack(outs_p[8], (bp, S5_GROUPS, S5_STATE)),
            stack(outs_s[0], kv_s), stack(outs_s[1], kv_s),
            stack(outs_s[2], (bs, POOL_BUF, D_BR)), stack(outs_s[3], (bs, RWKV_COLS)),
            stack(outs_s[4], (bs, N_HEADS, HEAD_DIM, HEAD_DIM)),
            stack(outs_s[5], (bs, S5_GROUPS, S5_STATE)), stack(outs_s[6], (bs, S5_GROUPS, S5_STATE)))
```

```python
import functools
import math

import jax
import jax.numpy as jnp
from jax import lax
from jax.experimental import pallas as pl
from jax.experimental.pallas import tpu as pltpu

F32 = jnp.float32
BF16 = jnp.bfloat16
HIGHEST = lax.Precision.HIGHEST

D_MODEL = 1024
D_BR = 256
HEAD_DIM = 64
N_HEADS = 4
POOL_WINDOWS = (2, 4, 8, 16)
POOL_BUF = 15
POOL_HDR = 16
RWKV_COLS = 1024
RWKV_LN_EPS = 64e-5
S5_GROUPS = 16
S5_STATE = 64
S5_N = S5_GROUPS * S5_STATE
RWKV_CHUNK = 64
PAGE_SIZE = 128
N_MEM = 256
D_FF = 2816
RMS_EPS = 1e-6
OFF_RWKV = 256
OFF_SB = 1280
OFF_S5 = 2048
OFF_GATE = 2304

VMEM_LIMIT = 56 * 1024 * 1024


def _params(sem):
    return pltpu.CompilerParams(dimension_semantics=sem, vmem_limit_bytes=VMEM_LIMIT)


def _dot(a, b):
    return jnp.dot(a.astype(BF16), b.astype(BF16), preferred_element_type=F32)


def _dot_hi(a, b):
    return jnp.dot(a, b, precision=HIGHEST, preferred_element_type=F32)


def _dot_nt(a, b, precision=None):
    return lax.dot_general(a, b, (((1,), (1,)), ((), ())), precision=precision,
                           preferred_element_type=F32)


def _dot_tn(a, b, precision=None):
    return lax.dot_general(a, b, (((0,), (0,)), ((), ())), precision=precision,
                           preferred_element_type=F32)


def _rms(x, g):
    ms = jnp.mean(x * x, axis=-1, keepdims=True)
    return x * lax.rsqrt(ms + RMS_EPS) * g


def _head_rms(x, gain_row):
    parts = []
    for h in range(N_HEADS):
        xs = x[:, h * HEAD_DIM:(h + 1) * HEAD_DIM]
        ms = jnp.mean(xs * xs, axis=-1, keepdims=True)
        parts.append(xs * lax.rsqrt(ms + RMS_EPS))
    return jnp.concatenate(parts, axis=-1) * gain_row


def _sigmoid(x):
    return 1.0 / (1.0 + jnp.exp(-x))


def _log_sigmoid(z):
    return jnp.minimum(z, 0.0) - jnp.log1p(jnp.exp(-jnp.abs(z)))


def _const_spec(shape):
    nd = len(shape)
    return pl.BlockSpec(shape, lambda *_: (0,) * nd)


def _in_proj_kernel(x_ref, g_ref, wp_ref, wr_ref, wsb_ref, ws5_ref, wg_ref, qg_ref, kg_ref,
                    pool_ref, rwkv_ref, q_ref, k_ref, v_ref, s5_ref, gate_ref, *head_major_refs):
    h = _rms(x_ref[...], g_ref[...]).astype(BF16)
    pool_ref[...] = jnp.dot(h, wp_ref[...], preferred_element_type=F32)
    rwkv_ref[...] = jnp.dot(h, wr_ref[...], preferred_element_type=F32)
    sb = jnp.dot(h, wsb_ref[...], preferred_element_type=F32)
    q = _head_rms(sb[:, :D_BR], qg_ref[...])
    k = _head_rms(sb[:, D_BR:2 * D_BR], kg_ref[...])
    v = sb[:, 2 * D_BR:]
    q_ref[...] = q
    k_ref[...] = k
    v_ref[...] = v
    s5_ref[...] = jnp.dot(h, ws5_ref[...], preferred_element_type=F32)
    gate_ref[...] = jnp.dot(h, wg_ref[...], preferred_element_type=F32)
    if head_major_refs:
        qh_ref, kh_ref, vh_ref = head_major_refs
        for hh in range(N_HEADS):
            cols = slice(hh * HEAD_DIM, (hh + 1) * HEAD_DIM)
            qh_ref[hh] = (q[:, cols] * HEAD_DIM ** -0.5).astype(BF16)
            kh_ref[hh] = k[:, cols].astype(BF16)
            vh_ref[hh] = v[:, cols].astype(BF16)


def in_proj(x, g, wp, wr, wsb, ws5, wg, qg, kg, head_major):
    m = x.shape[0]
    tm = min(m, 256)
    row = lambda n: pl.BlockSpec((tm, n), lambda i: (i, 0))
    widths = (D_BR, RWKV_COLS, D_BR, D_BR, D_BR, D_BR, 4 * D_MODEL)
    out_specs = [row(n) for n in widths]
    out_shape = [jax.ShapeDtypeStruct((m, n), F32) for n in widths]
    if head_major:
        out_specs += [pl.BlockSpec((N_HEADS, tm, HEAD_DIM), lambda i: (0, i, 0))] * 3
        out_shape += [jax.ShapeDtypeStruct((N_HEADS, m, HEAD_DIM), BF16)] * 3
    return pl.pallas_call(
        _in_proj_kernel,
        grid=(m // tm,),
        in_specs=[row(D_MODEL), _const_spec(g.shape), _const_spec(wp.shape), _const_spec(wr.shape),
                  _const_spec(wsb.shape), _const_spec(ws5.shape), _const_spec(wg.shape),
                  _const_spec(qg.shape), _const_spec(kg.shape)],
        out_specs=out_specs,
        out_shape=out_shape,
        compiler_params=_params(("arbitrary",)),
        name="in_proj",
    )(x, g, wp, wr, wsb, ws5, wg, qg, kg)


def _pool_kernel(u_ref, buf_ref, w_ref, scale_ref, o_ref, new_ref, ext_ref, *, tt, pos0):
    t = pl.program_id(1)

    @pl.when(t == 0)
    def _():
        ext_ref[0:POOL_HDR, :] = buf_ref[0]

    @pl.when(t > 0)
    def _():
        ext_ref[0:POOL_HDR, :] = ext_ref[tt:tt + POOL_HDR, :]

    u = u_ref[0]
    ext_ref[POOL_HDR:POOL_HDR + tt, :] = u
    lane = lax.broadcasted_iota(jnp.int32, (tt, D_BR), 1)
    pos = pos0 + t * tt + lax.broadcasted_iota(jnp.int32, (tt, D_BR), 0)
    s = u
    win_sum = jnp.zeros((tt, D_BR), F32)
    count = jnp.zeros((tt, D_BR), F32)
    for i in range(1, POOL_WINDOWS[-1]):
        s = s + ext_ref[POOL_HDR - i:POOL_HDR - i + tt, :]
        w = i + 1
        if w in POOL_WINDOWS:
            gi = POOL_WINDOWS.index(w)
            sel = (lane >= gi * HEAD_DIM) & (lane < (gi + 1) * HEAD_DIM)
            win_sum = jnp.where(sel, s, win_sum)
            count = jnp.where(sel, jnp.minimum(pos + 1, w).astype(F32), count)
    pooled = win_sum / count - u
    o_ref[0] = _dot_hi(pooled, w_ref[...]) * scale_ref[...]
    new_ref[0] = ext_ref[tt + 1:tt + POOL_HDR, :]


def pool_mixer(u, buf, w_bd, scale, pos0):
    b, t, _ = u.shape
    tt = min(t, 512)
    kern = functools.partial(_pool_kernel, tt=tt, pos0=pos0)
    return pl.pallas_call(
        kern,
        grid=(b, t // tt),
        in_specs=[pl.BlockSpec((1, tt, D_BR), lambda i, j: (i, j, 0)),
                  pl.BlockSpec((1, POOL_HDR, D_BR), lambda i, j: (i, 0, 0)),
                  _const_spec(w_bd.shape), _const_spec(scale.shape)],
        out_specs=[pl.BlockSpec((1, tt, D_BR), lambda i, j: (i, j, 0)),
                   pl.BlockSpec((1, POOL_BUF, D_BR), lambda i, j: (i, 0, 0))],
        out_shape=[jax.ShapeDtypeStruct((b, t, D_BR), F32),
                   jax.ShapeDtypeStruct((b, POOL_BUF, D_BR), F32)],
        scratch_shapes=[pltpu.VMEM((POOL_HDR + tt, D_BR), F32)],
        compiler_params=_params(("arbitrary", "arbitrary")),
        name="pool_mixer",
    )(u, buf, w_bd, scale)


def _rwkv_pre_kernel(p_ref, shift_ref, mu_ref, w0_ref, wup_ref, a0_ref, aup_ref, gup_ref,
                     kk_ref, ka_ref,
                     r_out, k_out, v_out, ld_out, kk_out, b_out, g_out, li_out, last_ref, *, tt, c):
    t = pl.program_id(1)

    @pl.when(t == 0)
    def _():
        last_ref[...] = shift_ref[0]

    p = p_ref[0]
    rowi = lax.broadcasted_iota(jnp.int32, (tt, RWKV_COLS), 0)
    prev = jnp.where(rowi == 0, last_ref[...], pltpu.roll(p, 1, 0))
    last_ref[...] = p[tt - 1:tt, :]
    pm = p + (prev - p) * mu_ref[...]
    r = pm[:, 0:D_BR]
    k = pm[:, D_BR:2 * D_BR]
    v = pm[:, 2 * D_BR:3 * D_BR]
    w_lo = pm[:, 768:832]
    a_lo = pm[:, 832:896]
    g_lo = pm[:, 896:1024]
    wz = w0_ref[...] + _dot(jnp.tanh(w_lo), wup_ref[...])
    w_log = _log_sigmoid(wz) - 0.5
    a = _sigmoid(a0_ref[...] + _dot(a_lo, aup_ref[...]))
    g = _dot(_sigmoid(g_lo), gup_ref[...])
    kk = k * kk_ref[...]
    parts = []
    for h in range(N_HEADS):
        ks = kk[:, h * HEAD_DIM:(h + 1) * HEAD_DIM]
        ss = jnp.sum(ks * ks, axis=-1, keepdims=True)
        parts.append(ks * lax.rsqrt(jnp.maximum(ss, 1e-24)))
    kk = jnp.concatenate(parts, axis=-1)
    ld = -jnp.exp(w_log)
    rowc = lax.broadcasted_iota(jnp.int32, (tt, D_BR), 0) & (c - 1)
    l_inc = ld
    off = 1
    while off < c:
        l_inc = l_inc + jnp.where(rowc >= off, pltpu.roll(l_inc, off, 0), 0.0)
        off *= 2
    r_out[0] = r
    k_out[0] = k * (1.0 + (a - 1.0) * ka_ref[...])
    v_out[0] = v
    ld_out[0] = ld
    kk_out[0] = kk
    b_out[0] = kk * a
    g_out[0] = g
    li_out[0] = l_inc


def rwkv_pre(p, shift, mu, w0, wup, a0, aup, gup, k_k, k_a):
    b, t, _ = p.shape
    tt = min(t, 512)
    kern = functools.partial(_rwkv_pre_kernel, tt=tt, c=min(t, RWKV_CHUNK))
    blk = pl.BlockSpec((1, tt, D_BR), lambda i, j: (i, j, 0))
    consts = (mu, w0, wup, a0, aup, gup, k_k, k_a)
    return pl.pallas_call(
        kern,
        grid=(b, t // tt),
        in_specs=[pl.BlockSpec((1, tt, RWKV_COLS), lambda i, j: (i, j, 0)),
                  pl.BlockSpec((1, 1, RWKV_COLS), lambda i, j: (i, 0, 0))]
                 + [_const_spec(c.shape) for c in consts],
        out_specs=[blk] * 8,
        out_shape=[jax.ShapeDtypeStruct((b, t, D_BR), F32)] * 8,
        scratch_shapes=[pltpu.VMEM((1, RWKV_COLS), F32)],
        compiler_params=_params(("arbitrary", "arbitrary")),
        name="rwkv_pre",
    )(p, shift, *consts)


_BMM_DIMS = {
    'nn': (((2,), (1,)), ((0,), (0,))),
    'nt': (((2,), (2,)), ((0,), (0,))),
}


def _split2(x):
    hi = x.astype(BF16)
    return hi, (x - hi.astype(F32)).astype(BF16)


def _bmm3(a, b, dims):
    dn = _BMM_DIMS[dims]
    ah, al = _split2(a)
    bh, bl = _split2(b)
    mm = lambda x, y: lax.dot_general(x, y, dn, preferred_element_type=F32)
    return mm(ah, bh) + mm(ah, bl) + mm(al, bh)


def _rwkv_chunk_kernel(r_ref, k_ref, v_ref, ld_ref, kk_ref, b_ref, g_ref, li_ref, s0_ref,
                       lnw_ref, lnb_ref, rk_ref, o_ref, sout_ref, s_ref, *, tt, c):
    t = pl.program_id(1)

    @pl.when(t == 0)
    def _():
        s_ref[...] = s0_ref[0]

    nch = tt // c
    nb = nch * N_HEADS

    def heads(x):
        return jnp.stack([x[ch * c:(ch + 1) * c, h * HEAD_DIM:(h + 1) * HEAD_DIM]
                          for ch in range(nch) for h in range(N_HEADS)])

    def head_rows(ref):
        return jnp.stack([ref[:, h * HEAD_DIM:(h + 1) * HEAD_DIM] for h in range(N_HEADS)])

    ri = lax.broadcasted_iota(jnp.int32, (nb, c, c), 1)
    ci = lax.broadcasted_iota(jnp.int32, (nb, c, c), 2)
    incl = (ri >= ci).astype(F32)
    strict = (ri > ci).astype(F32)
    eye = (ri == ci).astype(F32)
    n_sq = int(round(math.log2(c))) - 1

    r_all = r_ref[0]
    k_all = k_ref[0]
    v_all = v_ref[0]
    b_all = b_ref[0]
    l_inc = li_ref[0]
    l_tot = jnp.concatenate(
        [jnp.broadcast_to(l_inc[(ch + 1) * c - 1:(ch + 1) * c, :], (c, D_BR)) for ch in range(nch)], axis=0)
    e_neg = jnp.exp(-l_inc)
    e_end = jnp.exp(l_tot - l_inc)
    a_t = heads(-kk_ref[0] * jnp.exp(l_inc - ld_ref[0]))
    r_t = heads(r_all * jnp.exp(l_inc))
    b_t = heads(b_all * e_neg)
    k_t = heads(k_all * e_neg)
    b_e = heads(b_all * e_end)
    k_e = heads(k_all * e_end)
    d_tot = heads(jnp.exp(l_tot))[:, 0:1, :]
    r = heads(r_all)
    k = heads(k_all)
    v = heads(v_all)

    gram = _bmm3(jnp.concatenate([a_t, r_t], axis=1), jnp.concatenate([b_t, k_t], axis=1), 'nt')
    m_ab = gram[:, :c, :c] * strict
    m_ak = gram[:, :c, c:] * strict
    p_rb = gram[:, c:, :c] * incl
    p_rk = gram[:, c:, c:] * incl
    pw = m_ab
    tinv = eye + m_ab
    for _ in range(n_sq):
        pw = _bmm3(pw, pw, 'nn')
        tinv = tinv + _bmm3(tinv, pw, 'nn')
    wu = _bmm3(tinv, jnp.concatenate([a_t, _bmm3(m_ak, v, 'nn')], axis=2), 'nn')
    o0 = _bmm3(p_rk, v, 'nn')
    uv_rhs = jnp.concatenate([b_e, k_e], axis=1)

    lnw = head_rows(lnw_ref)
    lnb = head_rows(lnb_ref)
    rk = head_rows(rk_ref)
    s = s_ref[...]
    for ch in range(nch):
        sl = slice(ch * N_HEADS, (ch + 1) * N_HEADS)
        y = _bmm3(jnp.concatenate([wu[sl, :, :HEAD_DIM], r_t[sl]], axis=1), s, 'nt')
        u = y[:, :c] + wu[sl, :, HEAD_DIM:]
        o = y[:, c:] + _bmm3(p_rb[sl], u, 'nn') + o0[sl]
        uv_t = jnp.swapaxes(jnp.concatenate([u, v[sl]], axis=1), 1, 2)
        s = s * d_tot[sl] + _bmm3(uv_t, uv_rhs[sl], 'nn')
        mean = jnp.mean(o, axis=-1, keepdims=True)
        var = jnp.mean(jnp.square(o - mean), axis=-1, keepdims=True)
        o = (o - mean) * lax.rsqrt(var + RWKV_LN_EPS) * lnw + lnb
        o = o + jnp.sum(r[sl] * k[sl] * rk, axis=-1, keepdims=True) * v[sl]
        rows = slice(ch * c, (ch + 1) * c)
        o_ref[0, rows, :] = (jnp.concatenate([o[h] for h in range(N_HEADS)], axis=-1)
                             * g_ref[0, rows, :])
    s_ref[...] = s
    sout_ref[0] = s


def rwkv_chunk(r, k, v, ld, kk, bv, g, l_inc, s0, ln_w, ln_b, r_k):
    b, t, _ = r.shape
    c = min(t, RWKV_CHUNK)
    tt = min(t, 256)
    kern = functools.partial(_rwkv_chunk_kernel, tt=tt, c=c)
    blk = pl.BlockSpec((1, tt, D_BR), lambda i, j: (i, j, 0))
    sblk = pl.BlockSpec((1, N_HEADS, HEAD_DIM, HEAD_DIM), lambda i, j: (i, 0, 0, 0))
    return pl.pallas_call(
        kern,
        grid=(b, t // tt),
        in_specs=[blk] * 8 + [sblk, _const_spec(ln_w.shape), _const_spec(ln_b.shape),
                              _const_spec(r_k.shape)],
        out_specs=[blk, sblk],
        out_shape=[jax.ShapeDtypeStruct((b, t, D_BR), F32),
                   jax.ShapeDtypeStruct((b, N_HEADS, HEAD_DIM, HEAD_DIM), F32)],
        scratch_shapes=[pltpu.VMEM((N_HEADS, HEAD_DIM, HEAD_DIM), F32)],
        compiler_params=_params(("arbitrary", "arbitrary")),
        name="rwkv_chunk",
    )(r, k, v, ld, kk, bv, g, l_inc, s0, ln_w, ln_b, r_k)


SB_GROUP = 4


def _softplus(z):
    return jnp.maximum(z, 0.0) + jnp.log(1.0 + jnp.exp(-jnp.abs(z)))


def _sb_prompt_kernel(bias_ref, q_ref, k_ref, v_ref, o_ref, *, tq):
    h = pl.program_id(0)
    i = pl.program_id(1)
    q = q_ref[0]
    bias = bias_ref[h]
    ri = lax.broadcasted_iota(jnp.int32, (tq, tq), 0)
    ci = lax.broadcasted_iota(jnp.int32, (tq, tq), 1)
    later = (ri > ci).astype(BF16)

    def scores(j):
        start = pl.multiple_of(j * tq, tq)
        z = _dot_nt(q, k_ref[0, pl.ds(start, tq), :]) + bias
        return z, v_ref[0, pl.ds(start, tq), :]

    def full_blocks(j, n, car, acc):
        zv = [scores(j - b) for b in range(n)]
        sps = [_softplus(z) for z, _ in zv]
        afts = [jnp.dot(sp.astype(BF16), later, preferred_element_type=F32) for sp in sps]
        atts = []
        for b in range(n):
            atts.append(jnp.exp(zv[b][0] - sps[b] - (afts[b] + car)).astype(BF16))
            car = car + jnp.sum(sps[b], axis=-1, keepdims=True)
        for b in range(n):
            acc = acc + jnp.dot(atts[b], zv[b][1], preferred_element_type=F32)
        return car, acc

    z, vb = scores(i)
    mask = ci < ri
    sp = jnp.where(mask, _softplus(z), 0.0)
    aft = jnp.dot(sp.astype(BF16), later, preferred_element_type=F32)
    att = jnp.where(mask, jnp.exp(z - sp - aft), 0.0)
    acc = jnp.dot(att.astype(BF16), vb, preferred_element_type=F32)
    car = jnp.sum(sp, axis=-1, keepdims=True)

    rem = i & (SB_GROUP - 1)
    car, acc = lax.fori_loop(
        0, rem, lambda p, ca: full_blocks(i - 1 - p, 1, *ca), (car, acc))
    j0 = i - 1 - rem
    _, acc = lax.fori_loop(
        0, lax.shift_right_logical(i, SB_GROUP.bit_length() - 1),
        lambda p, ca: full_blocks(j0 - SB_GROUP * p, SB_GROUP, *ca), (car, acc))
    o_ref[0] = acc


def sb_prompt(q, k, v, bias):
    h, t, _ = q.shape
    tq = min(t, 256)
    kern = functools.partial(_sb_prompt_kernel, tq=tq)
    return pl.pallas_call(
        kern,
        grid=(h, t // tq),
        in_specs=[pl.BlockSpec(memory_space=pltpu.SMEM),
                  pl.BlockSpec((1, tq, HEAD_DIM), lambda a, i: (a, i, 0)),
                  pl.BlockSpec((1, t, HEAD_DIM), lambda a, i: (a, 0, 0)),
                  pl.BlockSpec((1, t, HEAD_DIM), lambda a, i: (a, 0, 0))],
        out_specs=pl.BlockSpec((1, tq, HEAD_DIM), lambda a, i: (a, i, 0)),
        out_shape=jax.ShapeDtypeStruct((h, t, HEAD_DIM), F32),
        compiler_params=_params(("arbitrary", "arbitrary")),
        name="sb_prompt",
    )(bias, q, k, v)


PAGES_PER_STEP = 8


def _sb_paged_kernel(pt_ref, q_ref, bias_ref, kn_ref, vn_ref, *rest, t_new, past_len):
    nk = PAGES_PER_STEP
    k_pages = rest[:nk]
    v_pages = rest[nk:2 * nk]
    o_ref = rest[2 * nk]
    acc_ref, car_ref = rest[2 * nk + 1:]
    s = pl.program_id(1)
    q = q_ref[0] * HEAD_DIM ** -0.5
    qh = [q[:, h * HEAD_DIM:(h + 1) * HEAD_DIM].astype(BF16) for h in range(N_HEADS)]
    bias = bias_ref[...]
    ri = lax.broadcasted_iota(jnp.int32, (PAGE_SIZE, PAGE_SIZE), 0)
    ci = lax.broadcasted_iota(jnp.int32, (PAGE_SIZE, PAGE_SIZE), 1)
    later = (ri > ci).astype(BF16)

    def head_rows(x):
        return [x[h * t_new:(h + 1) * t_new] for h in range(N_HEADS)]

    @pl.when(s == 0)
    def _():
        kn = kn_ref[0].astype(BF16)
        vn = vn_ref[0].astype(BF16)
        cols = [slice(h * HEAD_DIM, (h + 1) * HEAD_DIM) for h in range(N_HEADS)]
        z = jnp.concatenate([_dot_nt(qh[h], kn[:, cols[h]]) for h in range(N_HEADS)], axis=0) + bias
        qi = jnp.concatenate([lax.broadcasted_iota(jnp.int32, (t_new, PAGE_SIZE), 0)] * N_HEADS, axis=0)
        mask = lax.broadcasted_iota(jnp.int32, (N_HEADS * t_new, PAGE_SIZE), 1) < qi
        sp = jnp.where(mask, _softplus(z), 0.0)
        aft = jnp.dot(sp.astype(BF16), later, preferred_element_type=F32)
        att = head_rows(jnp.where(mask, jnp.exp(z - sp - aft), 0.0).astype(BF16))
        acc_ref[...] = jnp.concatenate(
            [jnp.dot(att[h], vn[:, cols[h]], preferred_element_type=F32) for h in range(N_HEADS)], axis=0)
        car_ref[...] = jnp.sum(sp, axis=-1, keepdims=True)

    def gather_head(pages, h):
        return jnp.concatenate([p[0, 0, h] for p in pages], axis=1).astype(BF16)

    z = jnp.concatenate(
        [jnp.dot(qh[h], gather_head(k_pages, h), preferred_element_type=F32) for h in range(N_HEADS)],
        axis=0) + bias
    sp = _softplus(z)
    car = car_ref[...]
    afts = []
    for rr in range(nk):
        sp_rr = sp[:, rr * PAGE_SIZE:(rr + 1) * PAGE_SIZE]
        afts.append(jnp.dot(sp_rr.astype(BF16), later, preferred_element_type=F32) + car)
        car = car + jnp.sum(sp_rr, axis=-1, keepdims=True)
    att = head_rows(jnp.exp(z - sp - jnp.concatenate(afts, axis=1)).astype(BF16))
    acc_ref[...] += jnp.concatenate(
        [_dot_nt(att[h], gather_head(v_pages, h)) for h in range(N_HEADS)], axis=0)
    car_ref[...] = car

    @pl.when(s == pl.num_programs(1) - 1)
    def _():
        o_ref[0] = jnp.concatenate(head_rows(acc_ref[...]), axis=-1)


def sb_paged(q, bias_rows, k_new, v_new, cache_k, cache_v, page_table, layer):
    b, t_new, _ = q.shape
    n_pages = page_table.shape[1]
    past_len = n_pages * PAGE_SIZE
    nk = PAGES_PER_STEP
    n_steps = n_pages // nk
    kern = functools.partial(_sb_paged_kernel, t_new=t_new, past_len=past_len)

    def page_spec(rr):
        return pl.BlockSpec(
            (1, 1, N_HEADS, HEAD_DIM, PAGE_SIZE),
            lambda i, s, pt: (layer, pt[i, n_pages - 1 - (s * nk + rr)], 0, 0, 0))

    n_rows = N_HEADS * t_new
    grid_spec = pltpu.PrefetchScalarGridSpec(
        num_scalar_prefetch=1,
        grid=(b, n_steps),
        in_specs=[pl.BlockSpec((1, t_new, D_BR), lambda i, s, pt: (i, 0, 0)),
                  pl.BlockSpec((n_rows, 1), lambda i, s, pt: (0, 0)),
                  pl.BlockSpec((1, PAGE_SIZE, D_BR), lambda i, s, pt: (i, 0, 0)),
                  pl.BlockSpec((1, PAGE_SIZE, D_BR), lambda i, s, pt: (i, 0, 0))]
                 + [page_spec(rr) for rr in range(nk)] * 2,
        out_specs=pl.BlockSpec((1, t_new, D_BR), lambda i, s, pt: (i, 0, 0)),
        scratch_shapes=[pltpu.VMEM((n_rows, HEAD_DIM), F32), pltpu.VMEM((n_rows, 1), F32)],
    )
    return pl.pallas_call(
        kern,
        grid_spec=grid_spec,
        out_shape=jax.ShapeDtypeStruct((b, t_new, D_BR), F32),
        compiler_params=_params(("arbitrary", "arbitrary")),
        name="sb_paged",
    )(page_table, q, bias_rows, k_new, v_new, *([cache_k] * nk), *([cache_v] * nk))


def _s5_kernel(u_ref, hre_ref, him_ref, bre_ref, bim_ref, cre_ref, cim_ref, d_ref, wglu_ref,
               powr_ref, powi_ref, lvr_ref, lvi_ref,
               o_ref, sre_ref, sim_ref, cr_ref, cim_s_ref, *, tt):
    t = pl.program_id(1)

    @pl.when(t == 0)
    def _():
        cr_ref[...] = hre_ref[0]
        cim_s_ref[...] = him_ref[0]

    u = u_ref[0]
    xr = _dot(u, bre_ref[...])
    xi = _dot(u, bim_ref[...])
    rowi = lax.broadcasted_iota(jnp.int32, (tt, S5_N), 0)
    off = 1
    lvl = 0
    while off < tt:
        lr = lvr_ref[lvl:lvl + 1, :]
        li = lvi_ref[lvl:lvl + 1, :]
        m = rowi >= off
        sr = jnp.where(m, pltpu.roll(xr, off, 0), 0.0)
        si = jnp.where(m, pltpu.roll(xi, off, 0), 0.0)
        xr, xi = xr + lr * sr - li * si, xi + lr * si + li * sr
        off *= 2
        lvl += 1
    h0r = cr_ref[...]
    h0i = cim_s_ref[...]
    pr = powr_ref[...]
    pi = powi_ref[...]
    xr, xi = xr + pr * h0r - pi * h0i, xi + pr * h0i + pi * h0r
    cr_ref[...] = xr[tt - 1:tt, :]
    cim_s_ref[...] = xi[tt - 1:tt, :]
    sre_ref[0] = xr[tt - 1:tt, :]
    sim_ref[0] = xi[tt - 1:tt, :]
    y = _dot(xr, cre_ref[...]) - _dot(xi, cim_ref[...]) + d_ref[...] * u
    z = _dot(jax.nn.gelu(y), wglu_ref[...])
    o_ref[0] = z[:, :D_BR] * _sigmoid(z[:, D_BR:])


def s5_mixer(u, h_re, h_im, b_re, b_im, c_re, c_im, d, w_glu, pow_re, pow_im, lv_re, lv_im):
    b, t, _ = u.shape
    tt = pow_re.shape[0]
    kern = functools.partial(_s5_kernel, tt=tt)
    consts = (b_re, b_im, c_re, c_im, d, w_glu, pow_re, pow_im, lv_re, lv_im)
    sblk = pl.BlockSpec((1, 1, S5_N), lambda i, j: (i, 0, 0))
    return pl.pallas_call(
        kern,
        grid=(b, t // tt),
        in_specs=[pl.BlockSpec((1, tt, D_BR), lambda i, j: (i, j, 0)), sblk, sblk]
                 + [_const_spec(c.shape) for c in consts],
        out_specs=[pl.BlockSpec((1, tt, D_BR), lambda i, j: (i, j, 0)), sblk, sblk],
        out_shape=[jax.ShapeDtypeStruct((b, t, D_BR), F32),
                   jax.ShapeDtypeStruct((b, 1, S5_N), F32),
                   jax.ShapeDtypeStruct((b, 1, S5_N), F32)],
        scratch_shapes=[pltpu.VMEM((1, S5_N), F32), pltpu.VMEM((1, S5_N), F32)],
        compiler_params=_params(("arbitrary", "arbitrary")),
        name="s5_mixer",
    )(u, h_re, h_im, *consts)


def _merge_kernel(x_ref, bp_ref, br_ref, bs_ref, b5_ref, gate_ref, wb_ref, wo_ref, o_ref):
    if len(bs_ref.shape) == 3:
        o_sb = jnp.concatenate([bs_ref[h] for h in range(N_HEADS)], axis=-1)
    else:
        o_sb = bs_ref[...]
    merged = None
    for n, br in enumerate((bp_ref[...], br_ref[...], o_sb, b5_ref[...])):
        lifted = _dot(br, wb_ref[n])
        term = _sigmoid(gate_ref[:, n * D_MODEL:(n + 1) * D_MODEL]) * lifted
        merged = term if merged is None else merged + term
    o_ref[...] = x_ref[...] + _dot(merged, wo_ref[...])


def merge(x, o_pool, o_rwkv, o_sb, o_s5, gates, w_branch, w_out):
    m = x.shape[0]
    tm = min(m, 256)
    row = lambda n: pl.BlockSpec((tm, n), lambda i: (i, 0))
    sb_spec = row(D_BR) if o_sb.ndim == 2 else pl.BlockSpec((N_HEADS, tm, HEAD_DIM), lambda i: (0, i, 0))
    return pl.pallas_call(
        _merge_kernel,
        grid=(m // tm,),
        in_specs=[row(D_MODEL), row(D_BR), row(D_BR), sb_spec, row(D_BR), row(4 * D_MODEL),
                  _const_spec(w_branch.shape), _const_spec(w_out.shape)],
        out_specs=row(D_MODEL),
        out_shape=jax.ShapeDtypeStruct((m, D_MODEL), F32),
        compiler_params=_params(("arbitrary",)),
        name="merge",
    )(x, o_pool, o_rwkv, o_sb, o_s5, gates, w_branch, w_out)


def _xattn_kernel(x_ref, mk_ref, mv_ref, g_ref, wq_ref, qg_ref, wo_ref, o_ref):
    x = x_ref[0]
    hn = _rms(x, g_ref[...])
    q = _head_rms(_dot(hn, wq_ref[...]), qg_ref[...]) * HEAD_DIM ** -0.5
    outs = []
    for h in range(N_HEADS):
        cols = slice(h * HEAD_DIM, (h + 1) * HEAD_DIM)
        s = _dot(q[:, cols], mk_ref[0, 0, h])
        s = s - jnp.max(s, axis=-1, keepdims=True)
        e = jnp.exp(s)
        pr = e / jnp.sum(e, axis=-1, keepdims=True)
        outs.append(_dot_nt(pr.astype(BF16), mv_ref[0, 0, h].astype(BF16)))
    o = jnp.concatenate(outs, axis=-1)
    o_ref[0] = x + _dot(o, wo_ref[...])


def cross_attention(x, mem_k, mem_v, layer, g, wq, qg, wo):
    b, t, _ = x.shape
    tt = min(t, 512)
    mem_spec = pl.BlockSpec((1, 1, N_HEADS, HEAD_DIM, N_MEM), lambda i, j: (layer, i, 0, 0, 0))
    return pl.pallas_call(
        _xattn_kernel,
        grid=(b, t // tt),
        in_specs=[pl.BlockSpec((1, tt, D_MODEL), lambda i, j: (i, j, 0)), mem_spec, mem_spec,
                  _const_spec(g.shape), _const_spec(wq.shape), _const_spec(qg.shape),
                  _const_spec(wo.shape)],
        out_specs=pl.BlockSpec((1, tt, D_MODEL), lambda i, j: (i, j, 0)),
        out_shape=jax.ShapeDtypeStruct((b, t, D_MODEL), F32),
        compiler_params=_params(("arbitrary", "arbitrary")),
        name="cross_attention",
    )(x, mem_k, mem_v, g, wq, qg, wo)


def _memkv_kernel(mem_ref, g_ref, wk_ref, wv_ref, kg_ref, k_ref, v_ref, kt_ref, vt_ref):
    mn = _rms(mem_ref[...], g_ref[...])
    k = _head_rms(_dot(mn, wk_ref[...]), kg_ref[...])
    v = _dot(mn, wv_ref[...])
    k_ref[...] = k
    v_ref[...] = v
    kt_ref[...] = k.T
    vt_ref[...] = v.T


def memory_kv(mem, g, wk, wv, kg):
    m = mem.shape[0]
    args = (mem, g, wk, wv, kg)
    return pl.pallas_call(
        _memkv_kernel,
        grid=(1,),
        in_specs=[_const_spec(a.shape) for a in args],
        out_specs=[_const_spec((m, D_BR))] * 2 + [_const_spec((D_BR, m))] * 2,
        out_shape=[jax.ShapeDtypeStruct((m, D_BR), F32)] * 2 + [jax.ShapeDtypeStruct((D_BR, m), F32)] * 2,
        compiler_params=_params(("arbitrary",)),
        name="memory_kv",
    )(*args)


def _ffn_kernel(x_ref, g_ref, wg_ref, wu_ref, wd_ref, o_ref):
    x = x_ref[...]
    hn = _rms(x, g_ref[...]).astype(BF16)
    a = jnp.dot(hn, wg_ref[...], preferred_element_type=F32)
    bq = jnp.dot(hn, wu_ref[...], preferred_element_type=F32)
    act = a * _sigmoid(a) * bq
    o_ref[...] = x + _dot(act, wd_ref[...])


def ffn(x, g, w_gate, w_up, w_down):
    m = x.shape[0]
    tm = min(m, 256)
    row = pl.BlockSpec((tm, D_MODEL), lambda i: (i, 0))
    return pl.pallas_call(
        _ffn_kernel,
        grid=(m // tm,),
        in_specs=[row, _const_spec(g.shape), _const_spec(w_gate.shape), _const_spec(w_up.shape),
                  _const_spec(w_down.shape)],
        out_specs=row,
        out_shape=jax.ShapeDtypeStruct((m, D_MODEL), F32),
        compiler_params=_params(("arbitrary",)),
        name="ffn",
    )(x, g, w_gate, w_up, w_down)


def _block_diag(blocks):
    g, m, n = blocks.shape
    eye = jnp.eye(g, dtype=blocks.dtype)
    return (eye[:, None, :, None] * blocks[:, :, None, :]).reshape(g * m, g * n)


def _tile_heads(v):
    return jnp.tile(v, N_HEADS).reshape(1, N_HEADS * v.shape[0])


def _cmul(ar, ai, br, bi):
    return ar * br - ai * bi, ar * bi + ai * br


def _s5_constants(a_re, a_im, log_dt, b_re, b_im, c_re, c_im, tt):
    dt_g = jnp.exp(log_dt)[:, None]
    mag = jnp.exp(dt_g * a_re)
    ab_re, ab_im = mag * jnp.cos(dt_g * a_im), mag * jnp.sin(dt_g * a_im)
    den = a_re * a_re + a_im * a_im
    n_re = ab_re - 1.0
    f_re = (n_re * a_re + ab_im * a_im) / den
    f_im = (ab_im * a_re - n_re * a_im) / den
    bb_re = f_re[..., None] * b_re - f_im[..., None] * b_im
    bb_im = f_re[..., None] * b_im + f_im[..., None] * b_re
    bmat_re = _block_diag(jnp.swapaxes(bb_re, 1, 2))
    bmat_im = _block_diag(jnp.swapaxes(bb_im, 1, 2))
    cmat_re = _block_diag(jnp.swapaxes(c_re, 1, 2))
    cmat_im = _block_diag(jnp.swapaxes(c_im, 1, 2))
    lam_re = ab_re.reshape(1, S5_N)
    lam_im = ab_im.reshape(1, S5_N)
    pow_re, pow_im = lam_re, lam_im
    lv_re, lv_im = [lam_re], [lam_im]
    cur_re, cur_im = lam_re, lam_im
    n = 1
    while n < tt:
        nr, ni = _cmul(pow_re, pow_im, cur_re, cur_im)
        pow_re = jnp.concatenate([pow_re, nr], axis=0)
        pow_im = jnp.concatenate([pow_im, ni], axis=0)
        cur_re, cur_im = _cmul(cur_re, cur_im, cur_re, cur_im)
        lv_re.append(cur_re)
        lv_im.append(cur_im)
        n *= 2
    n_lv = max(len(lv_re) - 1, 1)
    lv_re = jnp.concatenate(lv_re[:n_lv], axis=0)
    lv_im = jnp.concatenate(lv_im[:n_lv], axis=0)
    return (bmat_re.astype(BF16), bmat_im.astype(BF16), cmat_re.astype(BF16), cmat_im.astype(BF16),
            pow_re, pow_im, lv_re, lv_im)


def _layer_weights(l, w, tt_prompt, tt_sample):
    w_in = w['w_in'][l]
    lw = {
        'norm_mix': w['norm_mix'][l].reshape(1, D_MODEL),
        'norm_cross': w['norm_cross'][l].reshape(1, D_MODEL),
        'norm_mem': w['norm_mem'][l].reshape(1, D_MODEL),
        'norm_ffn': w['norm_ffn'][l].reshape(1, D_MODEL),
        'w_pool_in': w_in[:, :OFF_RWKV].astype(BF16),
        'w_rwkv_in': w_in[:, OFF_RWKV:OFF_SB].astype(BF16),
        'w_sb_in': w_in[:, OFF_SB:OFF_S5].astype(BF16),
        'w_s5_in': w_in[:, OFF_S5:OFF_GATE].astype(BF16),
        'w_gate_in': w_in[:, OFF_GATE:].astype(BF16),
        'sb_q_norm': _tile_heads(w['sb_q_norm'][l]),
        'sb_k_norm': _tile_heads(w['sb_k_norm'][l]),
        'sb_bias': w['sb_bias'][l],
        'pool_w': _block_diag(w['pool_w'][l]),
        'pool_scale': w['pool_scale'][l].reshape(1, D_BR),
        'rwkv_mu': w['rwkv_mu'][l].reshape(1, RWKV_COLS),
        'rwkv_w0': w['rwkv_w0'][l].reshape(1, D_BR),
        'rwkv_w_up': w['rwkv_w_up'][l].astype(BF16),
        'rwkv_a0': w['rwkv_a0'][l].reshape(1, D_BR),
        'rwkv_a_up': w['rwkv_a_up'][l].astype(BF16),
        'rwkv_g_up': w['rwkv_g_up'][l].astype(BF16),
        'rwkv_k_k': w['rwkv_k_k'][l].reshape(1, D_BR),
        'rwkv_k_a': w['rwkv_k_a'][l].reshape(1, D_BR),
        'rwkv_r_k': w['rwkv_r_k'][l].reshape(1, D_BR),
        'rwkv_ln_w': w['rwkv_ln_w'][l].reshape(1, D_BR),
        'rwkv_ln_b': w['rwkv_ln_b'][l].reshape(1, D_BR),
        's5_d': w['s5_d'][l].reshape(1, D_BR),
        's5_w_glu': w['s5_w_glu'][l].astype(BF16),
        'w_branch': w['w_branch'][l].astype(BF16),
        'w_out': w['w_out'][l].astype(BF16),
        'xa_w_q': w['xa_w_q'][l].astype(BF16),
        'xa_w_k': w['xa_w_k'][l].astype(BF16),
        'xa_w_v': w['xa_w_v'][l].astype(BF16),
        'xa_q_norm': _tile_heads(w['xa_q_norm'][l]),
        'xa_k_norm': _tile_heads(w['xa_k_norm'][l]),
        'xa_w_o': w['xa_w_o'][l].astype(BF16),
        'ffn_w_gate': w['ffn_w_gate'][l].astype(BF16),
        'ffn_w_up': w['ffn_w_up'][l].astype(BF16),
        'ffn_w_down': w['ffn_w_down'][l].astype(BF16),
    }
    s5_args = (w['s5_a_re'][l], w['s5_a_im'][l], w['s5_log_dt'][l], w['s5_b_re'][l], w['s5_b_im'][l],
               w['s5_c_re'][l], w['s5_c_im'][l])
    lw['s5_prompt'] = _s5_constants(*s5_args, tt_prompt)
    lw['s5_sample'] = _s5_constants(*s5_args, tt_sample)
    return lw


def _trunk_layer(x, pos0, mem_k, mem_v, mem_layer, sb_fn, pool_buf, shift, wkv, s5_re, s5_im, lw, s5c,
                 fresh):
    b, t, _ = x.shape
    m = b * t
    xf = x.reshape(m, D_MODEL)
    u_pool, p_rwkv, q, k, v, u_s5, gates, *head_major = in_proj(
        xf, lw['norm_mix'], lw['w_pool_in'], lw['w_rwkv_in'], lw['w_sb_in'], lw['w_s5_in'],
        lw['w_gate_in'], lw['sb_q_norm'], lw['sb_k_norm'], fresh)
    o_pool, new_pool = pool_mixer(u_pool.reshape(b, t, D_BR), pool_buf, lw['pool_w'],
                                  lw['pool_scale'], pos0)
    p3 = p_rwkv.reshape(b, t, RWKV_COLS)
    pre = rwkv_pre(p3, shift, lw['rwkv_mu'], lw['rwkv_w0'], lw['rwkv_w_up'], lw['rwkv_a0'],
                   lw['rwkv_a_up'], lw['rwkv_g_up'], lw['rwkv_k_k'], lw['rwkv_k_a'])
    o_rwkv, new_wkv = rwkv_chunk(*pre, wkv, lw['rwkv_ln_w'], lw['rwkv_ln_b'], lw['rwkv_r_k'])
    new_shift = p3[:, -1]
    q3, k3, v3 = (z.reshape(b, t, D_BR) for z in (q, k, v))
    o_sb = sb_fn(*head_major) if fresh else sb_fn(q3, k3, v3).reshape(m, D_BR)
    o_s5, new_re, new_im = s5_mixer(u_s5.reshape(b, t, D_BR), s5_re, s5_im, s5c[0], s5c[1], s5c[2],
                                    s5c[3], lw['s5_d'], lw['s5_w_glu'], *s5c[4:])
    x1 = merge(xf, o_pool.reshape(m, D_BR), o_rwkv.reshape(m, D_BR), o_sb,
               o_s5.reshape(m, D_BR), gates, lw['w_branch'], lw['w_out'])
    x2 = cross_attention(x1.reshape(b, t, D_MODEL), mem_k, mem_v, mem_layer, lw['norm_cross'],
                         lw['xa_w_q'], lw['xa_q_norm'], lw['xa_w_o'])
    x3 = ffn(x2.reshape(m, D_MODEL), lw['norm_ffn'], lw['ffn_w_gate'], lw['ffn_w_up'],
             lw['ffn_w_down'])
    return x3.reshape(b, t, D_MODEL), (k3, v3, new_pool, new_shift, new_wkv, new_re, new_im)


def kernel(x_prompt, x_sample, cache_sb_k, cache_sb_v, cache_mem_k, cache_mem_v, state_pool, state_rwkv_shift, state_rwkv_wkv, state_s5_re, state_s5_im, page_table, mem_prompt, norm_mix, norm_cross, norm_mem, norm_ffn, w_in, pool_w, pool_scale, rwkv_mu, rwkv_w0, rwkv_w_up, rwkv_a0, rwkv_a_up, rwkv_g_up, rwkv_k_k, rwkv_k_a, rwkv_r_k, rwkv_ln_w, rwkv_ln_b, sb_q_norm, sb_k_norm, sb_bias, s5_a_re, s5_a_im, s5_log_dt, s5_b_re, s5_b_im, s5_c_re, s5_c_im, s5_d, s5_w_glu, w_branch, w_out, xa_w_q, xa_w_k, xa_w_v, xa_q_norm, xa_k_norm, xa_w_o, ffn_w_gate, ffn_w_up, ffn_w_down):
    weights = dict(
        norm_mix=norm_mix, norm_cross=norm_cross, norm_mem=norm_mem, norm_ffn=norm_ffn, w_in=w_in,
        pool_w=pool_w, pool_scale=pool_scale, rwkv_mu=rwkv_mu, rwkv_w0=rwkv_w0, rwkv_w_up=rwkv_w_up,
        rwkv_a0=rwkv_a0, rwkv_a_up=rwkv_a_up, rwkv_g_up=rwkv_g_up, rwkv_k_k=rwkv_k_k,
        rwkv_k_a=rwkv_k_a, rwkv_r_k=rwkv_r_k, rwkv_ln_w=rwkv_ln_w, rwkv_ln_b=rwkv_ln_b,
        sb_q_norm=sb_q_norm, sb_k_norm=sb_k_norm, sb_bias=sb_bias, s5_a_re=s5_a_re, s5_a_im=s5_a_im,
        s5_log_dt=s5_log_dt, s5_b_re=s5_b_re, s5_b_im=s5_b_im, s5_c_re=s5_c_re, s5_c_im=s5_c_im,
        s5_d=s5_d, s5_w_glu=s5_w_glu, w_branch=w_branch, w_out=w_out, xa_w_q=xa_w_q, xa_w_k=xa_w_k,
        xa_w_v=xa_w_v, xa_q_norm=xa_q_norm, xa_k_norm=xa_k_norm, xa_w_o=xa_w_o,
        ffn_w_gate=ffn_w_gate, ffn_w_up=ffn_w_up, ffn_w_down=ffn_w_down)
    depth = w_in.shape[0]
    bp, tp, _ = x_prompt.shape
    bs, ts, _ = x_sample.shape
    n_pool = cache_sb_k.shape[1]
    cache_k = jnp.transpose(cache_sb_k, (0, 1, 3, 4, 2))
    cache_v = jnp.transpose(cache_sb_v, (0, 1, 3, 4, 2))
    mem_k_t = jnp.transpose(cache_mem_k, (0, 1, 3, 4, 2))
    mem_v_t = jnp.transpose(cache_mem_v, (0, 1, 3, 4, 2))
    past_len = page_table.shape[1] * PAGE_SIZE
    tt_s5_p = min(tp, 256)
    tt_s5_s = ts

    xp, xs = x_prompt, x_sample
    outs_p = [[] for _ in range(9)]
    outs_s = [[] for _ in range(7)]
    for l in range(depth):
        lw = _layer_weights(l, weights, tt_s5_p, tt_s5_s)
        mk_p, mv_p, mkt_p, mvt_p = memory_kv(mem_prompt.reshape(bp * N_MEM, D_MODEL), lw['norm_mem'],
                                             lw['xa_w_k'], lw['xa_w_v'], lw['xa_k_norm'])
        mem_t_shape = (1, bp, N_HEADS, HEAD_DIM, N_MEM)

        def sb_prompt_fn(q, k, v, lw=lw):
            return sb_prompt(q, k, v, lw['sb_bias'])

        xp, st = _trunk_layer(
            xp, 0, mkt_p.reshape(mem_t_shape), mvt_p.reshape(mem_t_shape), 0, sb_prompt_fn,
            jnp.zeros((bp, POOL_HDR, D_BR), F32), jnp.zeros((bp, 1, RWKV_COLS), F32),
            jnp.zeros((bp, N_HEADS, HEAD_DIM, HEAD_DIM), F32), jnp.zeros((bp, 1, S5_N), F32),
            jnp.zeros((bp, 1, S5_N), F32), lw, lw['s5_prompt'], True)
        for lst, val in zip(outs_p, (st[0], st[1], mk_p, mv_p) + st[2:]):
            lst.append(val)

        bias_rows = jnp.repeat(lw['sb_bias'], ts).reshape(N_HEADS * ts, 1)

        def sb_sample_fn(q, k, v, l=l, bias_rows=bias_rows):
            pad = ((0, 0), (0, PAGE_SIZE - ts), (0, 0))
            return sb_paged(q, bias_rows, jnp.pad(k, pad), jnp.pad(v, pad), cache_k, cache_v,
                            page_table, l)

        pool_buf = jnp.pad(state_pool[l], ((0, 0), (1, 0), (0, 0)))
        xs, st = _trunk_layer(
            xs, past_len, mem_k_t, mem_v_t, l, sb_sample_fn,
            pool_buf, state_rwkv_shift[l].reshape(bs, 1, RWKV_COLS), state_rwkv_wkv[l],
            state_s5_re[l].reshape(bs, 1, S5_N), state_s5_im[l].reshape(bs, 1, S5_N),
            lw, lw['s5_sample'], False)
        for lst, val in zip(outs_s, st):
            lst.append(val)

    def stack(lst, shape):
        return jnp.stack(lst, 0).reshape((depth,) + shape)

    kv_p = (bp, tp, N_HEADS, HEAD_DIM)
    kv_s = (bs, ts, N_HEADS, HEAD_DIM)
    mem_shape = (bp, N_MEM, N_HEADS, HEAD_DIM)
    return (xp, xs,
            stack(outs_p[0], kv_p), stack(outs_p[1], kv_p),
            stack(outs_p[2], mem_shape), stack(outs_p[3], mem_shape),
            stack(outs_p[4], (bp, POOL_BUF, D_BR)), stack(outs_p[5], (bp, RWKV_COLS)),
            stack(outs_p[6], (bp, N_HEADS, HEAD_DIM, HEAD_DIM)),
            stack(outs_p[7], (bp, S5_GROUPS, S5_STATE)), stack(outs_p[8], (bp, S5_GROUPS, S5_STATE)),
            stack(outs_s[0], kv_s), stack(outs_s[1], kv_s),
            stack(outs_s[2], (bs, POOL_BUF, D_BR)), stack(outs_s[3], (bs, RWKV_COLS)),
            stack(outs_s[4], (bs, N_HEADS, HEAD_DIM, HEAD_DIM)),
            stack(outs_s[5], (bs, S5_GROUPS, S5_STATE)), stack(outs_s[6], (bs, S5_GROUPS, S5_STATE)))
```

```python
import functools
import math

import jax
import jax.numpy as jnp
from jax import lax
from jax.experimental import pallas as pl
from jax.experimental.pallas import tpu as pltpu

F32 = jnp.float32
BF16 = jnp.bfloat16
HIGHEST = lax.Precision.HIGHEST

D_MODEL = 1024
D_BR = 256
HEAD_DIM = 64
N_HEADS = 4
POOL_WINDOWS = (2, 4, 8, 16)
POOL_BUF = 15
POOL_HDR = 16
RWKV_COLS = 1024
RWKV_LN_EPS = 64e-5
S5_GROUPS = 16
S5_STATE = 64
S5_N = S5_GROUPS * S5_STATE
RWKV_CHUNK = 64
PAGE_SIZE = 128
N_MEM = 256
D_FF = 2816
RMS_EPS = 1e-6
OFF_RWKV = 256
OFF_SB = 1280
OFF_S5 = 2048
OFF_GATE = 2304

VMEM_LIMIT = 56 * 1024 * 1024


def _params(sem):
    return pltpu.CompilerParams(dimension_semantics=sem, vmem_limit_bytes=VMEM_LIMIT)


def _dot(a, b):
    return jnp.dot(a.astype(BF16), b.astype(BF16), preferred_element_type=F32)


def _dot_hi(a, b):
    return jnp.dot(a, b, precision=HIGHEST, preferred_element_type=F32)


def _dot_nt(a, b, precision=None):
    return lax.dot_general(a, b, (((1,), (1,)), ((), ())), precision=precision,
                           preferred_element_type=F32)


def _dot_tn(a, b, precision=None):
    return lax.dot_general(a, b, (((0,), (0,)), ((), ())), precision=precision,
                           preferred_element_type=F32)


def _rms(x, g):
    ms = jnp.mean(x * x, axis=-1, keepdims=True)
    return x * lax.rsqrt(ms + RMS_EPS) * g


def _head_rms(x, gain_row):
    parts = []
    for h in range(N_HEADS):
        xs = x[:, h * HEAD_DIM:(h + 1) * HEAD_DIM]
        ms = jnp.mean(xs * xs, axis=-1, keepdims=True)
        parts.append(xs * lax.rsqrt(ms + RMS_EPS))
    return jnp.concatenate(parts, axis=-1) * gain_row


def _sigmoid(x):
    return 1.0 / (1.0 + jnp.exp(-x))


def _log_sigmoid(z):
    return jnp.minimum(z, 0.0) - jnp.log1p(jnp.exp(-jnp.abs(z)))


def _const_spec(shape):
    nd = len(shape)
    return pl.BlockSpec(shape, lambda *_: (0,) * nd)


def _in_proj_kernel(x_ref, g_ref, wp_ref, wr_ref, wsb_ref, ws5_ref, wg_ref, qg_ref, kg_ref, *refs):
    head_major_refs = refs[:1] + refs[8:] if len(refs) > 7 else ()
    pool_ref, rwkv_ref, q_ref, k_ref, v_ref, s5_ref, gate_ref = refs[len(refs) > 7:][:7]
    h = _rms(x_ref[...], g_ref[...]).astype(BF16)
    pool_ref[...] = jnp.dot(h, wp_ref[...], preferred_element_type=F32)
    rwkv_ref[...] = jnp.dot(h, wr_ref[...], preferred_element_type=F32)
    sb = jnp.dot(h, wsb_ref[...], preferred_element_type=F32)
    q = _head_rms(sb[:, :D_BR], qg_ref[...])
    k = _head_rms(sb[:, D_BR:2 * D_BR], kg_ref[...])
    v = sb[:, 2 * D_BR:]
    q_ref[...] = q
    k_ref[...] = k
    v_ref[...] = v
    s5_ref[...] = jnp.dot(h, ws5_ref[...], preferred_element_type=F32)
    gate_ref[...] = jnp.dot(h, wg_ref[...], preferred_element_type=F32)
    if head_major_refs:
        tails_ref, qh_ref, kh_ref, vh_ref = head_major_refs
        rows = q.shape[0]
        for hh in range(N_HEADS):
            cols = slice(hh * HEAD_DIM, (hh + 1) * HEAD_DIM)
            q_tail = jnp.broadcast_to(tails_ref[N_HEADS:N_HEADS + 1, :], (rows, HEAD_DIM))
            k_tail = jnp.broadcast_to(tails_ref[hh:hh + 1, :], (rows, HEAD_DIM))
            qh_ref[hh] = jnp.concatenate([q[:, cols] * (LOG2E * HEAD_DIM ** -0.5), q_tail], axis=-1).astype(BF16)
            kh_ref[hh] = jnp.concatenate([k[:, cols], k_tail], axis=-1).astype(BF16)
            vh_ref[hh] = v[:, cols].astype(BF16)


def in_proj(x, g, wp, wr, wsb, ws5, wg, qg, kg, sb_tails=None):
    m = x.shape[0]
    tm = min(m, 256)
    row = lambda n: pl.BlockSpec((tm, n), lambda i: (i, 0))
    widths = (D_BR, RWKV_COLS, D_BR, D_BR, D_BR, D_BR, 4 * D_MODEL)
    in_specs = [row(D_MODEL), _const_spec(g.shape), _const_spec(wp.shape), _const_spec(wr.shape),
                _const_spec(wsb.shape), _const_spec(ws5.shape), _const_spec(wg.shape),
                _const_spec(qg.shape), _const_spec(kg.shape)]
    args = [x, g, wp, wr, wsb, ws5, wg, qg, kg]
    out_specs = [row(n) for n in widths]
    out_shape = [jax.ShapeDtypeStruct((m, n), F32) for n in widths]
    if sb_tails is not None:
        in_specs.append(_const_spec(sb_tails.shape))
        args.append(sb_tails)
        for n in (2 * HEAD_DIM, 2 * HEAD_DIM, HEAD_DIM):
            out_specs.append(pl.BlockSpec((N_HEADS, tm, n), lambda i: (0, i, 0)))
            out_shape.append(jax.ShapeDtypeStruct((N_HEADS, m, n), BF16))
    return pl.pallas_call(
        _in_proj_kernel,
        grid=(m // tm,),
        in_specs=in_specs,
        out_specs=out_specs,
        out_shape=out_shape,
        compiler_params=_params(("arbitrary",)),
        name="in_proj",
    )(*args)


def _pool_kernel(u_ref, buf_ref, w_ref, scale_ref, o_ref, new_ref, ext_ref, *, tt, pos0):
    t = pl.program_id(1)

    @pl.when(t == 0)
    def _():
        ext_ref[0:POOL_HDR, :] = buf_ref[0]

    @pl.when(t > 0)
    def _():
        ext_ref[0:POOL_HDR, :] = ext_ref[tt:tt + POOL_HDR, :]

    u = u_ref[0]
    ext_ref[POOL_HDR:POOL_HDR + tt, :] = u
    lane = lax.broadcasted_iota(jnp.int32, (tt, D_BR), 1)
    pos = pos0 + t * tt + lax.broadcasted_iota(jnp.int32, (tt, D_BR), 0)
    s = u
    win_sum = jnp.zeros((tt, D_BR), F32)
    count = jnp.zeros((tt, D_BR), F32)
    for i in range(1, POOL_WINDOWS[-1]):
        s = s + ext_ref[POOL_HDR - i:POOL_HDR - i + tt, :]
        w = i + 1
        if w in POOL_WINDOWS:
            gi = POOL_WINDOWS.index(w)
            sel = (lane >= gi * HEAD_DIM) & (lane < (gi + 1) * HEAD_DIM)
            win_sum = jnp.where(sel, s, win_sum)
            count = jnp.where(sel, jnp.minimum(pos + 1, w).astype(F32), count)
    pooled = win_sum / count - u
    o_ref[0] = _dot_hi(pooled, w_ref[...]) * scale_ref[...]
    new_ref[0] = ext_ref[tt + 1:tt + POOL_HDR, :]


def pool_mixer(u, buf, w_bd, scale, pos0):
    b, t, _ = u.shape
    tt = min(t, 512)
    kern = functools.partial(_pool_kernel, tt=tt, pos0=pos0)
    return pl.pallas_call(
        kern,
        grid=(b, t // tt),
        in_specs=[pl.BlockSpec((1, tt, D_BR), lambda i, j: (i, j, 0)),
                  pl.BlockSpec((1, POOL_HDR, D_BR), lambda i, j: (i, 0, 0)),
                  _const_spec(w_bd.shape), _const_spec(scale.shape)],
        out_specs=[pl.BlockSpec((1, tt, D_BR), lambda i, j: (i, j, 0)),
                   pl.BlockSpec((1, POOL_BUF, D_BR), lambda i, j: (i, 0, 0))],
        out_shape=[jax.ShapeDtypeStruct((b, t, D_BR), F32),
                   jax.ShapeDtypeStruct((b, POOL_BUF, D_BR), F32)],
        scratch_shapes=[pltpu.VMEM((POOL_HDR + tt, D_BR), F32)],
        compiler_params=_params(("arbitrary", "arbitrary")),
        name="pool_mixer",
    )(u, buf, w_bd, scale)


def _rwkv_pre_kernel(p_ref, shift_ref, mu_ref, w0_ref, wup_ref, a0_ref, aup_ref, gup_ref,
                     kk_ref, ka_ref,
                     r_out, k_out, v_out, ld_out, kk_out, b_out, g_out, li_out, last_ref, *, tt, c):
    t = pl.program_id(1)

    @pl.when(t == 0)
    def _():
        last_ref[...] = shift_ref[0]

    p = p_ref[0]
    rowi = lax.broadcasted_iota(jnp.int32, (tt, RWKV_COLS), 0)
    prev = jnp.where(rowi == 0, last_ref[...], pltpu.roll(p, 1, 0))
    last_ref[...] = p[tt - 1:tt, :]
    pm = p + (prev - p) * mu_ref[...]
    r = pm[:, 0:D_BR]
    k = pm[:, D_BR:2 * D_BR]
    v = pm[:, 2 * D_BR:3 * D_BR]
    w_lo = pm[:, 768:832]
    a_lo = pm[:, 832:896]
    g_lo = pm[:, 896:1024]
    wz = w0_ref[...] + _dot(jnp.tanh(w_lo), wup_ref[...])
    w_log = _log_sigmoid(wz) - 0.5
    a = _sigmoid(a0_ref[...] + _dot(a_lo, aup_ref[...]))
    g = _dot(_sigmoid(g_lo), gup_ref[...])
    kk = k * kk_ref[...]
    parts = []
    for h in range(N_HEADS):
        ks = kk[:, h * HEAD_DIM:(h + 1) * HEAD_DIM]
        ss = jnp.sum(ks * ks, axis=-1, keepdims=True)
        parts.append(ks * lax.rsqrt(jnp.maximum(ss, 1e-24)))
    kk = jnp.concatenate(parts, axis=-1)
    ld = -jnp.exp(w_log)
    rowc = lax.broadcasted_iota(jnp.int32, (tt, D_BR), 0) & (c - 1)
    l_inc = ld
    off = 1
    while off < c:
        l_inc = l_inc + jnp.where(rowc >= off, pltpu.roll(l_inc, off, 0), 0.0)
        off *= 2
    r_out[0] = r
    k_out[0] = k * (1.0 + (a - 1.0) * ka_ref[...])
    v_out[0] = v
    ld_out[0] = ld
    kk_out[0] = kk
    b_out[0] = kk * a
    g_out[0] = g
    li_out[0] = l_inc


def rwkv_pre(p, shift, mu, w0, wup, a0, aup, gup, k_k, k_a):
    b, t, _ = p.shape
    tt = min(t, 512)
    kern = functools.partial(_rwkv_pre_kernel, tt=tt, c=min(t, RWKV_CHUNK))
    blk = pl.BlockSpec((1, tt, D_BR), lambda i, j: (i, j, 0))
    consts = (mu, w0, wup, a0, aup, gup, k_k, k_a)
    return pl.pallas_call(
        kern,
        grid=(b, t // tt),
        in_specs=[pl.BlockSpec((1, tt, RWKV_COLS), lambda i, j: (i, j, 0)),
                  pl.BlockSpec((1, 1, RWKV_COLS), lambda i, j: (i, 0, 0))]
                 + [_const_spec(c.shape) for c in consts],
        out_specs=[blk] * 8,
        out_shape=[jax.ShapeDtypeStruct((b, t, D_BR), F32)] * 8,
        scratch_shapes=[pltpu.VMEM((1, RWKV_COLS), F32)],
        compiler_params=_params(("arbitrary", "arbitrary")),
        name="rwkv_pre",
    )(p, shift, *consts)


_BMM_DIMS = {
    'nn': (((2,), (1,)), ((0,), (0,))),
    'nt': (((2,), (2,)), ((0,), (0,))),
}


def _bmm(a, b, dims):
    return lax.dot_general(a.astype(BF16), b.astype(BF16), _BMM_DIMS[dims], preferred_element_type=F32)


def _rwkv_chunk_kernel(r_ref, k_ref, v_ref, ld_ref, kk_ref, b_ref, g_ref, li_ref, s0_ref,
                       lnw_ref, lnb_ref, rk_ref, o_ref, sout_ref, s_ref, *, tt, c):
    t = pl.program_id(1)

    @pl.when(t == 0)
    def _():
        s_ref[...] = s0_ref[0]

    nch = tt // c
    nb = nch * N_HEADS

    def heads(x):
        return jnp.stack([x[ch * c:(ch + 1) * c, h * HEAD_DIM:(h + 1) * HEAD_DIM]
                          for ch in range(nch) for h in range(N_HEADS)])

    def head_rows(ref):
        return jnp.stack([ref[:, h * HEAD_DIM:(h + 1) * HEAD_DIM] for h in range(N_HEADS)])

    ri = lax.broadcasted_iota(jnp.int32, (nb, c, c), 1)
    ci = lax.broadcasted_iota(jnp.int32, (nb, c, c), 2)
    incl = (ri >= ci).astype(F32)
    strict = (ri > ci).astype(F32)
    eye = (ri == ci).astype(F32)
    n_sq = int(round(math.log2(c))) - 1

    r_all = r_ref[0]
    k_all = k_ref[0]
    v_all = v_ref[0]
    b_all = b_ref[0]
    l_inc = li_ref[0]
    l_tot = jnp.concatenate(
        [jnp.broadcast_to(l_inc[(ch + 1) * c - 1:(ch + 1) * c, :], (c, D_BR)) for ch in range(nch)], axis=0)
    e_neg = jnp.exp(-l_inc)
    e_end = jnp.exp(l_tot - l_inc)
    a_t = heads(-kk_ref[0] * jnp.exp(l_inc - ld_ref[0]))
    r_t = heads(r_all * jnp.exp(l_inc))
    b_t = heads(b_all * e_neg)
    k_t = heads(k_all * e_neg)
    b_e = heads(b_all * e_end)
    k_e = heads(k_all * e_end)
    d_tot = heads(jnp.exp(l_tot))[:, 0:1, :]
    r = heads(r_all)
    k = heads(k_all)
    v = heads(v_all)

    gram = _bmm(jnp.concatenate([a_t, r_t], axis=1), jnp.concatenate([b_t, k_t], axis=1), 'nt')
    m_ab = gram[:, :c, :c] * strict
    m_ak = gram[:, :c, c:] * strict
    p_rb = gram[:, c:, :c] * incl
    p_rk = gram[:, c:, c:] * incl
    pw = m_ab
    tinv = eye + m_ab
    for _ in range(n_sq):
        pw = _bmm(pw, pw, 'nn')
        tinv = tinv + _bmm(tinv, pw, 'nn')
    wu = _bmm(tinv, jnp.concatenate([a_t, _bmm(m_ak, v, 'nn')], axis=2), 'nn')
    o0 = _bmm(p_rk, v, 'nn')
    uv_rhs = jnp.concatenate([b_e, k_e], axis=1)

    lnw = head_rows(lnw_ref)
    lnb = head_rows(lnb_ref)
    rk = head_rows(rk_ref)
    s = s_ref[...]
    for ch in range(nch):
        sl = slice(ch * N_HEADS, (ch + 1) * N_HEADS)
        y = _bmm(jnp.concatenate([wu[sl, :, :HEAD_DIM], r_t[sl]], axis=1), s, 'nt')
        u = y[:, :c] + wu[sl, :, HEAD_DIM:]
        o = y[:, c:] + _bmm(p_rb[sl], u, 'nn') + o0[sl]
        uv_t = jnp.swapaxes(jnp.concatenate([u, v[sl]], axis=1), 1, 2)
        s = s * d_tot[sl] + _bmm(uv_t, uv_rhs[sl], 'nn')
        mean = jnp.mean(o, axis=-1, keepdims=True)
        var = jnp.mean(jnp.square(o - mean), axis=-1, keepdims=True)
        o = (o - mean) * lax.rsqrt(var + RWKV_LN_EPS) * lnw + lnb
        o = o + jnp.sum(r[sl] * k[sl] * rk, axis=-1, keepdims=True) * v[sl]
        rows = slice(ch * c, (ch + 1) * c)
        o_ref[0, rows, :] = (jnp.concatenate([o[h] for h in range(N_HEADS)], axis=-1)
                             * g_ref[0, rows, :])
    s_ref[...] = s
    sout_ref[0] = s


def rwkv_chunk(r, k, v, ld, kk, bv, g, l_inc, s0, ln_w, ln_b, r_k):
    b, t, _ = r.shape
    c = min(t, RWKV_CHUNK)
    tt = min(t, 256)
    kern = functools.partial(_rwkv_chunk_kernel, tt=tt, c=c)
    blk = pl.BlockSpec((1, tt, D_BR), lambda i, j: (i, j, 0))
    sblk = pl.BlockSpec((1, N_HEADS, HEAD_DIM, HEAD_DIM), lambda i, j: (i, 0, 0, 0))
    return pl.pallas_call(
        kern,
        grid=(b, t // tt),
        in_specs=[blk] * 8 + [sblk, _const_spec(ln_w.shape), _const_spec(ln_b.shape),
                              _const_spec(r_k.shape)],
        out_specs=[blk, sblk],
        out_shape=[jax.ShapeDtypeStruct((b, t, D_BR), F32),
                   jax.ShapeDtypeStruct((b, N_HEADS, HEAD_DIM, HEAD_DIM), F32)],
        scratch_shapes=[pltpu.VMEM((N_HEADS, HEAD_DIM, HEAD_DIM), F32)],
        compiler_params=_params(("arbitrary", "arbitrary")),
        name="rwkv_chunk",
    )(r, k, v, ld, kk, bv, g, l_inc, s0, ln_w, ln_b, r_k)


SB_GROUP = 4


LOG2E = 1.4426950408889634


def _softplus2(z2):
    return jnp.maximum(z2, 0.0) + jnp.log2(1.0 + jnp.exp2(-jnp.abs(z2)))


def _sb_prompt_kernel(q_ref, k_ref, v_ref, o_ref, *, tq):
    i = pl.program_id(1)
    q = q_ref[0]
    ri = lax.broadcasted_iota(jnp.int32, (tq, tq), 0)
    ci = lax.broadcasted_iota(jnp.int32, (tq, tq), 1)
    later = (ri > ci).astype(BF16)

    def scores(j):
        start = pl.multiple_of(j * tq, tq)
        return _dot_nt(q, k_ref[0, pl.ds(start, tq), :]), v_ref[0, pl.ds(start, tq), :]

    def full_blocks(j, n, car, acc):
        zv = [scores(j - b) for b in range(n)]
        sps = [_softplus2(z) for z, _ in zv]
        afts = [jnp.dot(sp.astype(BF16), later, preferred_element_type=F32) for sp in sps]
        atts = [jnp.exp2(zv[b][0] - sps[b] - afts[b]).astype(BF16) for b in range(n)]
        for b in range(n):
            acc = acc + jnp.exp2(-car) * jnp.dot(atts[b], zv[b][1], preferred_element_type=F32)
            car = car + jnp.sum(sps[b], axis=-1, keepdims=True)
        return car, acc

    z, vb = scores(i)
    mask = ci < ri
    sp = jnp.where(mask, _softplus2(z), 0.0)
    aft = jnp.dot(sp.astype(BF16), later, preferred_element_type=F32)
    att = jnp.where(mask, jnp.exp2(z - sp - aft), 0.0)
    acc = jnp.dot(att.astype(BF16), vb, preferred_element_type=F32)
    car = jnp.sum(sp, axis=-1, keepdims=True)

    rem = i & (SB_GROUP - 1)
    car, acc = lax.fori_loop(
        0, rem, lambda p, ca: full_blocks(i - 1 - p, 1, *ca), (car, acc))
    j0 = i - 1 - rem
    _, acc = lax.fori_loop(
        0, lax.shift_right_logical(i, SB_GROUP.bit_length() - 1),
        lambda p, ca: full_blocks(j0 - SB_GROUP * p, SB_GROUP, *ca), (car, acc))
    o_ref[0] = acc


def sb_prompt(q, k, v):
    h, t, _ = q.shape
    tq = min(t, 256)
    kern = functools.partial(_sb_prompt_kernel, tq=tq)
    return pl.pallas_call(
        kern,
        grid=(h, t // tq),
        in_specs=[pl.BlockSpec((1, tq, 2 * HEAD_DIM), lambda a, i: (a, i, 0)),
                  pl.BlockSpec((1, t, 2 * HEAD_DIM), lambda a, i: (a, 0, 0)),
                  pl.BlockSpec((1, t, HEAD_DIM), lambda a, i: (a, 0, 0))],
        out_specs=pl.BlockSpec((1, tq, HEAD_DIM), lambda a, i: (a, i, 0)),
        out_shape=jax.ShapeDtypeStruct((h, t, HEAD_DIM), F32),
        compiler_params=_params(("arbitrary", "arbitrary")),
        name="sb_prompt",
    )(q, k, v)


PAGES_PER_STEP = 16


def _sb_paged_kernel(pt_ref, q_ref, bias_ref, kn_ref, vn_ref, *rest, t_new, past_len):
    nk = PAGES_PER_STEP
    k_pages = rest[:nk]
    v_pages = rest[nk:2 * nk]
    o_ref = rest[2 * nk]
    acc_ref, car_ref = rest[2 * nk + 1:]
    s = pl.program_id(1)
    q = q_ref[0] * (LOG2E * HEAD_DIM ** -0.5)
    qh = [q[:, h * HEAD_DIM:(h + 1) * HEAD_DIM].astype(BF16) for h in range(N_HEADS)]
    bias = bias_ref[...]
    ri = lax.broadcasted_iota(jnp.int32, (PAGE_SIZE, PAGE_SIZE), 0)
    ci = lax.broadcasted_iota(jnp.int32, (PAGE_SIZE, PAGE_SIZE), 1)
    later = (ri > ci).astype(BF16)

    def head_rows(x):
        return [x[h * t_new:(h + 1) * t_new] for h in range(N_HEADS)]

    @pl.when(s == 0)
    def _():
        kn = kn_ref[0].astype(BF16)
        vn = vn_ref[0].astype(BF16)
        cols = [slice(h * HEAD_DIM, (h + 1) * HEAD_DIM) for h in range(N_HEADS)]
        z = jnp.concatenate([_dot_nt(qh[h], kn[:, cols[h]]) for h in range(N_HEADS)], axis=0) + bias
        qi = jnp.concatenate([lax.broadcasted_iota(jnp.int32, (t_new, PAGE_SIZE), 0)] * N_HEADS, axis=0)
        mask = lax.broadcasted_iota(jnp.int32, (N_HEADS * t_new, PAGE_SIZE), 1) < qi
        sp = jnp.where(mask, _softplus2(z), 0.0)
        aft = jnp.dot(sp.astype(BF16), later, preferred_element_type=F32)
        att = head_rows(jnp.where(mask, jnp.exp2(z - sp - aft), 0.0).astype(BF16))
        acc_ref[...] = jnp.concatenate(
            [jnp.dot(att[h], vn[:, cols[h]], preferred_element_type=F32) for h in range(N_HEADS)], axis=0)
        car_ref[...] = jnp.sum(sp, axis=-1, keepdims=True)

    def gather_head(pages, h):
        return jnp.concatenate([p[0, 0, h] for p in pages], axis=1).astype(BF16)

    z = jnp.concatenate(
        [jnp.dot(qh[h], gather_head(k_pages, h), preferred_element_type=F32) for h in range(N_HEADS)],
        axis=0) + bias
    sp = _softplus2(z)
    car = car_ref[...]
    afts = []
    for rr in range(nk):
        sp_rr = sp[:, rr * PAGE_SIZE:(rr + 1) * PAGE_SIZE]
        afts.append(jnp.dot(sp_rr.astype(BF16), later, preferred_element_type=F32) + car)
        car = car + jnp.sum(sp_rr, axis=-1, keepdims=True)
    att = head_rows(jnp.exp2(z - sp - jnp.concatenate(afts, axis=1)).astype(BF16))
    acc_ref[...] += jnp.concatenate(
        [_dot_nt(att[h], gather_head(v_pages, h)) for h in range(N_HEADS)], axis=0)
    car_ref[...] = car

    @pl.when(s == pl.num_programs(1) - 1)
    def _():
        o_ref[0] = jnp.concatenate(head_rows(acc_ref[...]), axis=-1)


def sb_paged(q, bias_rows, k_new, v_new, cache_k, cache_v, page_table, layer):
    b, t_new, _ = q.shape
    n_pages = page_table.shape[1]
    past_len = n_pages * PAGE_SIZE
    nk = PAGES_PER_STEP
    n_steps = n_pages // nk
    kern = functools.partial(_sb_paged_kernel, t_new=t_new, past_len=past_len)

    def page_spec(rr):
        return pl.BlockSpec(
            (1, 1, N_HEADS, HEAD_DIM, PAGE_SIZE),
            lambda i, s, pt: (layer, pt[i, n_pages - 1 - (s * nk + rr)], 0, 0, 0))

    n_rows = N_HEADS * t_new
    grid_spec = pltpu.PrefetchScalarGridSpec(
        num_scalar_prefetch=1,
        grid=(b, n_steps),
        in_specs=[pl.BlockSpec((1, t_new, D_BR), lambda i, s, pt: (i, 0, 0)),
                  pl.BlockSpec((n_rows, 1), lambda i, s, pt: (0, 0)),
                  pl.BlockSpec((1, PAGE_SIZE, D_BR), lambda i, s, pt: (i, 0, 0)),
                  pl.BlockSpec((1, PAGE_SIZE, D_BR), lambda i, s, pt: (i, 0, 0))]
                 + [page_spec(rr) for rr in range(nk)] * 2,
        out_specs=pl.BlockSpec((1, t_new, D_BR), lambda i, s, pt: (i, 0, 0)),
        scratch_shapes=[pltpu.VMEM((n_rows, HEAD_DIM), F32), pltpu.VMEM((n_rows, 1), F32)],
    )
    return pl.pallas_call(
        kern,
        grid_spec=grid_spec,
        out_shape=jax.ShapeDtypeStruct((b, t_new, D_BR), F32),
        compiler_params=_params(("arbitrary", "arbitrary")),
        name="sb_paged",
    )(page_table, q, bias_rows, k_new, v_new, *([cache_k] * nk), *([cache_v] * nk))


def _s5_kernel(u_ref, hre_ref, him_ref, bre_ref, bim_ref, cre_ref, cim_ref, d_ref, wglu_ref,
               powr_ref, powi_ref, lvr_ref, lvi_ref,
               o_ref, sre_ref, sim_ref, cr_ref, cim_s_ref, *, tt):
    t = pl.program_id(1)

    @pl.when(t == 0)
    def _():
        cr_ref[...] = hre_ref[0]
        cim_s_ref[...] = him_ref[0]

    u = u_ref[0]
    xr = _dot(u, bre_ref[...])
    xi = _dot(u, bim_ref[...])
    rowi = lax.broadcasted_iota(jnp.int32, (tt, S5_N), 0)
    off = 1
    lvl = 0
    while off < tt:
        lr = lvr_ref[lvl:lvl + 1, :]
        li = lvi_ref[lvl:lvl + 1, :]
        m = rowi >= off
        sr = jnp.where(m, pltpu.roll(xr, off, 0), 0.0)
        si = jnp.where(m, pltpu.roll(xi, off, 0), 0.0)
        xr, xi = xr + lr * sr - li * si, xi + lr * si + li * sr
        off *= 2
        lvl += 1
    h0r = cr_ref[...]
    h0i = cim_s_ref[...]
    pr = powr_ref[...]
    pi = powi_ref[...]
    xr, xi = xr + pr * h0r - pi * h0i, xi + pr * h0i + pi * h0r
    cr_ref[...] = xr[tt - 1:tt, :]
    cim_s_ref[...] = xi[tt - 1:tt, :]
    sre_ref[0] = xr[tt - 1:tt, :]
    sim_ref[0] = xi[tt - 1:tt, :]
    y = _dot(xr, cre_ref[...]) - _dot(xi, cim_ref[...]) + d_ref[...] * u
    z = _dot(jax.nn.gelu(y), wglu_ref[...])
    o_ref[0] = z[:, :D_BR] * _sigmoid(z[:, D_BR:])


def s5_mixer(u, h_re, h_im, b_re, b_im, c_re, c_im, d, w_glu, pow_re, pow_im, lv_re, lv_im):
    b, t, _ = u.shape
    tt = pow_re.shape[0]
    kern = functools.partial(_s5_kernel, tt=tt)
    consts = (b_re, b_im, c_re, c_im, d, w_glu, pow_re, pow_im, lv_re, lv_im)
    sblk = pl.BlockSpec((1, 1, S5_N), lambda i, j: (i, 0, 0))
    return pl.pallas_call(
        kern,
        grid=(b, t // tt),
        in_specs=[pl.BlockSpec((1, tt, D_BR), lambda i, j: (i, j, 0)), sblk, sblk]
                 + [_const_spec(c.shape) for c in consts],
        out_specs=[pl.BlockSpec((1, tt, D_BR), lambda i, j: (i, j, 0)), sblk, sblk],
        out_shape=[jax.ShapeDtypeStruct((b, t, D_BR), F32),
                   jax.ShapeDtypeStruct((b, 1, S5_N), F32),
                   jax.ShapeDtypeStruct((b, 1, S5_N), F32)],
        scratch_shapes=[pltpu.VMEM((1, S5_N), F32), pltpu.VMEM((1, S5_N), F32)],
        compiler_params=_params(("arbitrary", "arbitrary")),
        name="s5_mixer",
    )(u, h_re, h_im, *consts)


def _merge_kernel(x_ref, bp_ref, br_ref, bs_ref, b5_ref, gate_ref, wb_ref, wo_ref, o_ref):
    if len(bs_ref.shape) == 3:
        o_sb = jnp.concatenate([bs_ref[h] for h in range(N_HEADS)], axis=-1)
    else:
        o_sb = bs_ref[...]
    merged = None
    for n, br in enumerate((bp_ref[...], br_ref[...], o_sb, b5_ref[...])):
        lifted = _dot(br, wb_ref[n])
        term = _sigmoid(gate_ref[:, n * D_MODEL:(n + 1) * D_MODEL]) * lifted
        merged = term if merged is None else merged + term
    o_ref[...] = x_ref[...] + _dot(merged, wo_ref[...])


def merge(x, o_pool, o_rwkv, o_sb, o_s5, gates, w_branch, w_out):
    m = x.shape[0]
    tm = min(m, 256)
    row = lambda n: pl.BlockSpec((tm, n), lambda i: (i, 0))
    sb_spec = row(D_BR) if o_sb.ndim == 2 else pl.BlockSpec((N_HEADS, tm, HEAD_DIM), lambda i: (0, i, 0))
    return pl.pallas_call(
        _merge_kernel,
        grid=(m // tm,),
        in_specs=[row(D_MODEL), row(D_BR), row(D_BR), sb_spec, row(D_BR), row(4 * D_MODEL),
                  _const_spec(w_branch.shape), _const_spec(w_out.shape)],
        out_specs=row(D_MODEL),
        out_shape=jax.ShapeDtypeStruct((m, D_MODEL), F32),
        compiler_params=_params(("arbitrary",)),
        name="merge",
    )(x, o_pool, o_rwkv, o_sb, o_s5, gates, w_branch, w_out)


def _xattn_kernel(x_ref, mk_ref, mv_ref, g_ref, wq_ref, qg_ref, wo_ref, o_ref):
    x = x_ref[0]
    hn = _rms(x, g_ref[...])
    q = _head_rms(_dot(hn, wq_ref[...]), qg_ref[...]) * HEAD_DIM ** -0.5
    outs = []
    for h in range(N_HEADS):
        cols = slice(h * HEAD_DIM, (h + 1) * HEAD_DIM)
        s = _dot(q[:, cols], mk_ref[0, 0, h])
        s = s - jnp.max(s, axis=-1, keepdims=True)
        e = jnp.exp(s)
        pr = e / jnp.sum(e, axis=-1, keepdims=True)
        outs.append(_dot_nt(pr.astype(BF16), mv_ref[0, 0, h].astype(BF16)))
    o = jnp.concatenate(outs, axis=-1)
    o_ref[0] = x + _dot(o, wo_ref[...])


def cross_attention(x, mem_k, mem_v, layer, g, wq, qg, wo):
    b, t, _ = x.shape
    tt = min(t, 512)
    mem_spec = pl.BlockSpec((1, 1, N_HEADS, HEAD_DIM, N_MEM), lambda i, j: (layer, i, 0, 0, 0))
    return pl.pallas_call(
        _xattn_kernel,
        grid=(b, t // tt),
        in_specs=[pl.BlockSpec((1, tt, D_MODEL), lambda i, j: (i, j, 0)), mem_spec, mem_spec,
                  _const_spec(g.shape), _const_spec(wq.shape), _const_spec(qg.shape),
                  _const_spec(wo.shape)],
        out_specs=pl.BlockSpec((1, tt, D_MODEL), lambda i, j: (i, j, 0)),
        out_shape=jax.ShapeDtypeStruct((b, t, D_MODEL), F32),
        compiler_params=_params(("arbitrary", "arbitrary")),
        name="cross_attention",
    )(x, mem_k, mem_v, g, wq, qg, wo)


def _memkv_kernel(mem_ref, g_ref, wk_ref, wv_ref, kg_ref, k_ref, v_ref, kt_ref, vt_ref):
    mn = _rms(mem_ref[...], g_ref[...])
    k = _head_rms(_dot(mn, wk_ref[...]), kg_ref[...])
    v = _dot(mn, wv_ref[...])
    k_ref[...] = k
    v_ref[...] = v
    kt_ref[...] = k.T
    vt_ref[...] = v.T


def memory_kv(mem, g, wk, wv, kg):
    m = mem.shape[0]
    args = (mem, g, wk, wv, kg)
    return pl.pallas_call(
        _memkv_kernel,
        grid=(1,),
        in_specs=[_const_spec(a.shape) for a in args],
        out_specs=[_const_spec((m, D_BR))] * 2 + [_const_spec((D_BR, m))] * 2,
        out_shape=[jax.ShapeDtypeStruct((m, D_BR), F32)] * 2 + [jax.ShapeDtypeStruct((D_BR, m), F32)] * 2,
        compiler_params=_params(("arbitrary",)),
        name="memory_kv",
    )(*args)


def _ffn_kernel(x_ref, g_ref, wg_ref, wu_ref, wd_ref, o_ref):
    x = x_ref[...]
    hn = _rms(x, g_ref[...]).astype(BF16)
    a = jnp.dot(hn, wg_ref[...], preferred_element_type=F32)
    bq = jnp.dot(hn, wu_ref[...], preferred_element_type=F32)
    act = a * _sigmoid(a) * bq
    o_ref[...] = x + _dot(act, wd_ref[...])


def ffn(x, g, w_gate, w_up, w_down):
    m = x.shape[0]
    tm = min(m, 256)
    row = pl.BlockSpec((tm, D_MODEL), lambda i: (i, 0))
    return pl.pallas_call(
        _ffn_kernel,
        grid=(m // tm,),
        in_specs=[row, _const_spec(g.shape), _const_spec(w_gate.shape), _const_spec(w_up.shape),
                  _const_spec(w_down.shape)],
        out_specs=row,
        out_shape=jax.ShapeDtypeStruct((m, D_MODEL), F32),
        compiler_params=_params(("arbitrary",)),
        name="ffn",
    )(x, g, w_gate, w_up, w_down)


def _block_diag(blocks):
    g, m, n = blocks.shape
    eye = jnp.eye(g, dtype=blocks.dtype)
    return (eye[:, None, :, None] * blocks[:, :, None, :]).reshape(g * m, g * n)


def _tile_heads(v):
    return jnp.tile(v, N_HEADS).reshape(1, N_HEADS * v.shape[0])


def _cmul(ar, ai, br, bi):
    return ar * br - ai * bi, ar * bi + ai * br


def _s5_constants(a_re, a_im, log_dt, b_re, b_im, c_re, c_im, tt):
    dt_g = jnp.exp(log_dt)[:, None]
    mag = jnp.exp(dt_g * a_re)
    ab_re, ab_im = mag * jnp.cos(dt_g * a_im), mag * jnp.sin(dt_g * a_im)
    den = a_re * a_re + a_im * a_im
    n_re = ab_re - 1.0
    f_re = (n_re * a_re + ab_im * a_im) / den
    f_im = (ab_im * a_re - n_re * a_im) / den
    bb_re = f_re[..., None] * b_re - f_im[..., None] * b_im
    bb_im = f_re[..., None] * b_im + f_im[..., None] * b_re
    bmat_re = _block_diag(jnp.swapaxes(bb_re, 1, 2))
    bmat_im = _block_diag(jnp.swapaxes(bb_im, 1, 2))
    cmat_re = _block_diag(jnp.swapaxes(c_re, 1, 2))
    cmat_im = _block_diag(jnp.swapaxes(c_im, 1, 2))
    lam_re = ab_re.reshape(1, S5_N)
    lam_im = ab_im.reshape(1, S5_N)
    pow_re, pow_im = lam_re, lam_im
    lv_re, lv_im = [lam_re], [lam_im]
    cur_re, cur_im = lam_re, lam_im
    n = 1
    while n < tt:
        nr, ni = _cmul(pow_re, pow_im, cur_re, cur_im)
        pow_re = jnp.concatenate([pow_re, nr], axis=0)
        pow_im = jnp.concatenate([pow_im, ni], axis=0)
        cur_re, cur_im = _cmul(cur_re, cur_im, cur_re, cur_im)
        lv_re.append(cur_re)
        lv_im.append(cur_im)
        n *= 2
    n_lv = max(len(lv_re) - 1, 1)
    lv_re = jnp.concatenate(lv_re[:n_lv], axis=0)
    lv_im = jnp.concatenate(lv_im[:n_lv], axis=0)
    return (bmat_re.astype(BF16), bmat_im.astype(BF16), cmat_re.astype(BF16), cmat_im.astype(BF16),
            pow_re, pow_im, lv_re, lv_im)


def _layer_weights(l, w, tt_prompt, tt_sample):
    w_in = w['w_in'][l]
    lw = {
        'norm_mix': w['norm_mix'][l].reshape(1, D_MODEL),
        'norm_cross': w['norm_cross'][l].reshape(1, D_MODEL),
        'norm_mem': w['norm_mem'][l].reshape(1, D_MODEL),
        'norm_ffn': w['norm_ffn'][l].reshape(1, D_MODEL),
        'w_pool_in': w_in[:, :OFF_RWKV].astype(BF16),
        'w_rwkv_in': w_in[:, OFF_RWKV:OFF_SB].astype(BF16),
        'w_sb_in': w_in[:, OFF_SB:OFF_S5].astype(BF16),
        'w_s5_in': w_in[:, OFF_S5:OFF_GATE].astype(BF16),
        'w_gate_in': w_in[:, OFF_GATE:].astype(BF16),
        'sb_q_norm': _tile_heads(w['sb_q_norm'][l]),
        'sb_k_norm': _tile_heads(w['sb_k_norm'][l]),
        'sb_bias': w['sb_bias'][l],
        'pool_w': _block_diag(w['pool_w'][l]),
        'pool_scale': w['pool_scale'][l].reshape(1, D_BR),
        'rwkv_mu': w['rwkv_mu'][l].reshape(1, RWKV_COLS),
        'rwkv_w0': w['rwkv_w0'][l].reshape(1, D_BR),
        'rwkv_w_up': w['rwkv_w_up'][l].astype(BF16),
        'rwkv_a0': w['rwkv_a0'][l].reshape(1, D_BR),
        'rwkv_a_up': w['rwkv_a_up'][l].astype(BF16),
        'rwkv_g_up': w['rwkv_g_up'][l].astype(BF16),
        'rwkv_k_k': w['rwkv_k_k'][l].reshape(1, D_BR),
        'rwkv_k_a': w['rwkv_k_a'][l].reshape(1, D_BR),
        'rwkv_r_k': w['rwkv_r_k'][l].reshape(1, D_BR),
        'rwkv_ln_w': w['rwkv_ln_w'][l].reshape(1, D_BR),
        'rwkv_ln_b': w['rwkv_ln_b'][l].reshape(1, D_BR),
        's5_d': w['s5_d'][l].reshape(1, D_BR),
        's5_w_glu': w['s5_w_glu'][l].astype(BF16),
        'w_branch': w['w_branch'][l].astype(BF16),
        'w_out': w['w_out'][l].astype(BF16),
        'xa_w_q': w['xa_w_q'][l].astype(BF16),
        'xa_w_k': w['xa_w_k'][l].astype(BF16),
        'xa_w_v': w['xa_w_v'][l].astype(BF16),
        'xa_q_norm': _tile_heads(w['xa_q_norm'][l]),
        'xa_k_norm': _tile_heads(w['xa_k_norm'][l]),
        'xa_w_o': w['xa_w_o'][l].astype(BF16),
        'ffn_w_gate': w['ffn_w_gate'][l].astype(BF16),
        'ffn_w_up': w['ffn_w_up'][l].astype(BF16),
        'ffn_w_down': w['ffn_w_down'][l].astype(BF16),
    }
    b2 = w['sb_bias'][l] * LOG2E
    p1 = b2.astype(BF16).astype(F32)
    p2 = (b2 - p1).astype(BF16).astype(F32)
    p3 = (b2 - p1 - p2).astype(BF16).astype(F32)
    k_tails = jnp.pad(jnp.stack([p1, p2, p3], axis=1), ((0, 0), (0, HEAD_DIM - 3)))
    q_tail = jnp.pad(jnp.ones((1, 3), F32), ((0, 0), (0, HEAD_DIM - 3)))
    lw['sb_tails'] = jnp.concatenate([k_tails, q_tail], axis=0)
    s5_args = (w['s5_a_re'][l], w['s5_a_im'][l], w['s5_log_dt'][l], w['s5_b_re'][l], w['s5_b_im'][l],
               w['s5_c_re'][l], w['s5_c_im'][l])
    assert tt_prompt >= tt_sample
    s5c = _s5_constants(*s5_args, tt_prompt)
    lw['s5_prompt'] = s5c
    if tt_sample == tt_prompt:
        lw['s5_sample'] = s5c
    else:
        n_lv = max(tt_sample.bit_length() - 1, 1)
        lw['s5_sample'] = s5c[:4] + (s5c[4][:tt_sample], s5c[5][:tt_sample], s5c[6][:n_lv], s5c[7][:n_lv])
    return lw


def _trunk_layer(x, pos0, mem_k, mem_v, mem_layer, sb_fn, pool_buf, shift, wkv, s5_re, s5_im, lw, s5c,
                 fresh):
    b, t, _ = x.shape
    m = b * t
    xf = x.reshape(m, D_MODEL)
    u_pool, p_rwkv, q, k, v, u_s5, gates, *head_major = in_proj(
        xf, lw['norm_mix'], lw['w_pool_in'], lw['w_rwkv_in'], lw['w_sb_in'], lw['w_s5_in'],
        lw['w_gate_in'], lw['sb_q_norm'], lw['sb_k_norm'], lw['sb_tails'] if fresh else None)
    o_pool, new_pool = pool_mixer(u_pool.reshape(b, t, D_BR), pool_buf, lw['pool_w'],
                                  lw['pool_scale'], pos0)
    p3 = p_rwkv.reshape(b, t, RWKV_COLS)
    pre = rwkv_pre(p3, shift, lw['rwkv_mu'], lw['rwkv_w0'], lw['rwkv_w_up'], lw['rwkv_a0'],
                   lw['rwkv_a_up'], lw['rwkv_g_up'], lw['rwkv_k_k'], lw['rwkv_k_a'])
    o_rwkv, new_wkv = rwkv_chunk(*pre, wkv, lw['rwkv_ln_w'], lw['rwkv_ln_b'], lw['rwkv_r_k'])
    new_shift = p3[:, -1]
    q3, k3, v3 = (z.reshape(b, t, D_BR) for z in (q, k, v))
    o_sb = sb_fn(*head_major) if fresh else sb_fn(q3, k3, v3).reshape(m, D_BR)
    o_s5, new_re, new_im = s5_mixer(u_s5.reshape(b, t, D_BR), s5_re, s5_im, s5c[0], s5c[1], s5c[2],
                                    s5c[3], lw['s5_d'], lw['s5_w_glu'], *s5c[4:])
    x1 = merge(xf, o_pool.reshape(m, D_BR), o_rwkv.reshape(m, D_BR), o_sb,
               o_s5.reshape(m, D_BR), gates, lw['w_branch'], lw['w_out'])
    x2 = cross_attention(x1.reshape(b, t, D_MODEL), mem_k, mem_v, mem_layer, lw['norm_cross'],
                         lw['xa_w_q'], lw['xa_q_norm'], lw['xa_w_o'])
    x3 = ffn(x2.reshape(m, D_MODEL), lw['norm_ffn'], lw['ffn_w_gate'], lw['ffn_w_up'],
             lw['ffn_w_down'])
    return x3.reshape(b, t, D_MODEL), (k3, v3, new_pool, new_shift, new_wkv, new_re, new_im)


def kernel(x_prompt, x_sample, cache_sb_k, cache_sb_v, cache_mem_k, cache_mem_v, state_pool, state_rwkv_shift, state_rwkv_wkv, state_s5_re, state_s5_im, page_table, mem_prompt, norm_mix, norm_cross, norm_mem, norm_ffn, w_in, pool_w, pool_scale, rwkv_mu, rwkv_w0, rwkv_w_up, rwkv_a0, rwkv_a_up, rwkv_g_up, rwkv_k_k, rwkv_k_a, rwkv_r_k, rwkv_ln_w, rwkv_ln_b, sb_q_norm, sb_k_norm, sb_bias, s5_a_re, s5_a_im, s5_log_dt, s5_b_re, s5_b_im, s5_c_re, s5_c_im, s5_d, s5_w_glu, w_branch, w_out, xa_w_q, xa_w_k, xa_w_v, xa_q_norm, xa_k_norm, xa_w_o, ffn_w_gate, ffn_w_up, ffn_w_down):
    weights = dict(
        norm_mix=norm_mix, norm_cross=norm_cross, norm_mem=norm_mem, norm_ffn=norm_ffn, w_in=w_in,
        pool_w=pool_w, pool_scale=pool_scale, rwkv_mu=rwkv_mu, rwkv_w0=rwkv_w0, rwkv_w_up=rwkv_w_up,
        rwkv_a0=rwkv_a0, rwkv_a_up=rwkv_a_up, rwkv_g_up=rwkv_g_up, rwkv_k_k=rwkv_k_k,
        rwkv_k_a=rwkv_k_a, rwkv_r_k=rwkv_r_k, rwkv_ln_w=rwkv_ln_w, rwkv_ln_b=rwkv_ln_b,
        sb_q_norm=sb_q_norm, sb_k_norm=sb_k_norm, sb_bias=sb_bias, s5_a_re=s5_a_re, s5_a_im=s5_a_im,
        s5_log_dt=s5_log_dt, s5_b_re=s5_b_re, s5_b_im=s5_b_im, s5_c_re=s5_c_re, s5_c_im=s5_c_im,
        s5_d=s5_d, s5_w_glu=s5_w_glu, w_branch=w_branch, w_out=w_out, xa_w_q=xa_w_q, xa_w_k=xa_w_k,
        xa_w_v=xa_w_v, xa_q_norm=xa_q_norm, xa_k_norm=xa_k_norm, xa_w_o=xa_w_o,
        ffn_w_gate=ffn_w_gate, ffn_w_up=ffn_w_up, ffn_w_down=ffn_w_down)
    depth = w_in.shape[0]
    bp, tp, _ = x_prompt.shape
    bs, ts, _ = x_sample.shape
    n_pool = cache_sb_k.shape[1]
    cache_k = jnp.transpose(cache_sb_k, (0, 1, 3, 4, 2))
    cache_v = jnp.transpose(cache_sb_v, (0, 1, 3, 4, 2))
    mem_k_t = jnp.transpose(cache_mem_k, (0, 1, 3, 4, 2))
    mem_v_t = jnp.transpose(cache_mem_v, (0, 1, 3, 4, 2))
    past_len = page_table.shape[1] * PAGE_SIZE
    tt_s5_p = min(tp, 256)
    tt_s5_s = ts

    xp, xs = x_prompt, x_sample
    outs_p = [[] for _ in range(9)]
    outs_s = [[] for _ in range(7)]
    for l in range(depth):
        lw = _layer_weights(l, weights, tt_s5_p, tt_s5_s)
        mk_p, mv_p, mkt_p, mvt_p = memory_kv(mem_prompt.reshape(bp * N_MEM, D_MODEL), lw['norm_mem'],
                                             lw['xa_w_k'], lw['xa_w_v'], lw['xa_k_norm'])
        mem_t_shape = (1, bp, N_HEADS, HEAD_DIM, N_MEM)

        xp, st = _trunk_layer(
            xp, 0, mkt_p.reshape(mem_t_shape), mvt_p.reshape(mem_t_shape), 0, sb_prompt,
            jnp.zeros((bp, POOL_HDR, D_BR), F32), jnp.zeros((bp, 1, RWKV_COLS), F32),
            jnp.zeros((bp, N_HEADS, HEAD_DIM, HEAD_DIM), F32), jnp.zeros((bp, 1, S5_N), F32),
            jnp.zeros((bp, 1, S5_N), F32), lw, lw['s5_prompt'], True)
        for lst, val in zip(outs_p, (st[0], st[1], mk_p, mv_p) + st[2:]):
            lst.append(val)

        bias_rows = jnp.repeat(lw['sb_bias'] * LOG2E, ts).reshape(N_HEADS * ts, 1)

        def sb_sample_fn(q, k, v, l=l, bias_rows=bias_rows):
            pad = ((0, 0), (0, PAGE_SIZE - ts), (0, 0))
            return sb_paged(q, bias_rows, jnp.pad(k, pad), jnp.pad(v, pad), cache_k, cache_v,
                            page_table, l)

        pool_buf = jnp.pad(state_pool[l], ((0, 0), (1, 0), (0, 0)))
        xs, st = _trunk_layer(
            xs, past_len, mem_k_t, mem_v_t, l, sb_sample_fn,
            pool_buf, state_rwkv_shift[l].reshape(bs, 1, RWKV_COLS), state_rwkv_wkv[l],
            state_s5_re[l].reshape(bs, 1, S5_N), state_s5_im[l].reshape(bs, 1, S5_N),
            lw, lw['s5_sample'], False)
        for lst, val in zip(outs_s, st):
            lst.append(val)

    def stack(lst, shape):
        return jnp.stack(lst, 0).reshape((depth,) + shape)

    kv_p = (bp, tp, N_HEADS, HEAD_DIM)
    kv_s = (bs, ts, N_HEADS, HEAD_DIM)
    mem_shape = (bp, N_MEM, N_HEADS, HEAD_DIM)
    return (xp, xs,
            stack(outs_p[0], kv_p), stack(outs_p[1], kv_p),
            stack(outs_p[2], mem_shape), stack(outs_p[3], mem_shape),
            stack(outs_p[4], (bp, POOL_BUF, D_BR)), stack(outs_p[5], (bp, RWKV_COLS)),
            stack(outs_p[6], (bp, N_HEADS, HEAD_DIM, HEAD_DIM)),
            stack(outs_p[7], (bp, S5_GROUPS, S5_STATE)), stack(outs_p[8], (bp, S5_GROUPS, S5_STATE)),
            stack(outs_s[0], kv_s), stack(outs_s[1], kv_s),
            stack(outs_s[2], (bs, POOL_BUF, D_BR)), stack(outs_s[3], (bs, RWKV_COLS)),
            stack(outs_s[4], (bs, N_HEADS, HEAD_DIM, HEAD_DIM)),
            stack(outs_s[5], (bs, S5_GROUPS, S5_STATE)), stack(outs_s[6], (bs, S5_GROUPS, S5_STATE)))
```

```python
import functools
import math

import jax
import jax.numpy as jnp
from jax import lax
from jax.experimental import pallas as pl
from jax.experimental.pallas import tpu as pltpu

F32 = jnp.float32
BF16 = jnp.bfloat16
HIGHEST = lax.Precision.HIGHEST

D_MODEL = 1024
D_BR = 256
HEAD_DIM = 64
N_HEADS = 4
POOL_WINDOWS = (2, 4, 8, 16)
POOL_BUF = 15
POOL_HDR = 16
RWKV_COLS = 1024
RWKV_LN_EPS = 64e-5
S5_GROUPS = 16
S5_STATE = 64
S5_N = S5_GROUPS * S5_STATE
RWKV_CHUNK = 64
PAGE_SIZE = 128
N_MEM = 256
D_FF = 2816
RMS_EPS = 1e-6
OFF_RWKV = 256
OFF_SB = 1280
OFF_S5 = 2048
OFF_GATE = 2304

VMEM_LIMIT = 56 * 1024 * 1024


def _params(sem):
    return pltpu.CompilerParams(dimension_semantics=sem, vmem_limit_bytes=VMEM_LIMIT)


def _dot(a, b):
    return jnp.dot(a.astype(BF16), b.astype(BF16), preferred_element_type=F32)


def _dot_hi(a, b):
    return jnp.dot(a, b, precision=HIGHEST, preferred_element_type=F32)


def _dot_nt(a, b, precision=None):
    return lax.dot_general(a, b, (((1,), (1,)), ((), ())), precision=precision,
                           preferred_element_type=F32)


def _dot_tn(a, b, precision=None):
    return lax.dot_general(a, b, (((0,), (0,)), ((), ())), precision=precision,
                           preferred_element_type=F32)


def _rms(x, g):
    ms = jnp.mean(x * x, axis=-1, keepdims=True)
    return x * lax.rsqrt(ms + RMS_EPS) * g


def _head_rms(x, gain_row):
    parts = []
    for h in range(N_HEADS):
        xs = x[:, h * HEAD_DIM:(h + 1) * HEAD_DIM]
        ms = jnp.mean(xs * xs, axis=-1, keepdims=True)
        parts.append(xs * lax.rsqrt(ms + RMS_EPS))
    return jnp.concatenate(parts, axis=-1) * gain_row


def _sigmoid(x):
    return 1.0 / (1.0 + jnp.exp(-x))


def _log_sigmoid(z):
    return jnp.minimum(z, 0.0) - jnp.log1p(jnp.exp(-jnp.abs(z)))


def _const_spec(shape):
    nd = len(shape)
    return pl.BlockSpec(shape, lambda *_: (0,) * nd)


def _in_proj_kernel(x_ref, g_ref, wp_ref, wr_ref, wsb_ref, ws5_ref, wg_ref, qg_ref, kg_ref, *refs):
    head_major_refs = refs[:1] + refs[8:] if len(refs) > 7 else ()
    pool_ref, rwkv_ref, q_ref, k_ref, v_ref, s5_ref, gate_ref = refs[len(refs) > 7:][:7]
    h = _rms(x_ref[...], g_ref[...]).astype(BF16)
    pool_ref[...] = jnp.dot(h, wp_ref[...], preferred_element_type=F32)
    rwkv_ref[...] = jnp.dot(h, wr_ref[...], preferred_element_type=F32)
    sb = jnp.dot(h, wsb_ref[...], preferred_element_type=F32)
    q = _head_rms(sb[:, :D_BR], qg_ref[...])
    k = _head_rms(sb[:, D_BR:2 * D_BR], kg_ref[...])
    v = sb[:, 2 * D_BR:]
    q_ref[...] = q
    k_ref[...] = k
    v_ref[...] = v
    s5_ref[...] = jnp.dot(h, ws5_ref[...], preferred_element_type=F32)
    gate_ref[...] = jnp.dot(h, wg_ref[...], preferred_element_type=F32)
    if head_major_refs:
        tails_ref, qh_ref, kh_ref, vh_ref = head_major_refs
        rows = q.shape[0]
        for hh in range(N_HEADS):
            cols = slice(hh * HEAD_DIM, (hh + 1) * HEAD_DIM)
            q_tail = jnp.broadcast_to(tails_ref[N_HEADS:N_HEADS + 1, :], (rows, HEAD_DIM))
            k_tail = jnp.broadcast_to(tails_ref[hh:hh + 1, :], (rows, HEAD_DIM))
            qh_ref[hh] = jnp.concatenate([q[:, cols] * (LOG2E * HEAD_DIM ** -0.5), q_tail], axis=-1).astype(BF16)
            kh_ref[hh] = jnp.concatenate([k[:, cols], k_tail], axis=-1).astype(BF16)
            vh_ref[hh] = v[:, cols].astype(BF16)


def in_proj(x, g, wp, wr, wsb, ws5, wg, qg, kg, sb_tails=None):
    m = x.shape[0]
    tm = min(m, 256)
    row = lambda n: pl.BlockSpec((tm, n), lambda i: (i, 0))
    widths = (D_BR, RWKV_COLS, D_BR, D_BR, D_BR, D_BR, 4 * D_MODEL)
    in_specs = [row(D_MODEL), _const_spec(g.shape), _const_spec(wp.shape), _const_spec(wr.shape),
                _const_spec(wsb.shape), _const_spec(ws5.shape), _const_spec(wg.shape),
                _const_spec(qg.shape), _const_spec(kg.shape)]
    args = [x, g, wp, wr, wsb, ws5, wg, qg, kg]
    out_specs = [row(n) for n in widths]
    out_shape = [jax.ShapeDtypeStruct((m, n), F32) for n in widths]
    if sb_tails is not None:
        in_specs.append(_const_spec(sb_tails.shape))
        args.append(sb_tails)
        for n in (2 * HEAD_DIM, 2 * HEAD_DIM, HEAD_DIM):
            out_specs.append(pl.BlockSpec((N_HEADS, tm, n), lambda i: (0, i, 0)))
            out_shape.append(jax.ShapeDtypeStruct((N_HEADS, m, n), BF16))
    return pl.pallas_call(
        _in_proj_kernel,
        grid=(m // tm,),
        in_specs=in_specs,
        out_specs=out_specs,
        out_shape=out_shape,
        compiler_params=_params(("arbitrary",)),
        name="in_proj",
    )(*args)


def _pool_kernel(u_ref, buf_ref, w_ref, scale_ref, o_ref, new_ref, ext_ref, *, tt, pos0):
    t = pl.program_id(1)

    @pl.when(t == 0)
    def _():
        ext_ref[0:POOL_HDR, :] = buf_ref[0]

    @pl.when(t > 0)
    def _():
        ext_ref[0:POOL_HDR, :] = ext_ref[tt:tt + POOL_HDR, :]

    u = u_ref[0]
    ext_ref[POOL_HDR:POOL_HDR + tt, :] = u
    lane = lax.broadcasted_iota(jnp.int32, (tt, D_BR), 1)
    pos = pos0 + t * tt + lax.broadcasted_iota(jnp.int32, (tt, D_BR), 0)
    s = u
    win_sum = jnp.zeros((tt, D_BR), F32)
    count = jnp.zeros((tt, D_BR), F32)
    for i in range(1, POOL_WINDOWS[-1]):
        s = s + ext_ref[POOL_HDR - i:POOL_HDR - i + tt, :]
        w = i + 1
        if w in POOL_WINDOWS:
            gi = POOL_WINDOWS.index(w)
            sel = (lane >= gi * HEAD_DIM) & (lane < (gi + 1) * HEAD_DIM)
            win_sum = jnp.where(sel, s, win_sum)
            count = jnp.where(sel, jnp.minimum(pos + 1, w).astype(F32), count)
    pooled = win_sum / count - u
    o_ref[0] = _dot_hi(pooled, w_ref[...]) * scale_ref[...]
    new_ref[0] = ext_ref[tt + 1:tt + POOL_HDR, :]


def pool_mixer(u, buf, w_bd, scale, pos0):
    b, t, _ = u.shape
    tt = min(t, 512)
    kern = functools.partial(_pool_kernel, tt=tt, pos0=pos0)
    return pl.pallas_call(
        kern,
        grid=(b, t // tt),
        in_specs=[pl.BlockSpec((1, tt, D_BR), lambda i, j: (i, j, 0)),
                  pl.BlockSpec((1, POOL_HDR, D_BR), lambda i, j: (i, 0, 0)),
                  _const_spec(w_bd.shape), _const_spec(scale.shape)],
        out_specs=[pl.BlockSpec((1, tt, D_BR), lambda i, j: (i, j, 0)),
                   pl.BlockSpec((1, POOL_BUF, D_BR), lambda i, j: (i, 0, 0))],
        out_shape=[jax.ShapeDtypeStruct((b, t, D_BR), F32),
                   jax.ShapeDtypeStruct((b, POOL_BUF, D_BR), F32)],
        scratch_shapes=[pltpu.VMEM((POOL_HDR + tt, D_BR), F32)],
        compiler_params=_params(("arbitrary", "arbitrary")),
        name="pool_mixer",
    )(u, buf, w_bd, scale)


def _rwkv_pre_kernel(p_ref, shift_ref, mu_ref, w0_ref, wup_ref, a0_ref, aup_ref, gup_ref,
                     kk_ref, ka_ref,
                     r_out, k_out, v_out, ld_out, kk_out, b_out, g_out, li_out, last_ref, *, tt, c):
    t = pl.program_id(1)

    @pl.when(t == 0)
    def _():
        last_ref[...] = shift_ref[0]

    p = p_ref[0]
    rowi = lax.broadcasted_iota(jnp.int32, (tt, RWKV_COLS), 0)
    prev = jnp.where(rowi == 0, last_ref[...], pltpu.roll(p, 1, 0))
    last_ref[...] = p[tt - 1:tt, :]
    pm = p + (prev - p) * mu_ref[...]
    r = pm[:, 0:D_BR]
    k = pm[:, D_BR:2 * D_BR]
    v = pm[:, 2 * D_BR:3 * D_BR]
    w_lo = pm[:, 768:832]
    a_lo = pm[:, 832:896]
    g_lo = pm[:, 896:1024]
    wz = w0_ref[...] + _dot(jnp.tanh(w_lo), wup_ref[...])
    w_log = _log_sigmoid(wz) - 0.5
    a = _sigmoid(a0_ref[...] + _dot(a_lo, aup_ref[...]))
    g = _dot(_sigmoid(g_lo), gup_ref[...])
    kk = k * kk_ref[...]
    parts = []
    for h in range(N_HEADS):
        ks = kk[:, h * HEAD_DIM:(h + 1) * HEAD_DIM]
        ss = jnp.sum(ks * ks, axis=-1, keepdims=True)
        parts.append(ks * lax.rsqrt(jnp.maximum(ss, 1e-24)))
    kk = jnp.concatenate(parts, axis=-1)
    ld = -jnp.exp(w_log)
    rowc = lax.broadcasted_iota(jnp.int32, (tt, D_BR), 0) & (c - 1)
    l_inc = ld
    off = 1
    while off < c:
        l_inc = l_inc + jnp.where(rowc >= off, pltpu.roll(l_inc, off, 0), 0.0)
        off *= 2
    r_out[0] = r
    k_out[0] = k * (1.0 + (a - 1.0) * ka_ref[...])
    v_out[0] = v
    ld_out[0] = ld
    kk_out[0] = kk
    b_out[0] = kk * a
    g_out[0] = g
    li_out[0] = l_inc


def rwkv_pre(p, shift, mu, w0, wup, a0, aup, gup, k_k, k_a):
    b, t, _ = p.shape
    tt = min(t, 512)
    kern = functools.partial(_rwkv_pre_kernel, tt=tt, c=min(t, RWKV_CHUNK))
    blk = pl.BlockSpec((1, tt, D_BR), lambda i, j: (i, j, 0))
    consts = (mu, w0, wup, a0, aup, gup, k_k, k_a)
    return pl.pallas_call(
        kern,
        grid=(b, t // tt),
        in_specs=[pl.BlockSpec((1, tt, RWKV_COLS), lambda i, j: (i, j, 0)),
                  pl.BlockSpec((1, 1, RWKV_COLS), lambda i, j: (i, 0, 0))]
                 + [_const_spec(c.shape) for c in consts],
        out_specs=[blk] * 8,
        out_shape=[jax.ShapeDtypeStruct((b, t, D_BR), F32)] * 8,
        scratch_shapes=[pltpu.VMEM((1, RWKV_COLS), F32)],
        compiler_params=_params(("arbitrary", "arbitrary")),
        name="rwkv_pre",
    )(p, shift, *consts)


_BMM_DIMS = {
    'nn': (((2,), (1,)), ((0,), (0,))),
    'nt': (((2,), (2,)), ((0,), (0,))),
}


def _bmm(a, b, dims):
    return lax.dot_general(a.astype(BF16), b.astype(BF16), _BMM_DIMS[dims], preferred_element_type=F32)


def _rwkv_chunk_kernel(r_ref, k_ref, v_ref, ld_ref, kk_ref, b_ref, g_ref, li_ref, s0_ref,
                       lnw_ref, lnb_ref, rk_ref, o_ref, sout_ref, s_ref, *, tt, c):
    t = pl.program_id(1)

    @pl.when(t == 0)
    def _():
        s_ref[...] = s0_ref[0]

    nch = tt // c
    nb = nch * N_HEADS

    def heads(x):
        return jnp.stack([x[ch * c:(ch + 1) * c, h * HEAD_DIM:(h + 1) * HEAD_DIM]
                          for ch in range(nch) for h in range(N_HEADS)])

    def head_rows(ref):
        return jnp.stack([ref[:, h * HEAD_DIM:(h + 1) * HEAD_DIM] for h in range(N_HEADS)])

    ri = lax.broadcasted_iota(jnp.int32, (nb, c, c), 1)
    ci = lax.broadcasted_iota(jnp.int32, (nb, c, c), 2)
    incl = (ri >= ci).astype(F32)
    strict = (ri > ci).astype(F32)
    eye = (ri == ci).astype(F32)
    n_sq = int(round(math.log2(c))) - 1

    r_all = r_ref[0]
    k_all = k_ref[0]
    v_all = v_ref[0]
    b_all = b_ref[0]
    l_inc = li_ref[0]
    l_tot = jnp.concatenate(
        [jnp.broadcast_to(l_inc[(ch + 1) * c - 1:(ch + 1) * c, :], (c, D_BR)) for ch in range(nch)], axis=0)
    e_neg = jnp.exp(-l_inc)
    e_end = jnp.exp(l_tot - l_inc)
    a_t = heads(-kk_ref[0] * jnp.exp(l_inc - ld_ref[0]))
    r_t = heads(r_all * jnp.exp(l_inc))
    b_t = heads(b_all * e_neg)
    k_t = heads(k_all * e_neg)
    b_e = heads(b_all * e_end)
    k_e = heads(k_all * e_end)
    d_tot = heads(jnp.exp(l_tot))[:, 0:1, :]
    r = heads(r_all)
    k = heads(k_all)
    v = heads(v_all)

    gram = _bmm(jnp.concatenate([a_t, r_t], axis=1), jnp.concatenate([b_t, k_t], axis=1), 'nt')
    m_ab = gram[:, :c, :c] * strict
    m_ak = gram[:, :c, c:] * strict
    p_rb = gram[:, c:, :c] * incl
    p_rk = gram[:, c:, c:] * incl
    pw = m_ab
    tinv = eye + m_ab
    for _ in range(n_sq):
        pw = _bmm(pw, pw, 'nn')
        tinv = tinv + _bmm(tinv, pw, 'nn')
    wu = _bmm(tinv, jnp.concatenate([a_t, _bmm(m_ak, v, 'nn')], axis=2), 'nn')
    o0 = _bmm(p_rk, v, 'nn')
    uv_rhs = jnp.concatenate([b_e, k_e], axis=1)

    lnw = head_rows(lnw_ref)
    lnb = head_rows(lnb_ref)
    rk = head_rows(rk_ref)
    s = s_ref[...]
    for ch in range(nch):
        sl = slice(ch * N_HEADS, (ch + 1) * N_HEADS)
        y = _bmm(jnp.concatenate([wu[sl, :, :HEAD_DIM], r_t[sl]], axis=1), s, 'nt')
        u = y[:, :c] + wu[sl, :, HEAD_DIM:]
        o = y[:, c:] + _bmm(p_rb[sl], u, 'nn') + o0[sl]
        uv_t = jnp.swapaxes(jnp.concatenate([u, v[sl]], axis=1), 1, 2)
        s = s * d_tot[sl] + _bmm(uv_t, uv_rhs[sl], 'nn')
        mean = jnp.mean(o, axis=-1, keepdims=True)
        var = jnp.mean(jnp.square(o - mean), axis=-1, keepdims=True)
        o = (o - mean) * lax.rsqrt(var + RWKV_LN_EPS) * lnw + lnb
        o = o + jnp.sum(r[sl] * k[sl] * rk, axis=-1, keepdims=True) * v[sl]
        rows = slice(ch * c, (ch + 1) * c)
        o_ref[0, rows, :] = (jnp.concatenate([o[h] for h in range(N_HEADS)], axis=-1)
                             * g_ref[0, rows, :])
    s_ref[...] = s
    sout_ref[0] = s


def rwkv_chunk(r, k, v, ld, kk, bv, g, l_inc, s0, ln_w, ln_b, r_k):
    b, t, _ = r.shape
    c = min(t, RWKV_CHUNK)
    tt = min(t, 256)
    kern = functools.partial(_rwkv_chunk_kernel, tt=tt, c=c)
    blk = pl.BlockSpec((1, tt, D_BR), lambda i, j: (i, j, 0))
    sblk = pl.BlockSpec((1, N_HEADS, HEAD_DIM, HEAD_DIM), lambda i, j: (i, 0, 0, 0))
    return pl.pallas_call(
        kern,
        grid=(b, t // tt),
        in_specs=[blk] * 8 + [sblk, _const_spec(ln_w.shape), _const_spec(ln_b.shape),
                              _const_spec(r_k.shape)],
        out_specs=[blk, sblk],
        out_shape=[jax.ShapeDtypeStruct((b, t, D_BR), F32),
                   jax.ShapeDtypeStruct((b, N_HEADS, HEAD_DIM, HEAD_DIM), F32)],
        scratch_shapes=[pltpu.VMEM((N_HEADS, HEAD_DIM, HEAD_DIM), F32)],
        compiler_params=_params(("arbitrary", "arbitrary")),
        name="rwkv_chunk",
    )(r, k, v, ld, kk, bv, g, l_inc, s0, ln_w, ln_b, r_k)


SB_GROUP = 8


LOG2E = 1.4426950408889634


def _softplus2(z2):
    return jnp.maximum(z2, 0.0) + jnp.log2(1.0 + jnp.exp2(-jnp.abs(z2)))


def _sb_prompt_kernel(q_ref, k_ref, v_ref, o_ref, *, tq):
    i = pl.program_id(1)
    q = q_ref[0]
    ri = lax.broadcasted_iota(jnp.int32, (tq, tq), 0)
    ci = lax.broadcasted_iota(jnp.int32, (tq, tq), 1)
    later = (ri > ci).astype(BF16)

    def scores(j):
        start = pl.multiple_of(j * tq, tq)
        return _dot_nt(q, k_ref[0, pl.ds(start, tq), :]), v_ref[0, pl.ds(start, tq), :]

    def full_blocks(j, n, car, acc):
        zv = [scores(j - b) for b in range(n)]
        sps = [_softplus2(z) for z, _ in zv]
        afts = [jnp.dot(sp.astype(BF16), later, preferred_element_type=F32) for sp in sps]
        atts = [jnp.exp2(zv[b][0] - sps[b] - afts[b]).astype(BF16) for b in range(n)]
        for b in range(n):
            acc = acc + jnp.exp2(-car) * jnp.dot(atts[b], zv[b][1], preferred_element_type=F32)
            car = car + jnp.sum(sps[b], axis=-1, keepdims=True)
        return car, acc

    z, vb = scores(i)
    mask = ci < ri
    sp = jnp.where(mask, _softplus2(z), 0.0)
    aft = jnp.dot(sp.astype(BF16), later, preferred_element_type=F32)
    att = jnp.where(mask, jnp.exp2(z - sp - aft), 0.0)
    acc = jnp.dot(att.astype(BF16), vb, preferred_element_type=F32)
    car = jnp.sum(sp, axis=-1, keepdims=True)

    j = i - 1
    g = 1
    while g < SB_GROUP:
        n_g = lax.shift_right_logical(i, g.bit_length() - 1) & 1
        car, acc = lax.fori_loop(
            0, n_g, lambda p, ca, j=j, g=g: full_blocks(j, g, *ca), (car, acc))
        j = j - g * n_g
        g *= 2
    _, acc = lax.fori_loop(
        0, lax.shift_right_logical(i, SB_GROUP.bit_length() - 1),
        lambda p, ca: full_blocks(j - SB_GROUP * p, SB_GROUP, *ca), (car, acc))
    o_ref[0] = acc


def sb_prompt(q, k, v):
    h, t, _ = q.shape
    tq = min(t, 256)
    kern = functools.partial(_sb_prompt_kernel, tq=tq)
    return pl.pallas_call(
        kern,
        grid=(h, t // tq),
        in_specs=[pl.BlockSpec((1, tq, 2 * HEAD_DIM), lambda a, i: (a, i, 0)),
                  pl.BlockSpec((1, t, 2 * HEAD_DIM), lambda a, i: (a, 0, 0)),
                  pl.BlockSpec((1, t, HEAD_DIM), lambda a, i: (a, 0, 0))],
        out_specs=pl.BlockSpec((1, tq, HEAD_DIM), lambda a, i: (a, i, 0)),
        out_shape=jax.ShapeDtypeStruct((h, t, HEAD_DIM), F32),
        compiler_params=_params(("arbitrary", "arbitrary")),
        name="sb_prompt",
    )(q, k, v)


PAGES_PER_STEP = 16


def _sb_paged_kernel(pt_ref, q_ref, bias_ref, kn_ref, vn_ref, *rest, t_new, past_len):
    nk = PAGES_PER_STEP
    k_pages = rest[:nk]
    v_pages = rest[nk:2 * nk]
    o_ref = rest[2 * nk]
    acc_ref, car_ref = rest[2 * nk + 1:]
    s = pl.program_id(1)
    q = q_ref[0] * (LOG2E * HEAD_DIM ** -0.5)
    qh = [q[:, h * HEAD_DIM:(h + 1) * HEAD_DIM].astype(BF16) for h in range(N_HEADS)]
    bias = bias_ref[...]
    ri = lax.broadcasted_iota(jnp.int32, (PAGE_SIZE, PAGE_SIZE), 0)
    ci = lax.broadcasted_iota(jnp.int32, (PAGE_SIZE, PAGE_SIZE), 1)
    later = (ri > ci).astype(BF16)

    def head_rows(x):
        return [x[h * t_new:(h + 1) * t_new] for h in range(N_HEADS)]

    @pl.when(s == 0)
    def _():
        kn = kn_ref[0].astype(BF16)
        vn = vn_ref[0].astype(BF16)
        cols = [slice(h * HEAD_DIM, (h + 1) * HEAD_DIM) for h in range(N_HEADS)]
        z = jnp.concatenate([_dot_nt(qh[h], kn[:, cols[h]]) for h in range(N_HEADS)], axis=0) + bias
        qi = jnp.concatenate([lax.broadcasted_iota(jnp.int32, (t_new, PAGE_SIZE), 0)] * N_HEADS, axis=0)
        mask = lax.broadcasted_iota(jnp.int32, (N_HEADS * t_new, PAGE_SIZE), 1) < qi
        sp = jnp.where(mask, _softplus2(z), 0.0)
        aft = jnp.dot(sp.astype(BF16), later, preferred_element_type=F32)
        att = head_rows(jnp.where(mask, jnp.exp2(z - sp - aft), 0.0).astype(BF16))
        acc_ref[...] = jnp.concatenate(
            [jnp.dot(att[h], vn[:, cols[h]], preferred_element_type=F32) for h in range(N_HEADS)], axis=0)
        car_ref[...] = jnp.sum(sp, axis=-1, keepdims=True)

    def gather_head(pages, h):
        return jnp.concatenate([p[0, 0, h] for p in pages], axis=1).astype(BF16)

    z = jnp.concatenate(
        [jnp.dot(qh[h], gather_head(k_pages, h), preferred_element_type=F32) for h in range(N_HEADS)],
        axis=0) + bias
    sp = _softplus2(z)
    car = car_ref[...]
    afts = []
    for rr in range(nk):
        sp_rr = sp[:, rr * PAGE_SIZE:(rr + 1) * PAGE_SIZE]
        afts.append(jnp.dot(sp_rr.astype(BF16), later, preferred_element_type=F32) + car)
        car = car + jnp.sum(sp_rr, axis=-1, keepdims=True)
    att = head_rows(jnp.exp2(z - sp - jnp.concatenate(afts, axis=1)).astype(BF16))
    acc_ref[...] += jnp.concatenate(
        [_dot_nt(att[h], gather_head(v_pages, h)) for h in range(N_HEADS)], axis=0)
    car_ref[...] = car

    @pl.when(s == pl.num_programs(1) - 1)
    def _():
        o_ref[0] = jnp.concatenate(head_rows(acc_ref[...]), axis=-1)


def sb_paged(q, bias_rows, k_new, v_new, cache_k, cache_v, page_table, layer):
    b, t_new, _ = q.shape
    n_pages = page_table.shape[1]
    past_len = n_pages * PAGE_SIZE
    nk = PAGES_PER_STEP
    n_steps = n_pages // nk
    kern = functools.partial(_sb_paged_kernel, t_new=t_new, past_len=past_len)

    def page_spec(rr):
        return pl.BlockSpec(
            (1, 1, N_HEADS, HEAD_DIM, PAGE_SIZE),
            lambda i, s, pt: (layer, pt[i, n_pages - 1 - (s * nk + rr)], 0, 0, 0))

    n_rows = N_HEADS * t_new
    grid_spec = pltpu.PrefetchScalarGridSpec(
        num_scalar_prefetch=1,
        grid=(b, n_steps),
        in_specs=[pl.BlockSpec((1, t_new, D_BR), lambda i, s, pt: (i, 0, 0)),
                  pl.BlockSpec((n_rows, 1), lambda i, s, pt: (0, 0)),
                  pl.BlockSpec((1, PAGE_SIZE, D_BR), lambda i, s, pt: (i, 0, 0)),
                  pl.BlockSpec((1, PAGE_SIZE, D_BR), lambda i, s, pt: (i, 0, 0))]
                 + [page_spec(rr) for rr in range(nk)] * 2,
        out_specs=pl.BlockSpec((1, t_new, D_BR), lambda i, s, pt: (i, 0, 0)),
        scratch_shapes=[pltpu.VMEM((n_rows, HEAD_DIM), F32), pltpu.VMEM((n_rows, 1), F32)],
    )
    return pl.pallas_call(
        kern,
        grid_spec=grid_spec,
        out_shape=jax.ShapeDtypeStruct((b, t_new, D_BR), F32),
        compiler_params=_params(("arbitrary", "arbitrary")),
        name="sb_paged",
    )(page_table, q, bias_rows, k_new, v_new, *([cache_k] * nk), *([cache_v] * nk))


def _s5_kernel(u_ref, hre_ref, him_ref, bre_ref, bim_ref, cre_ref, cim_ref, d_ref, wglu_ref,
               powr_ref, powi_ref, lvr_ref, lvi_ref,
               o_ref, sre_ref, sim_ref, cr_ref, cim_s_ref, *, tt):
    t = pl.program_id(1)

    @pl.when(t == 0)
    def _():
        cr_ref[...] = hre_ref[0]
        cim_s_ref[...] = him_ref[0]

    u = u_ref[0]
    xr = _dot(u, bre_ref[...])
    xi = _dot(u, bim_ref[...])
    row_in_group = lax.broadcasted_iota(jnp.int32, (tt, S5_N), 0) & 7
    for lvl, off in enumerate((1, 2, 4)):
        lr = lvr_ref[lvl:lvl + 1, :]
        li = lvi_ref[lvl:lvl + 1, :]
        m = row_in_group >= off
        sr = jnp.where(m, pltpu.roll(xr, off, 0), 0.0)
        si = jnp.where(m, pltpu.roll(xi, off, 0), 0.0)
        xr, xi = xr + lr * sr - li * si, xi + lr * si + li * sr
    pr = powr_ref[0:8, :]
    pi = powi_ref[0:8, :]
    hr = cr_ref[...]
    hi = cim_s_ref[...]
    groups_r, groups_i = [], []
    for g in range(tt // 8):
        gr = xr[g * 8:(g + 1) * 8, :]
        gi = xi[g * 8:(g + 1) * 8, :]
        gr, gi = gr + pr * hr - pi * hi, gi + pr * hi + pi * hr
        groups_r.append(gr)
        groups_i.append(gi)
        hr = gr[7:8, :]
        hi = gi[7:8, :]
    xr = jnp.concatenate(groups_r, axis=0)
    xi = jnp.concatenate(groups_i, axis=0)
    cr_ref[...] = xr[tt - 1:tt, :]
    cim_s_ref[...] = xi[tt - 1:tt, :]
    sre_ref[0] = xr[tt - 1:tt, :]
    sim_ref[0] = xi[tt - 1:tt, :]
    y = _dot(xr, cre_ref[...]) - _dot(xi, cim_ref[...]) + d_ref[...] * u
    z = _dot(jax.nn.gelu(y), wglu_ref[...])
    o_ref[0] = z[:, :D_BR] * _sigmoid(z[:, D_BR:])


def s5_mixer(u, h_re, h_im, b_re, b_im, c_re, c_im, d, w_glu, pow_re, pow_im, lv_re, lv_im):
    b, t, _ = u.shape
    tt = min(t, 256)
    kern = functools.partial(_s5_kernel, tt=tt)
    consts = (b_re, b_im, c_re, c_im, d, w_glu, pow_re, pow_im, lv_re, lv_im)
    sblk = pl.BlockSpec((1, 1, S5_N), lambda i, j: (i, 0, 0))
    return pl.pallas_call(
        kern,
        grid=(b, t // tt),
        in_specs=[pl.BlockSpec((1, tt, D_BR), lambda i, j: (i, j, 0)), sblk, sblk]
                 + [_const_spec(c.shape) for c in consts],
        out_specs=[pl.BlockSpec((1, tt, D_BR), lambda i, j: (i, j, 0)), sblk, sblk],
        out_shape=[jax.ShapeDtypeStruct((b, t, D_BR), F32),
                   jax.ShapeDtypeStruct((b, 1, S5_N), F32),
                   jax.ShapeDtypeStruct((b, 1, S5_N), F32)],
        scratch_shapes=[pltpu.VMEM((1, S5_N), F32), pltpu.VMEM((1, S5_N), F32)],
        compiler_params=_params(("arbitrary", "arbitrary")),
        name="s5_mixer",
    )(u, h_re, h_im, *consts)


def _merge_kernel(x_ref, bp_ref, br_ref, bs_ref, b5_ref, gate_ref, wb_ref, wo_ref, o_ref):
    if len(bs_ref.shape) == 3:
        o_sb = jnp.concatenate([bs_ref[h] for h in range(N_HEADS)], axis=-1)
    else:
        o_sb = bs_ref[...]
    merged = None
    for n, br in enumerate((bp_ref[...], br_ref[...], o_sb, b5_ref[...])):
        lifted = _dot(br, wb_ref[n])
        term = _sigmoid(gate_ref[:, n * D_MODEL:(n + 1) * D_MODEL]) * lifted
        merged = term if merged is None else merged + term
    o_ref[...] = x_ref[...] + _dot(merged, wo_ref[...])


def merge(x, o_pool, o_rwkv, o_sb, o_s5, gates, w_branch, w_out):
    m = x.shape[0]
    tm = min(m, 256)
    row = lambda n: pl.BlockSpec((tm, n), lambda i: (i, 0))
    sb_spec = row(D_BR) if o_sb.ndim == 2 else pl.BlockSpec((N_HEADS, tm, HEAD_DIM), lambda i: (0, i, 0))
    return pl.pallas_call(
        _merge_kernel,
        grid=(m // tm,),
        in_specs=[row(D_MODEL), row(D_BR), row(D_BR), sb_spec, row(D_BR), row(4 * D_MODEL),
                  _const_spec(w_branch.shape), _const_spec(w_out.shape)],
        out_specs=row(D_MODEL),
        out_shape=jax.ShapeDtypeStruct((m, D_MODEL), F32),
        compiler_params=_params(("arbitrary",)),
        name="merge",
    )(x, o_pool, o_rwkv, o_sb, o_s5, gates, w_branch, w_out)


def _xattn_kernel(x_ref, mk_ref, mv_ref, g_ref, wq_ref, qg_ref, wo_ref, o_ref):
    x = x_ref[0]
    hn = _rms(x, g_ref[...])
    q = _head_rms(_dot(hn, wq_ref[...]), qg_ref[...]) * HEAD_DIM ** -0.5
    outs = []
    for h in range(N_HEADS):
        cols = slice(h * HEAD_DIM, (h + 1) * HEAD_DIM)
        s = _dot(q[:, cols], mk_ref[0, 0, h])
        s = s - jnp.max(s, axis=-1, keepdims=True)
        e = jnp.exp(s)
        pr = e / jnp.sum(e, axis=-1, keepdims=True)
        outs.append(_dot_nt(pr.astype(BF16), mv_ref[0, 0, h].astype(BF16)))
    o = jnp.concatenate(outs, axis=-1)
    o_ref[0] = x + _dot(o, wo_ref[...])


def cross_attention(x, mem_k, mem_v, layer, g, wq, qg, wo):
    b, t, _ = x.shape
    tt = min(t, 512)
    mem_spec = pl.BlockSpec((1, 1, N_HEADS, HEAD_DIM, N_MEM), lambda i, j: (layer, i, 0, 0, 0))
    return pl.pallas_call(
        _xattn_kernel,
        grid=(b, t // tt),
        in_specs=[pl.BlockSpec((1, tt, D_MODEL), lambda i, j: (i, j, 0)), mem_spec, mem_spec,
                  _const_spec(g.shape), _const_spec(wq.shape), _const_spec(qg.shape),
                  _const_spec(wo.shape)],
        out_specs=pl.BlockSpec((1, tt, D_MODEL), lambda i, j: (i, j, 0)),
        out_shape=jax.ShapeDtypeStruct((b, t, D_MODEL), F32),
        compiler_params=_params(("arbitrary", "arbitrary")),
        name="cross_attention",
    )(x, mem_k, mem_v, g, wq, qg, wo)


def _memkv_kernel(mem_ref, g_ref, wk_ref, wv_ref, kg_ref, k_ref, v_ref, kt_ref, vt_ref):
    mn = _rms(mem_ref[...], g_ref[...])
    k = _head_rms(_dot(mn, wk_ref[...]), kg_ref[...])
    v = _dot(mn, wv_ref[...])
    k_ref[...] = k
    v_ref[...] = v
    kt_ref[...] = k.T
    vt_ref[...] = v.T


def memory_kv(mem, g, wk, wv, kg):
    m = mem.shape[0]
    args = (mem, g, wk, wv, kg)
    return pl.pallas_call(
        _memkv_kernel,
        grid=(1,),
        in_specs=[_const_spec(a.shape) for a in args],
        out_specs=[_const_spec((m, D_BR))] * 2 + [_const_spec((D_BR, m))] * 2,
        out_shape=[jax.ShapeDtypeStruct((m, D_BR), F32)] * 2 + [jax.ShapeDtypeStruct((D_BR, m), F32)] * 2,
        compiler_params=_params(("arbitrary",)),
        name="memory_kv",
    )(*args)


def _ffn_kernel(x_ref, g_ref, wg_ref, wu_ref, wd_ref, o_ref):
    x = x_ref[...]
    hn = _rms(x, g_ref[...]).astype(BF16)
    a = jnp.dot(hn, wg_ref[...], preferred_element_type=F32)
    bq = jnp.dot(hn, wu_ref[...], preferred_element_type=F32)
    act = a * _sigmoid(a) * bq
    o_ref[...] = x + _dot(act, wd_ref[...])


def ffn(x, g, w_gate, w_up, w_down):
    m = x.shape[0]
    tm = min(m, 256)
    row = pl.BlockSpec((tm, D_MODEL), lambda i: (i, 0))
    return pl.pallas_call(
        _ffn_kernel,
        grid=(m // tm,),
        in_specs=[row, _const_spec(g.shape), _const_spec(w_gate.shape), _const_spec(w_up.shape),
                  _const_spec(w_down.shape)],
        out_specs=row,
        out_shape=jax.ShapeDtypeStruct((m, D_MODEL), F32),
        compiler_params=_params(("arbitrary",)),
        name="ffn",
    )(x, g, w_gate, w_up, w_down)


def _block_diag(blocks):
    g, m, n = blocks.shape
    eye = jnp.eye(g, dtype=blocks.dtype)
    return (eye[:, None, :, None] * blocks[:, :, None, :]).reshape(g * m, g * n)


def _tile_heads(v):
    return jnp.tile(v, N_HEADS).reshape(1, N_HEADS * v.shape[0])


def _cmul(ar, ai, br, bi):
    return ar * br - ai * bi, ar * bi + ai * br


def _s5_constants(a_re, a_im, log_dt, b_re, b_im, c_re, c_im, tt):
    dt_g = jnp.exp(log_dt)[:, None]
    mag = jnp.exp(dt_g * a_re)
    ab_re, ab_im = mag * jnp.cos(dt_g * a_im), mag * jnp.sin(dt_g * a_im)
    den = a_re * a_re + a_im * a_im
    n_re = ab_re - 1.0
    f_re = (n_re * a_re + ab_im * a_im) / den
    f_im = (ab_im * a_re - n_re * a_im) / den
    bb_re = f_re[..., None] * b_re - f_im[..., None] * b_im
    bb_im = f_re[..., None] * b_im + f_im[..., None] * b_re
    bmat_re = _block_diag(jnp.swapaxes(bb_re, 1, 2))
    bmat_im = _block_diag(jnp.swapaxes(bb_im, 1, 2))
    cmat_re = _block_diag(jnp.swapaxes(c_re, 1, 2))
    cmat_im = _block_diag(jnp.swapaxes(c_im, 1, 2))
    lam_re = ab_re.reshape(1, S5_N)
    lam_im = ab_im.reshape(1, S5_N)
    pow_re, pow_im = lam_re, lam_im
    lv_re, lv_im = [lam_re], [lam_im]
    cur_re, cur_im = lam_re, lam_im
    n = 1
    while n < tt:
        nr, ni = _cmul(pow_re, pow_im, cur_re, cur_im)
        pow_re = jnp.concatenate([pow_re, nr], axis=0)
        pow_im = jnp.concatenate([pow_im, ni], axis=0)
        cur_re, cur_im = _cmul(cur_re, cur_im, cur_re, cur_im)
        lv_re.append(cur_re)
        lv_im.append(cur_im)
        n *= 2
    n_lv = max(len(lv_re) - 1, 1)
    lv_re = jnp.concatenate(lv_re[:n_lv], axis=0)
    lv_im = jnp.concatenate(lv_im[:n_lv], axis=0)
    return (bmat_re.astype(BF16), bmat_im.astype(BF16), cmat_re.astype(BF16), cmat_im.astype(BF16),
            pow_re, pow_im, lv_re, lv_im)


def _layer_weights(l, w, tt_prompt, tt_sample):
    w_in = w['w_in'][l]
    lw = {
        'norm_mix': w['norm_mix'][l].reshape(1, D_MODEL),
        'norm_cross': w['norm_cross'][l].reshape(1, D_MODEL),
        'norm_mem': w['norm_mem'][l].reshape(1, D_MODEL),
        'norm_ffn': w['norm_ffn'][l].reshape(1, D_MODEL),
        'w_pool_in': w_in[:, :OFF_RWKV].astype(BF16),
        'w_rwkv_in': w_in[:, OFF_RWKV:OFF_SB].astype(BF16),
        'w_sb_in': w_in[:, OFF_SB:OFF_S5].astype(BF16),
        'w_s5_in': w_in[:, OFF_S5:OFF_GATE].astype(BF16),
        'w_gate_in': w_in[:, OFF_GATE:].astype(BF16),
        'sb_q_norm': _tile_heads(w['sb_q_norm'][l]),
        'sb_k_norm': _tile_heads(w['sb_k_norm'][l]),
        'sb_bias': w['sb_bias'][l],
        'pool_w': _block_diag(w['pool_w'][l]),
        'pool_scale': w['pool_scale'][l].reshape(1, D_BR),
        'rwkv_mu': w['rwkv_mu'][l].reshape(1, RWKV_COLS),
        'rwkv_w0': w['rwkv_w0'][l].reshape(1, D_BR),
        'rwkv_w_up': w['rwkv_w_up'][l].astype(BF16),
        'rwkv_a0': w['rwkv_a0'][l].reshape(1, D_BR),
        'rwkv_a_up': w['rwkv_a_up'][l].astype(BF16),
        'rwkv_g_up': w['rwkv_g_up'][l].astype(BF16),
        'rwkv_k_k': w['rwkv_k_k'][l].reshape(1, D_BR),
        'rwkv_k_a': w['rwkv_k_a'][l].reshape(1, D_BR),
        'rwkv_r_k': w['rwkv_r_k'][l].reshape(1, D_BR),
        'rwkv_ln_w': w['rwkv_ln_w'][l].reshape(1, D_BR),
        'rwkv_ln_b': w['rwkv_ln_b'][l].reshape(1, D_BR),
        's5_d': w['s5_d'][l].reshape(1, D_BR),
        's5_w_glu': w['s5_w_glu'][l].astype(BF16),
        'w_branch': w['w_branch'][l].astype(BF16),
        'w_out': w['w_out'][l].astype(BF16),
        'xa_w_q': w['xa_w_q'][l].astype(BF16),
        'xa_w_k': w['xa_w_k'][l].astype(BF16),
        'xa_w_v': w['xa_w_v'][l].astype(BF16),
        'xa_q_norm': _tile_heads(w['xa_q_norm'][l]),
        'xa_k_norm': _tile_heads(w['xa_k_norm'][l]),
        'xa_w_o': w['xa_w_o'][l].astype(BF16),
        'ffn_w_gate': w['ffn_w_gate'][l].astype(BF16),
        'ffn_w_up': w['ffn_w_up'][l].astype(BF16),
        'ffn_w_down': w['ffn_w_down'][l].astype(BF16),
    }
    b2 = w['sb_bias'][l] * LOG2E
    p1 = b2.astype(BF16).astype(F32)
    p2 = (b2 - p1).astype(BF16).astype(F32)
    p3 = (b2 - p1 - p2).astype(BF16).astype(F32)
    k_tails = jnp.pad(jnp.stack([p1, p2, p3], axis=1), ((0, 0), (0, HEAD_DIM - 3)))
    q_tail = jnp.pad(jnp.ones((1, 3), F32), ((0, 0), (0, HEAD_DIM - 3)))
    lw['sb_tails'] = jnp.concatenate([k_tails, q_tail], axis=0)
    s5_args = (w['s5_a_re'][l], w['s5_a_im'][l], w['s5_log_dt'][l], w['s5_b_re'][l], w['s5_b_im'][l],
               w['s5_c_re'][l], w['s5_c_im'][l])
    lw['s5_prompt'] = lw['s5_sample'] = _s5_constants(*s5_args, 8)
    return lw


def _trunk_layer(x, pos0, mem_k, mem_v, mem_layer, sb_fn, pool_buf, shift, wkv, s5_re, s5_im, lw, s5c,
                 fresh):
    b, t, _ = x.shape
    m = b * t
    xf = x.reshape(m, D_MODEL)
    u_pool, p_rwkv, q, k, v, u_s5, gates, *head_major = in_proj(
        xf, lw['norm_mix'], lw['w_pool_in'], lw['w_rwkv_in'], lw['w_sb_in'], lw['w_s5_in'],
        lw['w_gate_in'], lw['sb_q_norm'], lw['sb_k_norm'], lw['sb_tails'] if fresh else None)
    o_pool, new_pool = pool_mixer(u_pool.reshape(b, t, D_BR), pool_buf, lw['pool_w'],
                                  lw['pool_scale'], pos0)
    p3 = p_rwkv.reshape(b, t, RWKV_COLS)
    pre = rwkv_pre(p3, shift, lw['rwkv_mu'], lw['rwkv_w0'], lw['rwkv_w_up'], lw['rwkv_a0'],
                   lw['rwkv_a_up'], lw['rwkv_g_up'], lw['rwkv_k_k'], lw['rwkv_k_a'])
    o_rwkv, new_wkv = rwkv_chunk(*pre, wkv, lw['rwkv_ln_w'], lw['rwkv_ln_b'], lw['rwkv_r_k'])
    new_shift = p3[:, -1]
    q3, k3, v3 = (z.reshape(b, t, D_BR) for z in (q, k, v))
    o_sb = sb_fn(*head_major) if fresh else sb_fn(q3, k3, v3).reshape(m, D_BR)
    o_s5, new_re, new_im = s5_mixer(u_s5.reshape(b, t, D_BR), s5_re, s5_im, s5c[0], s5c[1], s5c[2],
                                    s5c[3], lw['s5_d'], lw['s5_w_glu'], *s5c[4:])
    x1 = merge(xf, o_pool.reshape(m, D_BR), o_rwkv.reshape(m, D_BR), o_sb,
               o_s5.reshape(m, D_BR), gates, lw['w_branch'], lw['w_out'])
    x2 = cross_attention(x1.reshape(b, t, D_MODEL), mem_k, mem_v, mem_layer, lw['norm_cross'],
                         lw['xa_w_q'], lw['xa_q_norm'], lw['xa_w_o'])
    x3 = ffn(x2.reshape(m, D_MODEL), lw['norm_ffn'], lw['ffn_w_gate'], lw['ffn_w_up'],
             lw['ffn_w_down'])
    return x3.reshape(b, t, D_MODEL), (k3, v3, new_pool, new_shift, new_wkv, new_re, new_im)


def kernel(x_prompt, x_sample, cache_sb_k, cache_sb_v, cache_mem_k, cache_mem_v, state_pool, state_rwkv_shift, state_rwkv_wkv, state_s5_re, state_s5_im, page_table, mem_prompt, norm_mix, norm_cross, norm_mem, norm_ffn, w_in, pool_w, pool_scale, rwkv_mu, rwkv_w0, rwkv_w_up, rwkv_a0, rwkv_a_up, rwkv_g_up, rwkv_k_k, rwkv_k_a, rwkv_r_k, rwkv_ln_w, rwkv_ln_b, sb_q_norm, sb_k_norm, sb_bias, s5_a_re, s5_a_im, s5_log_dt, s5_b_re, s5_b_im, s5_c_re, s5_c_im, s5_d, s5_w_glu, w_branch, w_out, xa_w_q, xa_w_k, xa_w_v, xa_q_norm, xa_k_norm, xa_w_o, ffn_w_gate, ffn_w_up, ffn_w_down):
    weights = dict(
        norm_mix=norm_mix, norm_cross=norm_cross, norm_mem=norm_mem, norm_ffn=norm_ffn, w_in=w_in,
        pool_w=pool_w, pool_scale=pool_scale, rwkv_mu=rwkv_mu, rwkv_w0=rwkv_w0, rwkv_w_up=rwkv_w_up,
        rwkv_a0=rwkv_a0, rwkv_a_up=rwkv_a_up, rwkv_g_up=rwkv_g_up, rwkv_k_k=rwkv_k_k,
        rwkv_k_a=rwkv_k_a, rwkv_r_k=rwkv_r_k, rwkv_ln_w=rwkv_ln_w, rwkv_ln_b=rwkv_ln_b,
        sb_q_norm=sb_q_norm, sb_k_norm=sb_k_norm, sb_bias=sb_bias, s5_a_re=s5_a_re, s5_a_im=s5_a_im,
        s5_log_dt=s5_log_dt, s5_b_re=s5_b_re, s5_b_im=s5_b_im, s5_c_re=s5_c_re, s5_c_im=s5_c_im,
        s5_d=s5_d, s5_w_glu=s5_w_glu, w_branch=w_branch, w_out=w_out, xa_w_q=xa_w_q, xa_w_k=xa_w_k,
        xa_w_v=xa_w_v, xa_q_norm=xa_q_norm, xa_k_norm=xa_k_norm, xa_w_o=xa_w_o,
        ffn_w_gate=ffn_w_gate, ffn_w_up=ffn_w_up, ffn_w_down=ffn_w_down)
    depth = w_in.shape[0]
    bp, tp, _ = x_prompt.shape
    bs, ts, _ = x_sample.shape
    n_pool = cache_sb_k.shape[1]
    cache_k = jnp.transpose(cache_sb_k, (0, 1, 3, 4, 2))
    cache_v = jnp.transpose(cache_sb_v, (0, 1, 3, 4, 2))
    mem_k_t = jnp.transpose(cache_mem_k, (0, 1, 3, 4, 2))
    mem_v_t = jnp.transpose(cache_mem_v, (0, 1, 3, 4, 2))
    past_len = page_table.shape[1] * PAGE_SIZE
    tt_s5_p = min(tp, 256)
    tt_s5_s = ts

    xp, xs = x_prompt, x_sample
    outs_p = [[] for _ in range(9)]
    outs_s = [[] for _ in range(7)]
    for l in range(depth):
        lw = _layer_weights(l, weights, tt_s5_p, tt_s5_s)
        mk_p, mv_p, mkt_p, mvt_p = memory_kv(mem_prompt.reshape(bp * N_MEM, D_MODEL), lw['norm_mem'],
                                             lw['xa_w_k'], lw['xa_w_v'], lw['xa_k_norm'])
        mem_t_shape = (1, bp, N_HEADS, HEAD_DIM, N_MEM)

        xp, st = _trunk_layer(
            xp, 0, mkt_p.reshape(mem_t_shape), mvt_p.reshape(mem_t_shape), 0, sb_prompt,
            jnp.zeros((bp, POOL_HDR, D_BR), F32), jnp.zeros((bp, 1, RWKV_COLS), F32),
            jnp.zeros((bp, N_HEADS, HEAD_DIM, HEAD_DIM), F32), jnp.zeros((bp, 1, S5_N), F32),
            jnp.zeros((bp, 1, S5_N), F32), lw, lw['s5_prompt'], True)
        for lst, val in zip(outs_p, (st[0], st[1], mk_p, mv_p) + st[2:]):
            lst.append(val)

        bias_rows = jnp.repeat(lw['sb_bias'] * LOG2E, ts).reshape(N_HEADS * ts, 1)

        def sb_sample_fn(q, k, v, l=l, bias_rows=bias_rows):
            pad = ((0, 0), (0, PAGE_SIZE - ts), (0, 0))
            return sb_paged(q, bias_rows, jnp.pad(k, pad), jnp.pad(v, pad), cache_k, cache_v,
                            page_table, l)

        pool_buf = jnp.pad(state_pool[l], ((0, 0), (1, 0), (0, 0)))
        xs, st = _trunk_layer(
            xs, past_len, mem_k_t, mem_v_t, l, sb_sample_fn,
            pool_buf, state_rwkv_shift[l].reshape(bs, 1, RWKV_COLS), state_rwkv_wkv[l],
            state_s5_re[l].reshape(bs, 1, S5_N), state_s5_im[l].reshape(bs, 1, S5_N),
            lw, lw['s5_sample'], False)
        for lst, val in zip(outs_s, st):
            lst.append(val)

    def stack(lst, shape):
        return jnp.stack(lst, 0).reshape((depth,) + shape)

    kv_p = (bp, tp, N_HEADS, HEAD_DIM)
    kv_s = (bs, ts, N_HEADS, HEAD_DIM)
    mem_shape = (bp, N_MEM, N_HEADS, HEAD_DIM)
    return (xp, xs,
            stack(outs_p[0], kv_p), stack(outs_p[1], kv_p),
            stack(outs_p[2], mem_shape), stack(outs_p[3], mem_shape),
            stack(outs_p[4], (bp, POOL_BUF, D_BR)), stack(outs_p[5], (bp, RWKV_COLS)),
            stack(outs_p[6], (bp, N_HEADS, HEAD_DIM, HEAD_DIM)),
            stack(outs_p[7], (bp, S5_GROUPS, S5_STATE)), stack(outs_p[8], (bp, S5_GROUPS, S5_STATE)),
            stack(outs_s[0], kv_s), stack(outs_s[1], kv_s),
            stack(outs_s[2], (bs, POOL_BUF, D_BR)), stack(outs_s[3], (bs, RWKV_COLS)),
            stack(outs_s[4], (bs, N_HEADS, HEAD_DIM, HEAD_DIM)),
            stack(outs_s[5], (bs, S5_GROUPS, S5_STATE)), stack(outs_s[6], (bs, S5_GROUPS, S5_STATE)))
```

```python
import functools
import math

import jax
import jax.numpy as jnp
from jax import lax
from jax.experimental import pallas as pl
from jax.experimental.pallas import tpu as pltpu

F32 = jnp.float32
BF16 = jnp.bfloat16
HIGHEST = lax.Precision.HIGHEST

D_MODEL = 1024
D_BR = 256
HEAD_DIM = 64
N_HEADS = 4
POOL_WINDOWS = (2, 4, 8, 16)
POOL_BUF = 15
POOL_HDR = 16
RWKV_COLS = 1024
RWKV_LN_EPS = 64e-5
S5_GROUPS = 16
S5_STATE = 64
S5_N = S5_GROUPS * S5_STATE
RWKV_CHUNK = 64
PAGE_SIZE = 128
N_MEM = 256
D_FF = 2816
RMS_EPS = 1e-6
OFF_RWKV = 256
OFF_SB = 1280
OFF_S5 = 2048
OFF_GATE = 2304

VMEM_LIMIT = 56 * 1024 * 1024


def _params(sem):
    return pltpu.CompilerParams(dimension_semantics=sem, vmem_limit_bytes=VMEM_LIMIT)


def _dot(a, b):
    return jnp.dot(a.astype(BF16), b.astype(BF16), preferred_element_type=F32)


def _dot_hi(a, b):
    return jnp.dot(a, b, precision=HIGHEST, preferred_element_type=F32)


def _dot_nt(a, b, precision=None):
    return lax.dot_general(a, b, (((1,), (1,)), ((), ())), precision=precision,
                           preferred_element_type=F32)


def _dot_tn(a, b, precision=None):
    return lax.dot_general(a, b, (((0,), (0,)), ((), ())), precision=precision,
                           preferred_element_type=F32)


def _rms(x, g):
    ms = jnp.mean(x * x, axis=-1, keepdims=True)
    return x * lax.rsqrt(ms + RMS_EPS) * g


def _head_rms(x, gain_row):
    parts = []
    for h in range(N_HEADS):
        xs = x[:, h * HEAD_DIM:(h + 1) * HEAD_DIM]
        ms = jnp.mean(xs * xs, axis=-1, keepdims=True)
        parts.append(xs * lax.rsqrt(ms + RMS_EPS))
    return jnp.concatenate(parts, axis=-1) * gain_row


def _sigmoid(x):
    return 1.0 / (1.0 + jnp.exp(-x))


def _log_sigmoid(z):
    return jnp.minimum(z, 0.0) - jnp.log1p(jnp.exp(-jnp.abs(z)))


def _const_spec(shape):
    nd = len(shape)
    return pl.BlockSpec(shape, lambda *_: (0,) * nd)


def _in_proj_kernel(x_ref, g_ref, wp_ref, wr_ref, wsb_ref, ws5_ref, wg_ref, qg_ref, kg_ref, *refs):
    head_major_refs = refs[:1] + refs[8:] if len(refs) > 7 else ()
    pool_ref, rwkv_ref, q_ref, k_ref, v_ref, s5_ref, gate_ref = refs[len(refs) > 7:][:7]
    h = _rms(x_ref[...], g_ref[...]).astype(BF16)
    pool_ref[...] = jnp.dot(h, wp_ref[...], preferred_element_type=F32)
    rwkv_ref[...] = jnp.dot(h, wr_ref[...], preferred_element_type=F32)
    sb = jnp.dot(h, wsb_ref[...], preferred_element_type=F32)
    q = _head_rms(sb[:, :D_BR], qg_ref[...])
    k = _head_rms(sb[:, D_BR:2 * D_BR], kg_ref[...])
    v = sb[:, 2 * D_BR:]
    q_ref[...] = q
    k_ref[...] = k
    v_ref[...] = v
    s5_ref[...] = jnp.dot(h, ws5_ref[...], preferred_element_type=F32)
    gate_ref[...] = jnp.dot(h, wg_ref[...], preferred_element_type=F32)
    if head_major_refs:
        tails_ref, qh_ref, kh_ref, vh_ref = head_major_refs
        rows = q.shape[0]
        for hh in range(N_HEADS):
            cols = slice(hh * HEAD_DIM, (hh + 1) * HEAD_DIM)
            q_tail = jnp.broadcast_to(tails_ref[N_HEADS:N_HEADS + 1, :], (rows, HEAD_DIM))
            k_tail = jnp.broadcast_to(tails_ref[hh:hh + 1, :], (rows, HEAD_DIM))
            qh_ref[hh] = jnp.concatenate([q[:, cols] * (LOG2E * HEAD_DIM ** -0.5), q_tail], axis=-1).astype(BF16)
            kh_ref[hh] = jnp.concatenate([k[:, cols], k_tail], axis=-1).astype(BF16)
            vh_ref[hh] = v[:, cols].astype(BF16)


def in_proj(x, g, wp, wr, wsb, ws5, wg, qg, kg, sb_tails=None):
    m = x.shape[0]
    tm = min(m, 256)
    row = lambda n: pl.BlockSpec((tm, n), lambda i: (i, 0))
    widths = (D_BR, RWKV_COLS, D_BR, D_BR, D_BR, D_BR, 4 * D_MODEL)
    in_specs = [row(D_MODEL), _const_spec(g.shape), _const_spec(wp.shape), _const_spec(wr.shape),
                _const_spec(wsb.shape), _const_spec(ws5.shape), _const_spec(wg.shape),
                _const_spec(qg.shape), _const_spec(kg.shape)]
    args = [x, g, wp, wr, wsb, ws5, wg, qg, kg]
    out_specs = [row(n) for n in widths]
    out_shape = [jax.ShapeDtypeStruct((m, n), F32) for n in widths]
    if sb_tails is not None:
        in_specs.append(_const_spec(sb_tails.shape))
        args.append(sb_tails)
        for n in (2 * HEAD_DIM, 2 * HEAD_DIM, HEAD_DIM):
            out_specs.append(pl.BlockSpec((N_HEADS, tm, n), lambda i: (0, i, 0)))
            out_shape.append(jax.ShapeDtypeStruct((N_HEADS, m, n), BF16))
    return pl.pallas_call(
        _in_proj_kernel,
        grid=(m // tm,),
        in_specs=in_specs,
        out_specs=out_specs,
        out_shape=out_shape,
        compiler_params=_params(("arbitrary",)),
        name="in_proj",
    )(*args)


def _pool_kernel(u_ref, buf_ref, w_ref, scale_ref, o_ref, new_ref, ext_ref, *, tt, pos0):
    t = pl.program_id(1)

    @pl.when(t == 0)
    def _():
        ext_ref[0:POOL_HDR, :] = buf_ref[0]

    @pl.when(t > 0)
    def _():
        ext_ref[0:POOL_HDR, :] = ext_ref[tt:tt + POOL_HDR, :]

    u = u_ref[0]
    ext_ref[POOL_HDR:POOL_HDR + tt, :] = u
    lane = lax.broadcasted_iota(jnp.int32, (tt, D_BR), 1)
    pos = pos0 + t * tt + lax.broadcasted_iota(jnp.int32, (tt, D_BR), 0)
    s = u
    win_sum = jnp.zeros((tt, D_BR), F32)
    count = jnp.zeros((tt, D_BR), F32)
    for i in range(1, POOL_WINDOWS[-1]):
        s = s + ext_ref[POOL_HDR - i:POOL_HDR - i + tt, :]
        w = i + 1
        if w in POOL_WINDOWS:
            gi = POOL_WINDOWS.index(w)
            sel = (lane >= gi * HEAD_DIM) & (lane < (gi + 1) * HEAD_DIM)
            win_sum = jnp.where(sel, s, win_sum)
            count = jnp.where(sel, jnp.minimum(pos + 1, w).astype(F32), count)
    pooled = win_sum / count - u
    o_ref[0] = _dot_hi(pooled, w_ref[...]) * scale_ref[...]
    new_ref[0] = ext_ref[tt + 1:tt + POOL_HDR, :]


def pool_mixer(u, buf, w_bd, scale, pos0):
    b, t, _ = u.shape
    tt = min(t, 512)
    kern = functools.partial(_pool_kernel, tt=tt, pos0=pos0)
    return pl.pallas_call(
        kern,
        grid=(b, t // tt),
        in_specs=[pl.BlockSpec((1, tt, D_BR), lambda i, j: (i, j, 0)),
                  pl.BlockSpec((1, POOL_HDR, D_BR), lambda i, j: (i, 0, 0)),
                  _const_spec(w_bd.shape), _const_spec(scale.shape)],
        out_specs=[pl.BlockSpec((1, tt, D_BR), lambda i, j: (i, j, 0)),
                   pl.BlockSpec((1, POOL_BUF, D_BR), lambda i, j: (i, 0, 0))],
        out_shape=[jax.ShapeDtypeStruct((b, t, D_BR), F32),
                   jax.ShapeDtypeStruct((b, POOL_BUF, D_BR), F32)],
        scratch_shapes=[pltpu.VMEM((POOL_HDR + tt, D_BR), F32)],
        compiler_params=_params(("arbitrary", "arbitrary")),
        name="pool_mixer",
    )(u, buf, w_bd, scale)


def _rwkv_pre_kernel(p_ref, shift_ref, mu_ref, w0_ref, wup_ref, a0_ref, aup_ref, gup_ref,
                     kk_ref, ka_ref,
                     r_out, k_out, v_out, ld_out, kk_out, b_out, g_out, li_out, last_ref, *, tt, c):
    t = pl.program_id(1)

    @pl.when(t == 0)
    def _():
        last_ref[...] = shift_ref[0]

    p = p_ref[0]
    rowi = lax.broadcasted_iota(jnp.int32, (tt, RWKV_COLS), 0)
    prev = jnp.where(rowi == 0, last_ref[...], pltpu.roll(p, 1, 0))
    last_ref[...] = p[tt - 1:tt, :]
    pm = p + (prev - p) * mu_ref[...]
    r = pm[:, 0:D_BR]
    k = pm[:, D_BR:2 * D_BR]
    v = pm[:, 2 * D_BR:3 * D_BR]
    w_lo = pm[:, 768:832]
    a_lo = pm[:, 832:896]
    g_lo = pm[:, 896:1024]
    wz = w0_ref[...] + _dot(jnp.tanh(w_lo), wup_ref[...])
    w_log = _log_sigmoid(wz) - 0.5
    a = _sigmoid(a0_ref[...] + _dot(a_lo, aup_ref[...]))
    g = _dot(_sigmoid(g_lo), gup_ref[...])
    kk = k * kk_ref[...]
    parts = []
    for h in range(N_HEADS):
        ks = kk[:, h * HEAD_DIM:(h + 1) * HEAD_DIM]
        ss = jnp.sum(ks * ks, axis=-1, keepdims=True)
        parts.append(ks * lax.rsqrt(jnp.maximum(ss, 1e-24)))
    kk = jnp.concatenate(parts, axis=-1)
    ld = -jnp.exp(w_log)
    rowc = lax.broadcasted_iota(jnp.int32, (tt, D_BR), 0) & (c - 1)
    l_inc = ld
    off = 1
    while off < c:
        l_inc = l_inc + jnp.where(rowc >= off, pltpu.roll(l_inc, off, 0), 0.0)
        off *= 2
    r_out[0] = r
    k_out[0] = k * (1.0 + (a - 1.0) * ka_ref[...])
    v_out[0] = v
    ld_out[0] = ld
    kk_out[0] = kk
    b_out[0] = kk * a
    g_out[0] = g
    li_out[0] = l_inc


def rwkv_pre(p, shift, mu, w0, wup, a0, aup, gup, k_k, k_a):
    b, t, _ = p.shape
    tt = min(t, 512)
    kern = functools.partial(_rwkv_pre_kernel, tt=tt, c=min(t, RWKV_CHUNK))
    blk = pl.BlockSpec((1, tt, D_BR), lambda i, j: (i, j, 0))
    consts = (mu, w0, wup, a0, aup, gup, k_k, k_a)
    return pl.pallas_call(
        kern,
        grid=(b, t // tt),
        in_specs=[pl.BlockSpec((1, tt, RWKV_COLS), lambda i, j: (i, j, 0)),
                  pl.BlockSpec((1, 1, RWKV_COLS), lambda i, j: (i, 0, 0))]
                 + [_const_spec(c.shape) for c in consts],
        out_specs=[blk] * 8,
        out_shape=[jax.ShapeDtypeStruct((b, t, D_BR), F32)] * 8,
        scratch_shapes=[pltpu.VMEM((1, RWKV_COLS), F32)],
        compiler_params=_params(("arbitrary", "arbitrary")),
        name="rwkv_pre",
    )(p, shift, *consts)


_BMM_DIMS = {
    'nn': (((2,), (1,)), ((0,), (0,))),
    'nt': (((2,), (2,)), ((0,), (0,))),
}


def _bmm(a, b, dims):
    return lax.dot_general(a.astype(BF16), b.astype(BF16), _BMM_DIMS[dims], preferred_element_type=F32)


def _rwkv_chunk_kernel(r_ref, k_ref, v_ref, ld_ref, kk_ref, b_ref, g_ref, li_ref, s0_ref,
                       lnw_ref, lnb_ref, rk_ref, o_ref, sout_ref, s_ref, *, tt, c):
    t = pl.program_id(1)

    @pl.when(t == 0)
    def _():
        s_ref[...] = s0_ref[0]

    nch = tt // c
    nb = nch * N_HEADS

    def heads(x):
        return jnp.stack([x[ch * c:(ch + 1) * c, h * HEAD_DIM:(h + 1) * HEAD_DIM]
                          for ch in range(nch) for h in range(N_HEADS)])

    def head_rows(ref):
        return jnp.stack([ref[:, h * HEAD_DIM:(h + 1) * HEAD_DIM] for h in range(N_HEADS)])

    ri = lax.broadcasted_iota(jnp.int32, (nb, c, c), 1)
    ci = lax.broadcasted_iota(jnp.int32, (nb, c, c), 2)
    incl = (ri >= ci).astype(F32)
    strict = (ri > ci).astype(F32)
    eye = (ri == ci).astype(F32)
    n_sq = int(round(math.log2(c))) - 1

    r_all = r_ref[0]
    k_all = k_ref[0]
    v_all = v_ref[0]
    b_all = b_ref[0]
    l_inc = li_ref[0]
    l_tot = jnp.concatenate(
        [jnp.broadcast_to(l_inc[(ch + 1) * c - 1:(ch + 1) * c, :], (c, D_BR)) for ch in range(nch)], axis=0)
    e_neg = jnp.exp(-l_inc)
    e_end = jnp.exp(l_tot - l_inc)
    a_t = heads(-kk_ref[0] * jnp.exp(l_inc - ld_ref[0]))
    r_t = heads(r_all * jnp.exp(l_inc))
    b_t = heads(b_all * e_neg)
    k_t = heads(k_all * e_neg)
    b_e = heads(b_all * e_end)
    k_e = heads(k_all * e_end)
    d_tot = heads(jnp.exp(l_tot))[:, 0:1, :]
    r = heads(r_all)
    k = heads(k_all)
    v = heads(v_all)

    gram = _bmm(jnp.concatenate([a_t, r_t], axis=1), jnp.concatenate([b_t, k_t], axis=1), 'nt')
    m_ab = gram[:, :c, :c] * strict
    m_ak = gram[:, :c, c:] * strict
    p_rb = gram[:, c:, :c] * incl
    p_rk = gram[:, c:, c:] * incl
    pw = m_ab
    tinv = eye + m_ab
    for _ in range(n_sq):
        pw = _bmm(pw, pw, 'nn')
        tinv = tinv + _bmm(tinv, pw, 'nn')
    wu = _bmm(tinv, jnp.concatenate([a_t, _bmm(m_ak, v, 'nn')], axis=2), 'nn')
    o0 = _bmm(p_rk, v, 'nn')
    uv_rhs = jnp.concatenate([b_e, k_e], axis=1)

    lnw = head_rows(lnw_ref)
    lnb = head_rows(lnb_ref)
    rk = head_rows(rk_ref)
    s = s_ref[...]
    for ch in range(nch):
        sl = slice(ch * N_HEADS, (ch + 1) * N_HEADS)
        y = _bmm(jnp.concatenate([wu[sl, :, :HEAD_DIM], r_t[sl]], axis=1), s, 'nt')
        u = y[:, :c] + wu[sl, :, HEAD_DIM:]
        o = y[:, c:] + _bmm(p_rb[sl], u, 'nn') + o0[sl]
        uv_t = jnp.swapaxes(jnp.concatenate([u, v[sl]], axis=1), 1, 2)
        s = s * d_tot[sl] + _bmm(uv_t, uv_rhs[sl], 'nn')
        mean = jnp.mean(o, axis=-1, keepdims=True)
        var = jnp.mean(jnp.square(o - mean), axis=-1, keepdims=True)
        o = (o - mean) * lax.rsqrt(var + RWKV_LN_EPS) * lnw + lnb
        o = o + jnp.sum(r[sl] * k[sl] * rk, axis=-1, keepdims=True) * v[sl]
        rows = slice(ch * c, (ch + 1) * c)
        o_ref[0, rows, :] = (jnp.concatenate([o[h] for h in range(N_HEADS)], axis=-1)
                             * g_ref[0, rows, :])
    s_ref[...] = s
    sout_ref[0] = s


def rwkv_chunk(r, k, v, ld, kk, bv, g, l_inc, s0, ln_w, ln_b, r_k):
    b, t, _ = r.shape
    c = min(t, RWKV_CHUNK)
    tt = min(t, 256)
    kern = functools.partial(_rwkv_chunk_kernel, tt=tt, c=c)
    blk = pl.BlockSpec((1, tt, D_BR), lambda i, j: (i, j, 0))
    sblk = pl.BlockSpec((1, N_HEADS, HEAD_DIM, HEAD_DIM), lambda i, j: (i, 0, 0, 0))
    return pl.pallas_call(
        kern,
        grid=(b, t // tt),
        in_specs=[blk] * 8 + [sblk, _const_spec(ln_w.shape), _const_spec(ln_b.shape),
                              _const_spec(r_k.shape)],
        out_specs=[blk, sblk],
        out_shape=[jax.ShapeDtypeStruct((b, t, D_BR), F32),
                   jax.ShapeDtypeStruct((b, N_HEADS, HEAD_DIM, HEAD_DIM), F32)],
        scratch_shapes=[pltpu.VMEM((N_HEADS, HEAD_DIM, HEAD_DIM), F32)],
        compiler_params=_params(("arbitrary", "arbitrary")),
        name="rwkv_chunk",
    )(r, k, v, ld, kk, bv, g, l_inc, s0, ln_w, ln_b, r_k)


SB_GROUP = 8


LOG2E = 1.4426950408889634


def _softplus2(z2):
    return jnp.maximum(z2, 0.0) + jnp.log2(1.0 + jnp.exp2(-jnp.abs(z2)))


def _sb_prompt_kernel(q_ref, k_ref, v_ref, o_ref, *, tq):
    i = pl.program_id(1)
    q = q_ref[0]
    ri = lax.broadcasted_iota(jnp.int32, (tq, tq), 0)
    ci = lax.broadcasted_iota(jnp.int32, (tq, tq), 1)
    later = (ri > ci).astype(BF16)

    def scores(j):
        start = pl.multiple_of(j * tq, tq)
        return _dot_nt(q, k_ref[0, pl.ds(start, tq), :]), v_ref[0, pl.ds(start, tq), :]

    def full_blocks(j, n, car, acc):
        zv = [scores(j - b) for b in range(n)]
        sps = [_softplus2(z) for z, _ in zv]
        afts = [jnp.dot(sp.astype(BF16), later, preferred_element_type=F32) for sp in sps]
        atts = [jnp.exp2(zv[b][0] - sps[b] - afts[b]).astype(BF16) for b in range(n)]
        for b in range(n):
            acc = acc + jnp.exp2(-car) * jnp.dot(atts[b], zv[b][1], preferred_element_type=F32)
            car = car + jnp.sum(sps[b], axis=-1, keepdims=True)
        return car, acc

    z, vb = scores(i)
    mask = ci < ri
    sp = jnp.where(mask, _softplus2(z), 0.0)
    aft = jnp.dot(sp.astype(BF16), later, preferred_element_type=F32)
    att = jnp.where(mask, jnp.exp2(z - sp - aft), 0.0)
    acc = jnp.dot(att.astype(BF16), vb, preferred_element_type=F32)
    car = jnp.sum(sp, axis=-1, keepdims=True)

    j = i - 1
    g = 1
    while g < SB_GROUP:
        n_g = lax.shift_right_logical(i, g.bit_length() - 1) & 1
        car, acc = lax.fori_loop(
            0, n_g, lambda p, ca, j=j, g=g: full_blocks(j, g, *ca), (car, acc))
        j = j - g * n_g
        g *= 2
    _, acc = lax.fori_loop(
        0, lax.shift_right_logical(i, SB_GROUP.bit_length() - 1),
        lambda p, ca: full_blocks(j - SB_GROUP * p, SB_GROUP, *ca), (car, acc))
    o_ref[0] = acc


def sb_prompt(q, k, v):
    h, t, _ = q.shape
    tq = min(t, 256)
    kern = functools.partial(_sb_prompt_kernel, tq=tq)
    return pl.pallas_call(
        kern,
        grid=(h, t // tq),
        in_specs=[pl.BlockSpec((1, tq, 2 * HEAD_DIM), lambda a, i: (a, i, 0)),
                  pl.BlockSpec((1, t, 2 * HEAD_DIM), lambda a, i: (a, 0, 0)),
                  pl.BlockSpec((1, t, HEAD_DIM), lambda a, i: (a, 0, 0))],
        out_specs=pl.BlockSpec((1, tq, HEAD_DIM), lambda a, i: (a, i, 0)),
        out_shape=jax.ShapeDtypeStruct((h, t, HEAD_DIM), F32),
        compiler_params=_params(("arbitrary", "arbitrary")),
        name="sb_prompt",
    )(q, k, v)


PAGES_PER_STEP = 32


def _sb_paged_kernel(pt_ref, q_ref, bias_ref, kn_ref, vn_ref, *rest, t_new, past_len):
    nk = PAGES_PER_STEP
    k_pages = rest[:nk]
    v_pages = rest[nk:2 * nk]
    o_ref = rest[2 * nk]
    acc_ref, car_ref = rest[2 * nk + 1:]
    s = pl.program_id(1)
    q = q_ref[0] * (LOG2E * HEAD_DIM ** -0.5)
    qh = [q[:, h * HEAD_DIM:(h + 1) * HEAD_DIM].astype(BF16) for h in range(N_HEADS)]
    bias = bias_ref[...]
    ri = lax.broadcasted_iota(jnp.int32, (PAGE_SIZE, PAGE_SIZE), 0)
    ci = lax.broadcasted_iota(jnp.int32, (PAGE_SIZE, PAGE_SIZE), 1)
    later = (ri > ci).astype(BF16)

    def head_rows(x):
        return [x[h * t_new:(h + 1) * t_new] for h in range(N_HEADS)]

    @pl.when(s == 0)
    def _():
        kn = kn_ref[0].astype(BF16)
        vn = vn_ref[0].astype(BF16)
        cols = [slice(h * HEAD_DIM, (h + 1) * HEAD_DIM) for h in range(N_HEADS)]
        z = jnp.concatenate([_dot_nt(qh[h], kn[:, cols[h]]) for h in range(N_HEADS)], axis=0) + bias
        qi = jnp.concatenate([lax.broadcasted_iota(jnp.int32, (t_new, PAGE_SIZE), 0)] * N_HEADS, axis=0)
        mask = lax.broadcasted_iota(jnp.int32, (N_HEADS * t_new, PAGE_SIZE), 1) < qi
        sp = jnp.where(mask, _softplus2(z), 0.0)
        aft = jnp.dot(sp.astype(BF16), later, preferred_element_type=F32)
        att = head_rows(jnp.where(mask, jnp.exp2(z - sp - aft), 0.0).astype(BF16))
        acc_ref[...] = jnp.concatenate(
            [jnp.dot(att[h], vn[:, cols[h]], preferred_element_type=F32) for h in range(N_HEADS)], axis=0)
        car_ref[...] = jnp.sum(sp, axis=-1, keepdims=True)

    def gather_head(pages, h):
        return jnp.concatenate([p[0, 0, h] for p in pages], axis=1).astype(BF16)

    z = jnp.concatenate(
        [jnp.dot(qh[h], gather_head(k_pages, h), preferred_element_type=F32) for h in range(N_HEADS)],
        axis=0) + bias
    sp = _softplus2(z)
    car = car_ref[...]
    afts = []
    for rr in range(nk):
        sp_rr = sp[:, rr * PAGE_SIZE:(rr + 1) * PAGE_SIZE]
        afts.append(jnp.dot(sp_rr.astype(BF16), later, preferred_element_type=F32) + car)
        car = car + jnp.sum(sp_rr, axis=-1, keepdims=True)
    att = head_rows(jnp.exp2(z - sp - jnp.concatenate(afts, axis=1)).astype(BF16))
    acc_ref[...] += jnp.concatenate(
        [_dot_nt(att[h], gather_head(v_pages, h)) for h in range(N_HEADS)], axis=0)
    car_ref[...] = car

    @pl.when(s == pl.num_programs(1) - 1)
    def _():
        o_ref[0] = jnp.concatenate(head_rows(acc_ref[...]), axis=-1)


def sb_paged(q, bias_rows, k_new, v_new, cache_k, cache_v, page_table, layer):
    b, t_new, _ = q.shape
    n_pages = page_table.shape[1]
    past_len = n_pages * PAGE_SIZE
    nk = PAGES_PER_STEP
    n_steps = n_pages // nk
    kern = functools.partial(_sb_paged_kernel, t_new=t_new, past_len=past_len)

    def page_spec(rr):
        return pl.BlockSpec(
            (1, 1, N_HEADS, HEAD_DIM, PAGE_SIZE),
            lambda i, s, pt: (layer, pt[i, n_pages - 1 - (s * nk + rr)], 0, 0, 0))

    n_rows = N_HEADS * t_new
    grid_spec = pltpu.PrefetchScalarGridSpec(
        num_scalar_prefetch=1,
        grid=(b, n_steps),
        in_specs=[pl.BlockSpec((1, t_new, D_BR), lambda i, s, pt: (i, 0, 0)),
                  pl.BlockSpec((n_rows, 1), lambda i, s, pt: (0, 0)),
                  pl.BlockSpec((1, PAGE_SIZE, D_BR), lambda i, s, pt: (i, 0, 0)),
                  pl.BlockSpec((1, PAGE_SIZE, D_BR), lambda i, s, pt: (i, 0, 0))]
                 + [page_spec(rr) for rr in range(nk)] * 2,
        out_specs=pl.BlockSpec((1, t_new, D_BR), lambda i, s, pt: (i, 0, 0)),
        scratch_shapes=[pltpu.VMEM((n_rows, HEAD_DIM), F32), pltpu.VMEM((n_rows, 1), F32)],
    )
    return pl.pallas_call(
        kern,
        grid_spec=grid_spec,
        out_shape=jax.ShapeDtypeStruct((b, t_new, D_BR), F32),
        compiler_params=_params(("arbitrary", "arbitrary")),
        name="sb_paged",
    )(page_table, q, bias_rows, k_new, v_new, *([cache_k] * nk), *([cache_v] * nk))


def _s5_kernel(u_ref, hre_ref, him_ref, bre_ref, bim_ref, cre_ref, cim_ref, d_ref, wglu_ref,
               powr_ref, powi_ref, lvr_ref, lvi_ref,
               o_ref, sre_ref, sim_ref, cr_ref, cim_s_ref, *, tt):
    t = pl.program_id(1)

    @pl.when(t == 0)
    def _():
        cr_ref[...] = hre_ref[0]
        cim_s_ref[...] = him_ref[0]

    u = u_ref[0]
    xr = _dot(u, bre_ref[...])
    xi = _dot(u, bim_ref[...])
    row_in_group = lax.broadcasted_iota(jnp.int32, (tt, S5_N), 0) & 7
    for lvl, off in enumerate((1, 2, 4)):
        lr = lvr_ref[lvl:lvl + 1, :]
        li = lvi_ref[lvl:lvl + 1, :]
        m = row_in_group >= off
        sr = jnp.where(m, pltpu.roll(xr, off, 0), 0.0)
        si = jnp.where(m, pltpu.roll(xi, off, 0), 0.0)
        xr, xi = xr + lr * sr - li * si, xi + lr * si + li * sr
    pr = powr_ref[0:8, :]
    pi = powi_ref[0:8, :]
    hr = cr_ref[...]
    hi = cim_s_ref[...]
    groups_r, groups_i = [], []
    for g in range(tt // 8):
        gr = xr[g * 8:(g + 1) * 8, :]
        gi = xi[g * 8:(g + 1) * 8, :]
        gr, gi = gr + pr * hr - pi * hi, gi + pr * hi + pi * hr
        groups_r.append(gr)
        groups_i.append(gi)
        hr = gr[7:8, :]
        hi = gi[7:8, :]
    xr = jnp.concatenate(groups_r, axis=0)
    xi = jnp.concatenate(groups_i, axis=0)
    cr_ref[...] = xr[tt - 1:tt, :]
    cim_s_ref[...] = xi[tt - 1:tt, :]
    sre_ref[0] = xr[tt - 1:tt, :]
    sim_ref[0] = xi[tt - 1:tt, :]
    y = _dot(xr, cre_ref[...]) - _dot(xi, cim_ref[...]) + d_ref[...] * u
    z = _dot(jax.nn.gelu(y), wglu_ref[...])
    o_ref[0] = z[:, :D_BR] * _sigmoid(z[:, D_BR:])


def s5_mixer(u, h_re, h_im, b_re, b_im, c_re, c_im, d, w_glu, pow_re, pow_im, lv_re, lv_im):
    b, t, _ = u.shape
    tt = min(t, 256)
    kern = functools.partial(_s5_kernel, tt=tt)
    consts = (b_re, b_im, c_re, c_im, d, w_glu, pow_re, pow_im, lv_re, lv_im)
    sblk = pl.BlockSpec((1, 1, S5_N), lambda i, j: (i, 0, 0))
    return pl.pallas_call(
        kern,
        grid=(b, t // tt),
        in_specs=[pl.BlockSpec((1, tt, D_BR), lambda i, j: (i, j, 0)), sblk, sblk]
                 + [_const_spec(c.shape) for c in consts],
        out_specs=[pl.BlockSpec((1, tt, D_BR), lambda i, j: (i, j, 0)), sblk, sblk],
        out_shape=[jax.ShapeDtypeStruct((b, t, D_BR), F32),
                   jax.ShapeDtypeStruct((b, 1, S5_N), F32),
                   jax.ShapeDtypeStruct((b, 1, S5_N), F32)],
        scratch_shapes=[pltpu.VMEM((1, S5_N), F32), pltpu.VMEM((1, S5_N), F32)],
        compiler_params=_params(("arbitrary", "arbitrary")),
        name="s5_mixer",
    )(u, h_re, h_im, *consts)


def _merge_kernel(x_ref, bp_ref, br_ref, bs_ref, b5_ref, gate_ref, wb_ref, wo_ref, o_ref):
    if len(bs_ref.shape) == 3:
        o_sb = jnp.concatenate([bs_ref[h] for h in range(N_HEADS)], axis=-1)
    else:
        o_sb = bs_ref[...]
    merged = None
    for n, br in enumerate((bp_ref[...], br_ref[...], o_sb, b5_ref[...])):
        lifted = _dot(br, wb_ref[n])
        term = _sigmoid(gate_ref[:, n * D_MODEL:(n + 1) * D_MODEL]) * lifted
        merged = term if merged is None else merged + term
    o_ref[...] = x_ref[...] + _dot(merged, wo_ref[...])


def merge(x, o_pool, o_rwkv, o_sb, o_s5, gates, w_branch, w_out):
    m = x.shape[0]
    tm = min(m, 256)
    row = lambda n: pl.BlockSpec((tm, n), lambda i: (i, 0))
    sb_spec = row(D_BR) if o_sb.ndim == 2 else pl.BlockSpec((N_HEADS, tm, HEAD_DIM), lambda i: (0, i, 0))
    return pl.pallas_call(
        _merge_kernel,
        grid=(m // tm,),
        in_specs=[row(D_MODEL), row(D_BR), row(D_BR), sb_spec, row(D_BR), row(4 * D_MODEL),
                  _const_spec(w_branch.shape), _const_spec(w_out.shape)],
        out_specs=row(D_MODEL),
        out_shape=jax.ShapeDtypeStruct((m, D_MODEL), F32),
        compiler_params=_params(("arbitrary",)),
        name="merge",
    )(x, o_pool, o_rwkv, o_sb, o_s5, gates, w_branch, w_out)


def _xattn_kernel(x_ref, mk_ref, mv_ref, g_ref, wq_ref, qg_ref, wo_ref, o_ref):
    x = x_ref[0]
    hn = _rms(x, g_ref[...])
    q = _head_rms(_dot(hn, wq_ref[...]), qg_ref[...]) * HEAD_DIM ** -0.5
    outs = []
    for h in range(N_HEADS):
        cols = slice(h * HEAD_DIM, (h + 1) * HEAD_DIM)
        s = _dot(q[:, cols], mk_ref[0, 0, h])
        s = s - jnp.max(s, axis=-1, keepdims=True)
        e = jnp.exp(s)
        pr = e / jnp.sum(e, axis=-1, keepdims=True)
        outs.append(_dot_nt(pr.astype(BF16), mv_ref[0, 0, h].astype(BF16)))
    o = jnp.concatenate(outs, axis=-1)
    o_ref[0] = x + _dot(o, wo_ref[...])


def cross_attention(x, mem_k, mem_v, layer, g, wq, qg, wo):
    b, t, _ = x.shape
    tt = min(t, 512)
    mem_spec = pl.BlockSpec((1, 1, N_HEADS, HEAD_DIM, N_MEM), lambda i, j: (layer, i, 0, 0, 0))
    return pl.pallas_call(
        _xattn_kernel,
        grid=(b, t // tt),
        in_specs=[pl.BlockSpec((1, tt, D_MODEL), lambda i, j: (i, j, 0)), mem_spec, mem_spec,
                  _const_spec(g.shape), _const_spec(wq.shape), _const_spec(qg.shape),
                  _const_spec(wo.shape)],
        out_specs=pl.BlockSpec((1, tt, D_MODEL), lambda i, j: (i, j, 0)),
        out_shape=jax.ShapeDtypeStruct((b, t, D_MODEL), F32),
        compiler_params=_params(("arbitrary", "arbitrary")),
        name="cross_attention",
    )(x, mem_k, mem_v, g, wq, qg, wo)


def _xattn_batched_kernel(x_ref, mk_ref, mv_ref, g_ref, wq_ref, qg_ref, wo_ref, o_ref, *, b, t):
    x = x_ref[...]
    hn = _rms(x, g_ref[...])
    q = (_head_rms(_dot(hn, wq_ref[...]), qg_ref[...]) * HEAD_DIM ** -0.5).astype(BF16)
    cols = [slice(h * HEAD_DIM, (h + 1) * HEAD_DIM) for h in range(N_HEADS)]
    s = jnp.concatenate(
        [jnp.dot(q[bi * t:(bi + 1) * t, cols[h]], mk_ref[0, bi, h].astype(BF16), preferred_element_type=F32)
         for bi in range(b) for h in range(N_HEADS)], axis=0)
    s = s - jnp.max(s, axis=-1, keepdims=True)
    e = jnp.exp(s)
    pr = (e / jnp.sum(e, axis=-1, keepdims=True)).astype(BF16)
    o = jnp.concatenate(
        [jnp.concatenate(
            [_dot_nt(pr[(bi * N_HEADS + h) * t:(bi * N_HEADS + h + 1) * t], mv_ref[0, bi, h].astype(BF16))
             for h in range(N_HEADS)], axis=-1)
         for bi in range(b)], axis=0)
    o_ref[...] = x + _dot(o, wo_ref[...])


def cross_attention_batched(x, mem_k, mem_v, layer, g, wq, qg, wo):
    b, t, _ = x.shape
    m = b * t
    kern = functools.partial(_xattn_batched_kernel, b=b, t=t)
    mem_spec = pl.BlockSpec((1, b, N_HEADS, HEAD_DIM, N_MEM), lambda i: (layer, 0, 0, 0, 0))
    out = pl.pallas_call(
        kern,
        grid=(1,),
        in_specs=[_const_spec((m, D_MODEL)), mem_spec, mem_spec, _const_spec(g.shape),
                  _const_spec(wq.shape), _const_spec(qg.shape), _const_spec(wo.shape)],
        out_specs=_const_spec((m, D_MODEL)),
        out_shape=jax.ShapeDtypeStruct((m, D_MODEL), F32),
        compiler_params=_params(("arbitrary",)),
        name="cross_attention_batched",
    )(x.reshape(m, D_MODEL), mem_k, mem_v, g, wq, qg, wo)
    return out.reshape(b, t, D_MODEL)


def _memkv_kernel(mem_ref, g_ref, wk_ref, wv_ref, kg_ref, k_ref, v_ref, kt_ref, vt_ref):
    mn = _rms(mem_ref[...], g_ref[...])
    k = _head_rms(_dot(mn, wk_ref[...]), kg_ref[...])
    v = _dot(mn, wv_ref[...])
    k_ref[...] = k
    v_ref[...] = v
    kt_ref[...] = k.T
    vt_ref[...] = v.T


def memory_kv(mem, g, wk, wv, kg):
    m = mem.shape[0]
    args = (mem, g, wk, wv, kg)
    return pl.pallas_call(
        _memkv_kernel,
        grid=(1,),
        in_specs=[_const_spec(a.shape) for a in args],
        out_specs=[_const_spec((m, D_BR))] * 2 + [_const_spec((D_BR, m))] * 2,
        out_shape=[jax.ShapeDtypeStruct((m, D_BR), F32)] * 2 + [jax.ShapeDtypeStruct((D_BR, m), F32)] * 2,
        compiler_params=_params(("arbitrary",)),
        name="memory_kv",
    )(*args)


def _ffn_kernel(x_ref, g_ref, wg_ref, wu_ref, wd_ref, o_ref):
    x = x_ref[...]
    hn = _rms(x, g_ref[...]).astype(BF16)
    a = jnp.dot(hn, wg_ref[...], preferred_element_type=F32)
    bq = jnp.dot(hn, wu_ref[...], preferred_element_type=F32)
    act = a * _sigmoid(a) * bq
    o_ref[...] = x + _dot(act, wd_ref[...])


def ffn(x, g, w_gate, w_up, w_down):
    m = x.shape[0]
    tm = min(m, 256)
    row = pl.BlockSpec((tm, D_MODEL), lambda i: (i, 0))
    return pl.pallas_call(
        _ffn_kernel,
        grid=(m // tm,),
        in_specs=[row, _const_spec(g.shape), _const_spec(w_gate.shape), _const_spec(w_up.shape),
                  _const_spec(w_down.shape)],
        out_specs=row,
        out_shape=jax.ShapeDtypeStruct((m, D_MODEL), F32),
        compiler_params=_params(("arbitrary",)),
        name="ffn",
    )(x, g, w_gate, w_up, w_down)


def _block_diag(blocks):
    g, m, n = blocks.shape
    eye = jnp.eye(g, dtype=blocks.dtype)
    return (eye[:, None, :, None] * blocks[:, :, None, :]).reshape(g * m, g * n)


def _tile_heads(v):
    return jnp.tile(v, N_HEADS).reshape(1, N_HEADS * v.shape[0])


def _cmul(ar, ai, br, bi):
    return ar * br - ai * bi, ar * bi + ai * br


def _s5_constants(a_re, a_im, log_dt, b_re, b_im, c_re, c_im, tt):
    dt_g = jnp.exp(log_dt)[:, None]
    mag = jnp.exp(dt_g * a_re)
    ab_re, ab_im = mag * jnp.cos(dt_g * a_im), mag * jnp.sin(dt_g * a_im)
    den = a_re * a_re + a_im * a_im
    n_re = ab_re - 1.0
    f_re = (n_re * a_re + ab_im * a_im) / den
    f_im = (ab_im * a_re - n_re * a_im) / den
    bb_re = f_re[..., None] * b_re - f_im[..., None] * b_im
    bb_im = f_re[..., None] * b_im + f_im[..., None] * b_re
    bmat_re = _block_diag(jnp.swapaxes(bb_re, 1, 2))
    bmat_im = _block_diag(jnp.swapaxes(bb_im, 1, 2))
    cmat_re = _block_diag(jnp.swapaxes(c_re, 1, 2))
    cmat_im = _block_diag(jnp.swapaxes(c_im, 1, 2))
    lam_re = ab_re.reshape(1, S5_N)
    lam_im = ab_im.reshape(1, S5_N)
    pow_re, pow_im = lam_re, lam_im
    lv_re, lv_im = [lam_re], [lam_im]
    cur_re, cur_im = lam_re, lam_im
    n = 1
    while n < tt:
        nr, ni = _cmul(pow_re, pow_im, cur_re, cur_im)
        pow_re = jnp.concatenate([pow_re, nr], axis=0)
        pow_im = jnp.concatenate([pow_im, ni], axis=0)
        cur_re, cur_im = _cmul(cur_re, cur_im, cur_re, cur_im)
        lv_re.append(cur_re)
        lv_im.append(cur_im)
        n *= 2
    n_lv = max(len(lv_re) - 1, 1)
    lv_re = jnp.concatenate(lv_re[:n_lv], axis=0)
    lv_im = jnp.concatenate(lv_im[:n_lv], axis=0)
    return (bmat_re.astype(BF16), bmat_im.astype(BF16), cmat_re.astype(BF16), cmat_im.astype(BF16),
            pow_re, pow_im, lv_re, lv_im)


def _layer_weights(l, w, tt_prompt, tt_sample):
    w_in = w['w_in'][l]
    lw = {
        'norm_mix': w['norm_mix'][l].reshape(1, D_MODEL),
        'norm_cross': w['norm_cross'][l].reshape(1, D_MODEL),
        'norm_mem': w['norm_mem'][l].reshape(1, D_MODEL),
        'norm_ffn': w['norm_ffn'][l].reshape(1, D_MODEL),
        'w_pool_in': w_in[:, :OFF_RWKV].astype(BF16),
        'w_rwkv_in': w_in[:, OFF_RWKV:OFF_SB].astype(BF16),
        'w_sb_in': w_in[:, OFF_SB:OFF_S5].astype(BF16),
        'w_s5_in': w_in[:, OFF_S5:OFF_GATE].astype(BF16),
        'w_gate_in': w_in[:, OFF_GATE:].astype(BF16),
        'sb_q_norm': _tile_heads(w['sb_q_norm'][l]),
        'sb_k_norm': _tile_heads(w['sb_k_norm'][l]),
        'sb_bias': w['sb_bias'][l],
        'pool_w': _block_diag(w['pool_w'][l]),
        'pool_scale': w['pool_scale'][l].reshape(1, D_BR),
        'rwkv_mu': w['rwkv_mu'][l].reshape(1, RWKV_COLS),
        'rwkv_w0': w['rwkv_w0'][l].reshape(1, D_BR),
        'rwkv_w_up': w['rwkv_w_up'][l].astype(BF16),
        'rwkv_a0': w['rwkv_a0'][l].reshape(1, D_BR),
        'rwkv_a_up': w['rwkv_a_up'][l].astype(BF16),
        'rwkv_g_up': w['rwkv_g_up'][l].astype(BF16),
        'rwkv_k_k': w['rwkv_k_k'][l].reshape(1, D_BR),
        'rwkv_k_a': w['rwkv_k_a'][l].reshape(1, D_BR),
        'rwkv_r_k': w['rwkv_r_k'][l].reshape(1, D_BR),
        'rwkv_ln_w': w['rwkv_ln_w'][l].reshape(1, D_BR),
        'rwkv_ln_b': w['rwkv_ln_b'][l].reshape(1, D_BR),
        's5_d': w['s5_d'][l].reshape(1, D_BR),
        's5_w_glu': w['s5_w_glu'][l].astype(BF16),
        'w_branch': w['w_branch'][l].astype(BF16),
        'w_out': w['w_out'][l].astype(BF16),
        'xa_w_q': w['xa_w_q'][l].astype(BF16),
        'xa_w_k': w['xa_w_k'][l].astype(BF16),
        'xa_w_v': w['xa_w_v'][l].astype(BF16),
        'xa_q_norm': _tile_heads(w['xa_q_norm'][l]),
        'xa_k_norm': _tile_heads(w['xa_k_norm'][l]),
        'xa_w_o': w['xa_w_o'][l].astype(BF16),
        'ffn_w_gate': w['ffn_w_gate'][l].astype(BF16),
        'ffn_w_up': w['ffn_w_up'][l].astype(BF16),
        'ffn_w_down': w['ffn_w_down'][l].astype(BF16),
    }
    b2 = w['sb_bias'][l] * LOG2E
    p1 = b2.astype(BF16).astype(F32)
    p2 = (b2 - p1).astype(BF16).astype(F32)
    p3 = (b2 - p1 - p2).astype(BF16).astype(F32)
    k_tails = jnp.pad(jnp.stack([p1, p2, p3], axis=1), ((0, 0), (0, HEAD_DIM - 3)))
    q_tail = jnp.pad(jnp.ones((1, 3), F32), ((0, 0), (0, HEAD_DIM - 3)))
    lw['sb_tails'] = jnp.concatenate([k_tails, q_tail], axis=0)
    s5_args = (w['s5_a_re'][l], w['s5_a_im'][l], w['s5_log_dt'][l], w['s5_b_re'][l], w['s5_b_im'][l],
               w['s5_c_re'][l], w['s5_c_im'][l])
    lw['s5_prompt'] = lw['s5_sample'] = _s5_constants(*s5_args, 8)
    return lw


def _trunk_layer(x, pos0, mem_k, mem_v, mem_layer, sb_fn, pool_buf, shift, wkv, s5_re, s5_im, lw, s5c,
                 fresh):
    b, t, _ = x.shape
    m = b * t
    xf = x.reshape(m, D_MODEL)
    u_pool, p_rwkv, q, k, v, u_s5, gates, *head_major = in_proj(
        xf, lw['norm_mix'], lw['w_pool_in'], lw['w_rwkv_in'], lw['w_sb_in'], lw['w_s5_in'],
        lw['w_gate_in'], lw['sb_q_norm'], lw['sb_k_norm'], lw['sb_tails'] if fresh else None)
    o_pool, new_pool = pool_mixer(u_pool.reshape(b, t, D_BR), pool_buf, lw['pool_w'],
                                  lw['pool_scale'], pos0)
    p3 = p_rwkv.reshape(b, t, RWKV_COLS)
    pre = rwkv_pre(p3, shift, lw['rwkv_mu'], lw['rwkv_w0'], lw['rwkv_w_up'], lw['rwkv_a0'],
                   lw['rwkv_a_up'], lw['rwkv_g_up'], lw['rwkv_k_k'], lw['rwkv_k_a'])
    o_rwkv, new_wkv = rwkv_chunk(*pre, wkv, lw['rwkv_ln_w'], lw['rwkv_ln_b'], lw['rwkv_r_k'])
    new_shift = p3[:, -1]
    q3, k3, v3 = (z.reshape(b, t, D_BR) for z in (q, k, v))
    o_sb = sb_fn(*head_major) if fresh else sb_fn(q3, k3, v3).reshape(m, D_BR)
    o_s5, new_re, new_im = s5_mixer(u_s5.reshape(b, t, D_BR), s5_re, s5_im, s5c[0], s5c[1], s5c[2],
                                    s5c[3], lw['s5_d'], lw['s5_w_glu'], *s5c[4:])
    x1 = merge(xf, o_pool.reshape(m, D_BR), o_rwkv.reshape(m, D_BR), o_sb,
               o_s5.reshape(m, D_BR), gates, lw['w_branch'], lw['w_out'])
    xattn = cross_attention if fresh else cross_attention_batched
    x2 = xattn(x1.reshape(b, t, D_MODEL), mem_k, mem_v, mem_layer, lw['norm_cross'],
               lw['xa_w_q'], lw['xa_q_norm'], lw['xa_w_o'])
    x3 = ffn(x2.reshape(m, D_MODEL), lw['norm_ffn'], lw['ffn_w_gate'], lw['ffn_w_up'],
             lw['ffn_w_down'])
    return x3.reshape(b, t, D_MODEL), (k3, v3, new_pool, new_shift, new_wkv, new_re, new_im)


def kernel(x_prompt, x_sample, cache_sb_k, cache_sb_v, cache_mem_k, cache_mem_v, state_pool, state_rwkv_shift, state_rwkv_wkv, state_s5_re, state_s5_im, page_table, mem_prompt, norm_mix, norm_cross, norm_mem, norm_ffn, w_in, pool_w, pool_scale, rwkv_mu, rwkv_w0, rwkv_w_up, rwkv_a0, rwkv_a_up, rwkv_g_up, rwkv_k_k, rwkv_k_a, rwkv_r_k, rwkv_ln_w, rwkv_ln_b, sb_q_norm, sb_k_norm, sb_bias, s5_a_re, s5_a_im, s5_log_dt, s5_b_re, s5_b_im, s5_c_re, s5_c_im, s5_d, s5_w_glu, w_branch, w_out, xa_w_q, xa_w_k, xa_w_v, xa_q_norm, xa_k_norm, xa_w_o, ffn_w_gate, ffn_w_up, ffn_w_down):
    weights = dict(
        norm_mix=norm_mix, norm_cross=norm_cross, norm_mem=norm_mem, norm_ffn=norm_ffn, w_in=w_in,
        pool_w=pool_w, pool_scale=pool_scale, rwkv_mu=rwkv_mu, rwkv_w0=rwkv_w0, rwkv_w_up=rwkv_w_up,
        rwkv_a0=rwkv_a0, rwkv_a_up=rwkv_a_up, rwkv_g_up=rwkv_g_up, rwkv_k_k=rwkv_k_k,
        rwkv_k_a=rwkv_k_a, rwkv_r_k=rwkv_r_k, rwkv_ln_w=rwkv_ln_w, rwkv_ln_b=rwkv_ln_b,
        sb_q_norm=sb_q_norm, sb_k_norm=sb_k_norm, sb_bias=sb_bias, s5_a_re=s5_a_re, s5_a_im=s5_a_im,
        s5_log_dt=s5_log_dt, s5_b_re=s5_b_re, s5_b_im=s5_b_im, s5_c_re=s5_c_re, s5_c_im=s5_c_im,
        s5_d=s5_d, s5_w_glu=s5_w_glu, w_branch=w_branch, w_out=w_out, xa_w_q=xa_w_q, xa_w_k=xa_w_k,
        xa_w_v=xa_w_v, xa_q_norm=xa_q_norm, xa_k_norm=xa_k_norm, xa_w_o=xa_w_o,
        ffn_w_gate=ffn_w_gate, ffn_w_up=ffn_w_up, ffn_w_down=ffn_w_down)
    depth = w_in.shape[0]
    bp, tp, _ = x_prompt.shape
    bs, ts, _ = x_sample.shape
    n_pool = cache_sb_k.shape[1]
    cache_k = jnp.transpose(cache_sb_k, (0, 1, 3, 4, 2))
    cache_v = jnp.transpose(cache_sb_v, (0, 1, 3, 4, 2))
    mem_k_t = jnp.transpose(cache_mem_k, (0, 1, 3, 4, 2))
    mem_v_t = jnp.transpose(cache_mem_v, (0, 1, 3, 4, 2))
    past_len = page_table.shape[1] * PAGE_SIZE
    tt_s5_p = min(tp, 256)
    tt_s5_s = ts

    xp, xs = x_prompt, x_sample
    outs_p = [[] for _ in range(9)]
    outs_s = [[] for _ in range(7)]
    for l in range(depth):
        lw = _layer_weights(l, weights, tt_s5_p, tt_s5_s)
        mk_p, mv_p, mkt_p, mvt_p = memory_kv(mem_prompt.reshape(bp * N_MEM, D_MODEL), lw['norm_mem'],
                                             lw['xa_w_k'], lw['xa_w_v'], lw['xa_k_norm'])
        mem_t_shape = (1, bp, N_HEADS, HEAD_DIM, N_MEM)

        xp, st = _trunk_layer(
            xp, 0, mkt_p.reshape(mem_t_shape), mvt_p.reshape(mem_t_shape), 0, sb_prompt,
            jnp.zeros((bp, POOL_HDR, D_BR), F32), jnp.zeros((bp, 1, RWKV_COLS), F32),
            jnp.zeros((bp, N_HEADS, HEAD_DIM, HEAD_DIM), F32), jnp.zeros((bp, 1, S5_N), F32),
            jnp.zeros((bp, 1, S5_N), F32), lw, lw['s5_prompt'], True)
        for lst, val in zip(outs_p, (st[0], st[1], mk_p, mv_p) + st[2:]):
            lst.append(val)

        bias_rows = jnp.repeat(lw['sb_bias'] * LOG2E, ts).reshape(N_HEADS * ts, 1)

        def sb_sample_fn(q, k, v, l=l, bias_rows=bias_rows):
            pad = ((0, 0), (0, PAGE_SIZE - ts), (0, 0))
            return sb_paged(q, bias_rows, jnp.pad(k, pad), jnp.pad(v, pad), cache_k, cache_v,
                            page_table, l)

        pool_buf = jnp.pad(state_pool[l], ((0, 0), (1, 0), (0, 0)))
        xs, st = _trunk_layer(
            xs, past_len, mem_k_t, mem_v_t, l, sb_sample_fn,
            pool_buf, state_rwkv_shift[l].reshape(bs, 1, RWKV_COLS), state_rwkv_wkv[l],
            state_s5_re[l].reshape(bs, 1, S5_N), state_s5_im[l].reshape(bs, 1, S5_N),
            lw, lw['s5_sample'], False)
        for lst, val in zip(outs_s, st):
            lst.append(val)

    def stack(lst, shape):
        return jnp.stack(lst, 0).reshape((depth,) + shape)

    kv_p = (bp, tp, N_HEADS, HEAD_DIM)
    kv_s = (bs, ts, N_HEADS, HEAD_DIM)
    mem_shape = (bp, N_MEM, N_HEADS, HEAD_DIM)
    return (xp, xs,
            stack(outs_p[0], kv_p), stack(outs_p[1], kv_p),
            stack(outs_p[2], mem_shape), stack(outs_p[3], mem_shape),
            stack(outs_p[4], (bp, POOL_BUF, D_BR)), stack(outs_p[5], (bp, RWKV_COLS)),
            stack(outs_p[6], (bp, N_HEADS, HEAD_DIM, HEAD_DIM)),
            stack(outs_p[7], (bp, S5_GROUPS, S5_STATE)), stack(outs_p[8], (bp, S5_GROUPS, S5_STATE)),
            stack(outs_s[0], kv_s), stack(outs_s[1], kv_s),
            stack(outs_s[2], (bs, POOL_BUF, D_BR)), stack(outs_s[3], (bs, RWKV_COLS)),
            stack(outs_s[4], (bs, N_HEADS, HEAD_DIM, HEAD_DIM)),
            stack(outs_s[5], (bs, S5_GROUPS, S5_STATE)), stack(outs_s[6], (bs, S5_GROUPS, S5_STATE)))
```

```python
import functools
import math

import jax
import jax.numpy as jnp
from jax import lax
from jax.experimental import pallas as pl
from jax.experimental.pallas import tpu as pltpu

F32 = jnp.float32
BF16 = jnp.bfloat16
HIGHEST = lax.Precision.HIGHEST

D_MODEL = 1024
D_BR = 256
HEAD_DIM = 64
N_HEADS = 4
POOL_WINDOWS = (2, 4, 8, 16)
POOL_BUF = 15
POOL_HDR = 16
RWKV_COLS = 1024
RWKV_LN_EPS = 64e-5
S5_GROUPS = 16
S5_STATE = 64
S5_N = S5_GROUPS * S5_STATE
RWKV_CHUNK = 64
PAGE_SIZE = 128
N_MEM = 256
D_FF = 2816
RMS_EPS = 1e-6
OFF_RWKV = 256
OFF_SB = 1280
OFF_S5 = 2048
OFF_GATE = 2304

VMEM_LIMIT = 56 * 1024 * 1024


def _params(sem):
    return pltpu.CompilerParams(dimension_semantics=sem, vmem_limit_bytes=VMEM_LIMIT)


def _dot(a, b):
    return jnp.dot(a.astype(BF16), b.astype(BF16), preferred_element_type=F32)


def _dot_hi(a, b):
    return jnp.dot(a, b, precision=HIGHEST, preferred_element_type=F32)


def _dot_nt(a, b, precision=None):
    return lax.dot_general(a, b, (((1,), (1,)), ((), ())), precision=precision,
                           preferred_element_type=F32)


def _dot_tn(a, b, precision=None):
    return lax.dot_general(a, b, (((0,), (0,)), ((), ())), precision=precision,
                           preferred_element_type=F32)


def _rms(x, g):
    ms = jnp.mean(x * x, axis=-1, keepdims=True)
    return x * lax.rsqrt(ms + RMS_EPS) * g


def _head_rms(x, gain_row):
    parts = []
    for h in range(N_HEADS):
        xs = x[:, h * HEAD_DIM:(h + 1) * HEAD_DIM]
        ms = jnp.mean(xs * xs, axis=-1, keepdims=True)
        parts.append(xs * lax.rsqrt(ms + RMS_EPS))
    return jnp.concatenate(parts, axis=-1) * gain_row


def _sigmoid(x):
    return 1.0 / (1.0 + jnp.exp(-x))


def _log_sigmoid(z):
    return jnp.minimum(z, 0.0) - jnp.log1p(jnp.exp(-jnp.abs(z)))


def _const_spec(shape):
    nd = len(shape)
    return pl.BlockSpec(shape, lambda *_: (0,) * nd)


def _in_proj_kernel(x_ref, g_ref, wp_ref, wr_ref, wsb_ref, ws5_ref, qg_ref, kg_ref, *refs):
    head_major_refs = refs[:1] + refs[7:] if len(refs) > 6 else ()
    pool_ref, rwkv_ref, q_ref, k_ref, v_ref, s5_ref = refs[len(refs) > 6:][:6]
    h = _rms(x_ref[...], g_ref[...]).astype(BF16)
    pool_ref[...] = jnp.dot(h, wp_ref[...], preferred_element_type=F32)
    rwkv_ref[...] = jnp.dot(h, wr_ref[...], preferred_element_type=F32)
    sb = jnp.dot(h, wsb_ref[...], preferred_element_type=F32)
    q = _head_rms(sb[:, :D_BR], qg_ref[...])
    k = _head_rms(sb[:, D_BR:2 * D_BR], kg_ref[...])
    v = sb[:, 2 * D_BR:]
    q_ref[...] = q
    k_ref[...] = k
    v_ref[...] = v
    s5_ref[...] = jnp.dot(h, ws5_ref[...], preferred_element_type=F32)
    if head_major_refs:
        tails_ref, qh_ref, kh_ref, vh_ref = head_major_refs
        rows = q.shape[0]
        for hh in range(N_HEADS):
            cols = slice(hh * HEAD_DIM, (hh + 1) * HEAD_DIM)
            q_tail = jnp.broadcast_to(tails_ref[N_HEADS:N_HEADS + 1, :], (rows, HEAD_DIM))
            k_tail = jnp.broadcast_to(tails_ref[hh:hh + 1, :], (rows, HEAD_DIM))
            qh_ref[hh] = jnp.concatenate([q[:, cols] * (LOG2E * HEAD_DIM ** -0.5), q_tail], axis=-1).astype(BF16)
            kh_ref[hh] = jnp.concatenate([k[:, cols], k_tail], axis=-1).astype(BF16)
            vh_ref[hh] = v[:, cols].astype(BF16)


def in_proj(x, g, wp, wr, wsb, ws5, qg, kg, sb_tails=None):
    m = x.shape[0]
    tm = min(m, 256)
    row = lambda n: pl.BlockSpec((tm, n), lambda i: (i, 0))
    widths = (D_BR, RWKV_COLS, D_BR, D_BR, D_BR, D_BR)
    in_specs = [row(D_MODEL), _const_spec(g.shape), _const_spec(wp.shape), _const_spec(wr.shape),
                _const_spec(wsb.shape), _const_spec(ws5.shape),
                _const_spec(qg.shape), _const_spec(kg.shape)]
    args = [x, g, wp, wr, wsb, ws5, qg, kg]
    out_specs = [row(n) for n in widths]
    out_shape = [jax.ShapeDtypeStruct((m, n), F32) for n in widths]
    if sb_tails is not None:
        in_specs.append(_const_spec(sb_tails.shape))
        args.append(sb_tails)
        for n in (2 * HEAD_DIM, 2 * HEAD_DIM, HEAD_DIM):
            out_specs.append(pl.BlockSpec((N_HEADS, tm, n), lambda i: (0, i, 0)))
            out_shape.append(jax.ShapeDtypeStruct((N_HEADS, m, n), BF16))
    return pl.pallas_call(
        _in_proj_kernel,
        grid=(m // tm,),
        in_specs=in_specs,
        out_specs=out_specs,
        out_shape=out_shape,
        compiler_params=_params(("arbitrary",)),
        name="in_proj",
    )(*args)


def _pool_kernel(u_ref, buf_ref, w_ref, scale_ref, o_ref, new_ref, ext_ref, *, tt, pos0):
    t = pl.program_id(1)

    @pl.when(t == 0)
    def _():
        ext_ref[0:POOL_HDR, :] = buf_ref[0]

    @pl.when(t > 0)
    def _():
        ext_ref[0:POOL_HDR, :] = ext_ref[tt:tt + POOL_HDR, :]

    u = u_ref[0]
    ext_ref[POOL_HDR:POOL_HDR + tt, :] = u
    lane = lax.broadcasted_iota(jnp.int32, (tt, D_BR), 1)
    pos = pos0 + t * tt + lax.broadcasted_iota(jnp.int32, (tt, D_BR), 0)
    s = u
    win_sum = jnp.zeros((tt, D_BR), F32)
    count = jnp.zeros((tt, D_BR), F32)
    for i in range(1, POOL_WINDOWS[-1]):
        s = s + ext_ref[POOL_HDR - i:POOL_HDR - i + tt, :]
        w = i + 1
        if w in POOL_WINDOWS:
            gi = POOL_WINDOWS.index(w)
            sel = (lane >= gi * HEAD_DIM) & (lane < (gi + 1) * HEAD_DIM)
            win_sum = jnp.where(sel, s, win_sum)
            count = jnp.where(sel, jnp.minimum(pos + 1, w).astype(F32), count)
    pooled = win_sum / count - u
    o_ref[0] = _dot_hi(pooled, w_ref[...]) * scale_ref[...]
    new_ref[0] = ext_ref[tt + 1:tt + POOL_HDR, :]


def pool_mixer(u, buf, w_bd, scale, pos0):
    b, t, _ = u.shape
    tt = min(t, 512)
    kern = functools.partial(_pool_kernel, tt=tt, pos0=pos0)
    return pl.pallas_call(
        kern,
        grid=(b, t // tt),
        in_specs=[pl.BlockSpec((1, tt, D_BR), lambda i, j: (i, j, 0)),
                  pl.BlockSpec((1, POOL_HDR, D_BR), lambda i, j: (i, 0, 0)),
                  _const_spec(w_bd.shape), _const_spec(scale.shape)],
        out_specs=[pl.BlockSpec((1, tt, D_BR), lambda i, j: (i, j, 0)),
                   pl.BlockSpec((1, POOL_BUF, D_BR), lambda i, j: (i, 0, 0))],
        out_shape=[jax.ShapeDtypeStruct((b, t, D_BR), F32),
                   jax.ShapeDtypeStruct((b, POOL_BUF, D_BR), F32)],
        scratch_shapes=[pltpu.VMEM((POOL_HDR + tt, D_BR), F32)],
        compiler_params=_params(("arbitrary", "arbitrary")),
        name="pool_mixer",
    )(u, buf, w_bd, scale)


def _rwkv_pre_kernel(p_ref, shift_ref, mu_ref, w0_ref, wup_ref, a0_ref, aup_ref, gup_ref,
                     kk_ref, ka_ref,
                     r_out, k_out, v_out, ld_out, kk_out, b_out, g_out, li_out, last_ref, *, tt, c):
    t = pl.program_id(1)

    @pl.when(t == 0)
    def _():
        last_ref[...] = shift_ref[0]

    p = p_ref[0]
    rowi = lax.broadcasted_iota(jnp.int32, (tt, RWKV_COLS), 0)
    prev = jnp.where(rowi == 0, last_ref[...], pltpu.roll(p, 1, 0))
    last_ref[...] = p[tt - 1:tt, :]
    pm = p + (prev - p) * mu_ref[...]
    r = pm[:, 0:D_BR]
    k = pm[:, D_BR:2 * D_BR]
    v = pm[:, 2 * D_BR:3 * D_BR]
    w_lo = pm[:, 768:832]
    a_lo = pm[:, 832:896]
    g_lo = pm[:, 896:1024]
    wz = w0_ref[...] + _dot(jnp.tanh(w_lo), wup_ref[...])
    w_log = _log_sigmoid(wz) - 0.5
    a = _sigmoid(a0_ref[...] + _dot(a_lo, aup_ref[...]))
    g = _dot(_sigmoid(g_lo), gup_ref[...])
    kk = k * kk_ref[...]
    parts = []
    for h in range(N_HEADS):
        ks = kk[:, h * HEAD_DIM:(h + 1) * HEAD_DIM]
        ss = jnp.sum(ks * ks, axis=-1, keepdims=True)
        parts.append(ks * lax.rsqrt(jnp.maximum(ss, 1e-24)))
    kk = jnp.concatenate(parts, axis=-1)
    ld = -jnp.exp(w_log)
    rowc = lax.broadcasted_iota(jnp.int32, (tt, D_BR), 0) & (c - 1)
    l_inc = ld
    off = 1
    while off < c:
        l_inc = l_inc + jnp.where(rowc >= off, pltpu.roll(l_inc, off, 0), 0.0)
        off *= 2
    r_out[0] = r
    k_out[0] = k * (1.0 + (a - 1.0) * ka_ref[...])
    v_out[0] = v
    ld_out[0] = ld
    kk_out[0] = kk
    b_out[0] = kk * a
    g_out[0] = g
    li_out[0] = l_inc


def rwkv_pre(p, shift, mu, w0, wup, a0, aup, gup, k_k, k_a):
    b, t, _ = p.shape
    tt = min(t, 512)
    kern = functools.partial(_rwkv_pre_kernel, tt=tt, c=min(t, RWKV_CHUNK))
    blk = pl.BlockSpec((1, tt, D_BR), lambda i, j: (i, j, 0))
    consts = (mu, w0, wup, a0, aup, gup, k_k, k_a)
    return pl.pallas_call(
        kern,
        grid=(b, t // tt),
        in_specs=[pl.BlockSpec((1, tt, RWKV_COLS), lambda i, j: (i, j, 0)),
                  pl.BlockSpec((1, 1, RWKV_COLS), lambda i, j: (i, 0, 0))]
                 + [_const_spec(c.shape) for c in consts],
        out_specs=[blk] * 8,
        out_shape=[jax.ShapeDtypeStruct((b, t, D_BR), F32)] * 8,
        scratch_shapes=[pltpu.VMEM((1, RWKV_COLS), F32)],
        compiler_params=_params(("arbitrary", "arbitrary")),
        name="rwkv_pre",
    )(p, shift, *consts)


_BMM_DIMS = {
    'nn': (((2,), (1,)), ((0,), (0,))),
    'nt': (((2,), (2,)), ((0,), (0,))),
}


def _bmm(a, b, dims):
    return lax.dot_general(a.astype(BF16), b.astype(BF16), _BMM_DIMS[dims], preferred_element_type=F32)


def _rwkv_chunk_kernel(r_ref, k_ref, v_ref, ld_ref, kk_ref, b_ref, g_ref, li_ref, s0_ref,
                       lnw_ref, lnb_ref, rk_ref, o_ref, sout_ref, s_ref, *, tt, c):
    t = pl.program_id(1)

    @pl.when(t == 0)
    def _():
        s_ref[...] = s0_ref[0]

    nch = tt // c
    nb = nch * N_HEADS

    def heads(x):
        return jnp.stack([x[ch * c:(ch + 1) * c, h * HEAD_DIM:(h + 1) * HEAD_DIM]
                          for ch in range(nch) for h in range(N_HEADS)])

    def head_rows(ref):
        return jnp.stack([ref[:, h * HEAD_DIM:(h + 1) * HEAD_DIM] for h in range(N_HEADS)])

    ri = lax.broadcasted_iota(jnp.int32, (nb, c, c), 1)
    ci = lax.broadcasted_iota(jnp.int32, (nb, c, c), 2)
    incl = (ri >= ci).astype(F32)
    strict = (ri > ci).astype(F32)
    eye = (ri == ci).astype(F32)
    n_sq = int(round(math.log2(c))) - 1

    r_all = r_ref[0]
    k_all = k_ref[0]
    v_all = v_ref[0]
    b_all = b_ref[0]
    l_inc = li_ref[0]
    l_tot = jnp.concatenate(
        [jnp.broadcast_to(l_inc[(ch + 1) * c - 1:(ch + 1) * c, :], (c, D_BR)) for ch in range(nch)], axis=0)
    e_neg = jnp.exp(-l_inc)
    e_end = jnp.exp(l_tot - l_inc)
    a_t = heads(-kk_ref[0] * jnp.exp(l_inc - ld_ref[0]))
    r_t = heads(r_all * jnp.exp(l_inc))
    b_t = heads(b_all * e_neg)
    k_t = heads(k_all * e_neg)
    b_e = heads(b_all * e_end)
    k_e = heads(k_all * e_end)
    d_tot = heads(jnp.exp(l_tot))[:, 0:1, :]
    r = heads(r_all)
    k = heads(k_all)
    v = heads(v_all)

    gram = _bmm(jnp.concatenate([a_t, r_t], axis=1), jnp.concatenate([b_t, k_t], axis=1), 'nt')
    m_ab = gram[:, :c, :c] * strict
    m_ak = gram[:, :c, c:] * strict
    p_rb = gram[:, c:, :c] * incl
    p_rk = gram[:, c:, c:] * incl
    pw = m_ab
    tinv = eye + m_ab
    for _ in range(n_sq):
        pw = _bmm(pw, pw, 'nn')
        tinv = tinv + _bmm(tinv, pw, 'nn')
    wu = _bmm(tinv, jnp.concatenate([a_t, _bmm(m_ak, v, 'nn')], axis=2), 'nn')
    o0 = _bmm(p_rk, v, 'nn')
    uv_rhs = jnp.concatenate([b_e, k_e], axis=1)

    lnw = head_rows(lnw_ref)
    lnb = head_rows(lnb_ref)
    rk = head_rows(rk_ref)
    s = s_ref[...]
    for ch in range(nch):
        sl = slice(ch * N_HEADS, (ch + 1) * N_HEADS)
        y = _bmm(jnp.concatenate([wu[sl, :, :HEAD_DIM], r_t[sl]], axis=1), s, 'nt')
        u = y[:, :c] + wu[sl, :, HEAD_DIM:]
        o = y[:, c:] + _bmm(p_rb[sl], u, 'nn') + o0[sl]
        uv_t = jnp.swapaxes(jnp.concatenate([u, v[sl]], axis=1), 1, 2)
        s = s * d_tot[sl] + _bmm(uv_t, uv_rhs[sl], 'nn')
        mean = jnp.mean(o, axis=-1, keepdims=True)
        var = jnp.mean(jnp.square(o - mean), axis=-1, keepdims=True)
        o = (o - mean) * lax.rsqrt(var + RWKV_LN_EPS) * lnw + lnb
        o = o + jnp.sum(r[sl] * k[sl] * rk, axis=-1, keepdims=True) * v[sl]
        rows = slice(ch * c, (ch + 1) * c)
        o_ref[0, rows, :] = (jnp.concatenate([o[h] for h in range(N_HEADS)], axis=-1)
                             * g_ref[0, rows, :])
    s_ref[...] = s
    sout_ref[0] = s


def rwkv_chunk(r, k, v, ld, kk, bv, g, l_inc, s0, ln_w, ln_b, r_k):
    b, t, _ = r.shape
    c = min(t, RWKV_CHUNK)
    tt = min(t, 256)
    kern = functools.partial(_rwkv_chunk_kernel, tt=tt, c=c)
    blk = pl.BlockSpec((1, tt, D_BR), lambda i, j: (i, j, 0))
    sblk = pl.BlockSpec((1, N_HEADS, HEAD_DIM, HEAD_DIM), lambda i, j: (i, 0, 0, 0))
    return pl.pallas_call(
        kern,
        grid=(b, t // tt),
        in_specs=[blk] * 8 + [sblk, _const_spec(ln_w.shape), _const_spec(ln_b.shape),
                              _const_spec(r_k.shape)],
        out_specs=[blk, sblk],
        out_shape=[jax.ShapeDtypeStruct((b, t, D_BR), F32),
                   jax.ShapeDtypeStruct((b, N_HEADS, HEAD_DIM, HEAD_DIM), F32)],
        scratch_shapes=[pltpu.VMEM((N_HEADS, HEAD_DIM, HEAD_DIM), F32)],
        compiler_params=_params(("arbitrary", "arbitrary")),
        name="rwkv_chunk",
    )(r, k, v, ld, kk, bv, g, l_inc, s0, ln_w, ln_b, r_k)


SB_GROUP = 8


LOG2E = 1.4426950408889634


def _softplus2(z2):
    return jnp.maximum(z2, 0.0) + jnp.log2(1.0 + jnp.exp2(-jnp.abs(z2)))


def _sb_prompt_kernel(q_ref, k_ref, v_ref, o_ref, *, tq):
    i = pl.program_id(1)
    q = q_ref[0]
    ri = lax.broadcasted_iota(jnp.int32, (tq, tq), 0)
    ci = lax.broadcasted_iota(jnp.int32, (tq, tq), 1)
    later = (ri > ci).astype(BF16)

    def scores(j):
        start = pl.multiple_of(j * tq, tq)
        return _dot_nt(q, k_ref[0, pl.ds(start, tq), :]), v_ref[0, pl.ds(start, tq), :]

    def full_blocks(j, n, car, acc):
        zv = [scores(j - b) for b in range(n)]
        sps = [_softplus2(z) for z, _ in zv]
        afts = [jnp.dot(sp.astype(BF16), later, preferred_element_type=F32) for sp in sps]
        atts = [jnp.exp2(zv[b][0] - sps[b] - afts[b]).astype(BF16) for b in range(n)]
        for b in range(n):
            acc = acc + jnp.exp2(-car) * jnp.dot(atts[b], zv[b][1], preferred_element_type=F32)
            car = car + jnp.sum(sps[b], axis=-1, keepdims=True)
        return car, acc

    z, vb = scores(i)
    mask = ci < ri
    sp = jnp.where(mask, _softplus2(z), 0.0)
    aft = jnp.dot(sp.astype(BF16), later, preferred_element_type=F32)
    att = jnp.where(mask, jnp.exp2(z - sp - aft), 0.0)
    acc = jnp.dot(att.astype(BF16), vb, preferred_element_type=F32)
    car = jnp.sum(sp, axis=-1, keepdims=True)

    j = i - 1
    g = 1
    while g < SB_GROUP:
        n_g = lax.shift_right_logical(i, g.bit_length() - 1) & 1
        car, acc = lax.fori_loop(
            0, n_g, lambda p, ca, j=j, g=g: full_blocks(j, g, *ca), (car, acc))
        j = j - g * n_g
        g *= 2
    _, acc = lax.fori_loop(
        0, lax.shift_right_logical(i, SB_GROUP.bit_length() - 1),
        lambda p, ca: full_blocks(j - SB_GROUP * p, SB_GROUP, *ca), (car, acc))
    o_ref[0] = acc


def sb_prompt(q, k, v):
    h, t, _ = q.shape
    tq = min(t, 256)
    kern = functools.partial(_sb_prompt_kernel, tq=tq)
    return pl.pallas_call(
        kern,
        grid=(h, t // tq),
        in_specs=[pl.BlockSpec((1, tq, 2 * HEAD_DIM), lambda a, i: (a, i, 0)),
                  pl.BlockSpec((1, t, 2 * HEAD_DIM), lambda a, i: (a, 0, 0)),
                  pl.BlockSpec((1, t, HEAD_DIM), lambda a, i: (a, 0, 0))],
        out_specs=pl.BlockSpec((1, tq, HEAD_DIM), lambda a, i: (a, i, 0)),
        out_shape=jax.ShapeDtypeStruct((h, t, HEAD_DIM), F32),
        compiler_params=_params(("arbitrary", "arbitrary")),
        name="sb_prompt",
    )(q, k, v)


PAGES_PER_STEP = 32


def _sb_paged_kernel(pt_ref, q_ref, bias_ref, kn_ref, vn_ref, *rest, t_new, past_len):
    nk = PAGES_PER_STEP
    k_pages = rest[:nk]
    v_pages = rest[nk:2 * nk]
    o_ref = rest[2 * nk]
    acc_ref, car_ref = rest[2 * nk + 1:]
    s = pl.program_id(1)
    q = q_ref[0] * (LOG2E * HEAD_DIM ** -0.5)
    qh = [q[:, h * HEAD_DIM:(h + 1) * HEAD_DIM].astype(BF16) for h in range(N_HEADS)]
    bias = bias_ref[...]
    ri = lax.broadcasted_iota(jnp.int32, (PAGE_SIZE, PAGE_SIZE), 0)
    ci = lax.broadcasted_iota(jnp.int32, (PAGE_SIZE, PAGE_SIZE), 1)
    later = (ri > ci).astype(BF16)

    def head_rows(x):
        return [x[h * t_new:(h + 1) * t_new] for h in range(N_HEADS)]

    @pl.when(s == 0)
    def _():
        kn = kn_ref[0].astype(BF16)
        vn = vn_ref[0].astype(BF16)
        cols = [slice(h * HEAD_DIM, (h + 1) * HEAD_DIM) for h in range(N_HEADS)]
        z = jnp.concatenate([_dot_nt(qh[h], kn[:, cols[h]]) for h in range(N_HEADS)], axis=0) + bias
        qi = jnp.concatenate([lax.broadcasted_iota(jnp.int32, (t_new, PAGE_SIZE), 0)] * N_HEADS, axis=0)
        mask = lax.broadcasted_iota(jnp.int32, (N_HEADS * t_new, PAGE_SIZE), 1) < qi
        sp = jnp.where(mask, _softplus2(z), 0.0)
        aft = jnp.dot(sp.astype(BF16), later, preferred_element_type=F32)
        att = head_rows(jnp.where(mask, jnp.exp2(z - sp - aft), 0.0).astype(BF16))
        acc_ref[...] = jnp.concatenate(
            [jnp.dot(att[h], vn[:, cols[h]], preferred_element_type=F32) for h in range(N_HEADS)], axis=0)
        car_ref[...] = jnp.sum(sp, axis=-1, keepdims=True)

    def gather_head(pages, h):
        return jnp.concatenate([p[0, 0, h] for p in pages], axis=1).astype(BF16)

    z = jnp.concatenate(
        [jnp.dot(qh[h], gather_head(k_pages, h), preferred_element_type=F32) for h in range(N_HEADS)],
        axis=0) + bias
    sp = _softplus2(z)
    car = car_ref[...]
    afts = []
    for rr in range(nk):
        sp_rr = sp[:, rr * PAGE_SIZE:(rr + 1) * PAGE_SIZE]
        afts.append(jnp.dot(sp_rr.astype(BF16), later, preferred_element_type=F32) + car)
        car = car + jnp.sum(sp_rr, axis=-1, keepdims=True)
    att = head_rows(jnp.exp2(z - sp - jnp.concatenate(afts, axis=1)).astype(BF16))
    acc_ref[...] += jnp.concatenate(
        [_dot_nt(att[h], gather_head(v_pages, h)) for h in range(N_HEADS)], axis=0)
    car_ref[...] = car

    @pl.when(s == pl.num_programs(1) - 1)
    def _():
        o_ref[0] = jnp.concatenate(head_rows(acc_ref[...]), axis=-1)


def sb_paged(q, bias_rows, k_new, v_new, cache_k, cache_v, page_table, layer):
    b, t_new, _ = q.shape
    n_pages = page_table.shape[1]
    past_len = n_pages * PAGE_SIZE
    nk = PAGES_PER_STEP
    n_steps = n_pages // nk
    kern = functools.partial(_sb_paged_kernel, t_new=t_new, past_len=past_len)

    def page_spec(rr):
        return pl.BlockSpec(
            (1, 1, N_HEADS, HEAD_DIM, PAGE_SIZE),
            lambda i, s, pt: (layer, pt[i, n_pages - 1 - (s * nk + rr)], 0, 0, 0))

    n_rows = N_HEADS * t_new
    grid_spec = pltpu.PrefetchScalarGridSpec(
        num_scalar_prefetch=1,
        grid=(b, n_steps),
        in_specs=[pl.BlockSpec((1, t_new, D_BR), lambda i, s, pt: (i, 0, 0)),
                  pl.BlockSpec((n_rows, 1), lambda i, s, pt: (0, 0)),
                  pl.BlockSpec((1, PAGE_SIZE, D_BR), lambda i, s, pt: (i, 0, 0)),
                  pl.BlockSpec((1, PAGE_SIZE, D_BR), lambda i, s, pt: (i, 0, 0))]
                 + [page_spec(rr) for rr in range(nk)] * 2,
        out_specs=pl.BlockSpec((1, t_new, D_BR), lambda i, s, pt: (i, 0, 0)),
        scratch_shapes=[pltpu.VMEM((n_rows, HEAD_DIM), F32), pltpu.VMEM((n_rows, 1), F32)],
    )
    return pl.pallas_call(
        kern,
        grid_spec=grid_spec,
        out_shape=jax.ShapeDtypeStruct((b, t_new, D_BR), F32),
        compiler_params=_params(("arbitrary", "arbitrary")),
        name="sb_paged",
    )(page_table, q, bias_rows, k_new, v_new, *([cache_k] * nk), *([cache_v] * nk))


def _s5_kernel(u_ref, hre_ref, him_ref, bre_ref, bim_ref, cre_ref, cim_ref, d_ref, wglu_ref,
               powr_ref, powi_ref, lvr_ref, lvi_ref,
               o_ref, sre_ref, sim_ref, cr_ref, cim_s_ref, *, tt):
    t = pl.program_id(1)

    @pl.when(t == 0)
    def _():
        cr_ref[...] = hre_ref[0]
        cim_s_ref[...] = him_ref[0]

    u = u_ref[0]
    xr = _dot(u, bre_ref[...])
    xi = _dot(u, bim_ref[...])
    row_in_group = lax.broadcasted_iota(jnp.int32, (tt, S5_N), 0) & 7
    for lvl, off in enumerate((1, 2, 4)):
        lr = lvr_ref[lvl:lvl + 1, :]
        li = lvi_ref[lvl:lvl + 1, :]
        m = row_in_group >= off
        sr = jnp.where(m, pltpu.roll(xr, off, 0), 0.0)
        si = jnp.where(m, pltpu.roll(xi, off, 0), 0.0)
        xr, xi = xr + lr * sr - li * si, xi + lr * si + li * sr
    pr = powr_ref[0:8, :]
    pi = powi_ref[0:8, :]
    hr = cr_ref[...]
    hi = cim_s_ref[...]
    groups_r, groups_i = [], []
    for g in range(tt // 8):
        gr = xr[g * 8:(g + 1) * 8, :]
        gi = xi[g * 8:(g + 1) * 8, :]
        gr, gi = gr + pr * hr - pi * hi, gi + pr * hi + pi * hr
        groups_r.append(gr)
        groups_i.append(gi)
        hr = gr[7:8, :]
        hi = gi[7:8, :]
    xr = jnp.concatenate(groups_r, axis=0)
    xi = jnp.concatenate(groups_i, axis=0)
    cr_ref[...] = xr[tt - 1:tt, :]
    cim_s_ref[...] = xi[tt - 1:tt, :]
    sre_ref[0] = xr[tt - 1:tt, :]
    sim_ref[0] = xi[tt - 1:tt, :]
    y = _dot(xr, cre_ref[...]) - _dot(xi, cim_ref[...]) + d_ref[...] * u
    z = _dot(jax.nn.gelu(y), wglu_ref[...])
    o_ref[0] = z[:, :D_BR] * _sigmoid(z[:, D_BR:])


def s5_mixer(u, h_re, h_im, b_re, b_im, c_re, c_im, d, w_glu, pow_re, pow_im, lv_re, lv_im):
    b, t, _ = u.shape
    tt = min(t, 256)
    kern = functools.partial(_s5_kernel, tt=tt)
    consts = (b_re, b_im, c_re, c_im, d, w_glu, pow_re, pow_im, lv_re, lv_im)
    sblk = pl.BlockSpec((1, 1, S5_N), lambda i, j: (i, 0, 0))
    return pl.pallas_call(
        kern,
        grid=(b, t // tt),
        in_specs=[pl.BlockSpec((1, tt, D_BR), lambda i, j: (i, j, 0)), sblk, sblk]
                 + [_const_spec(c.shape) for c in consts],
        out_specs=[pl.BlockSpec((1, tt, D_BR), lambda i, j: (i, j, 0)), sblk, sblk],
        out_shape=[jax.ShapeDtypeStruct((b, t, D_BR), F32),
                   jax.ShapeDtypeStruct((b, 1, S5_N), F32),
                   jax.ShapeDtypeStruct((b, 1, S5_N), F32)],
        scratch_shapes=[pltpu.VMEM((1, S5_N), F32), pltpu.VMEM((1, S5_N), F32)],
        compiler_params=_params(("arbitrary", "arbitrary")),
        name="s5_mixer",
    )(u, h_re, h_im, *consts)


def _merge_kernel(x_ref, g_ref, wg_ref, bp_ref, br_ref, bs_ref, b5_ref, wb_ref, wo_ref, o_ref):
    if len(bs_ref.shape) == 3:
        o_sb = jnp.concatenate([bs_ref[h] for h in range(N_HEADS)], axis=-1)
    else:
        o_sb = bs_ref[...]
    x = x_ref[...]
    h = _rms(x, g_ref[...]).astype(BF16)
    merged = None
    for n, br in enumerate((bp_ref[...], br_ref[...], o_sb, b5_ref[...])):
        lifted = _dot(br, wb_ref[n])
        gate = jnp.dot(h, wg_ref[:, n * D_MODEL:(n + 1) * D_MODEL], preferred_element_type=F32)
        term = _sigmoid(gate) * lifted
        merged = term if merged is None else merged + term
    o_ref[...] = x + _dot(merged, wo_ref[...])


def merge(x, g, w_gate, o_pool, o_rwkv, o_sb, o_s5, w_branch, w_out):
    m = x.shape[0]
    tm = min(m, 256)
    row = lambda n: pl.BlockSpec((tm, n), lambda i: (i, 0))
    sb_spec = row(D_BR) if o_sb.ndim == 2 else pl.BlockSpec((N_HEADS, tm, HEAD_DIM), lambda i: (0, i, 0))
    return pl.pallas_call(
        _merge_kernel,
        grid=(m // tm,),
        in_specs=[row(D_MODEL), _const_spec(g.shape), _const_spec(w_gate.shape),
                  row(D_BR), row(D_BR), sb_spec, row(D_BR),
                  _const_spec(w_branch.shape), _const_spec(w_out.shape)],
        out_specs=row(D_MODEL),
        out_shape=jax.ShapeDtypeStruct((m, D_MODEL), F32),
        compiler_params=_params(("arbitrary",)),
        name="merge",
    )(x, g, w_gate, o_pool, o_rwkv, o_sb, o_s5, w_branch, w_out)


def _xattn_kernel(x_ref, mk_ref, mv_ref, g_ref, wq_ref, qg_ref, wo_ref, o_ref):
    x = x_ref[0]
    hn = _rms(x, g_ref[...])
    q = _head_rms(_dot(hn, wq_ref[...]), qg_ref[...]) * HEAD_DIM ** -0.5
    outs = []
    for h in range(N_HEADS):
        cols = slice(h * HEAD_DIM, (h + 1) * HEAD_DIM)
        s = _dot(q[:, cols], mk_ref[0, 0, h])
        s = s - jnp.max(s, axis=-1, keepdims=True)
        e = jnp.exp(s)
        pr = e / jnp.sum(e, axis=-1, keepdims=True)
        outs.append(_dot_nt(pr.astype(BF16), mv_ref[0, 0, h].astype(BF16)))
    o = jnp.concatenate(outs, axis=-1)
    o_ref[0] = x + _dot(o, wo_ref[...])


def cross_attention(x, mem_k, mem_v, layer, g, wq, qg, wo):
    b, t, _ = x.shape
    tt = min(t, 512)
    mem_spec = pl.BlockSpec((1, 1, N_HEADS, HEAD_DIM, N_MEM), lambda i, j: (layer, i, 0, 0, 0))
    return pl.pallas_call(
        _xattn_kernel,
        grid=(b, t // tt),
        in_specs=[pl.BlockSpec((1, tt, D_MODEL), lambda i, j: (i, j, 0)), mem_spec, mem_spec,
                  _const_spec(g.shape), _const_spec(wq.shape), _const_spec(qg.shape),
                  _const_spec(wo.shape)],
        out_specs=pl.BlockSpec((1, tt, D_MODEL), lambda i, j: (i, j, 0)),
        out_shape=jax.ShapeDtypeStruct((b, t, D_MODEL), F32),
        compiler_params=_params(("arbitrary", "arbitrary")),
        name="cross_attention",
    )(x, mem_k, mem_v, g, wq, qg, wo)


def _xattn_batched_kernel(x_ref, mk_ref, mv_ref, g_ref, wq_ref, qg_ref, wo_ref, o_ref, *, b, t):
    x = x_ref[...]
    hn = _rms(x, g_ref[...])
    q = (_head_rms(_dot(hn, wq_ref[...]), qg_ref[...]) * HEAD_DIM ** -0.5).astype(BF16)
    cols = [slice(h * HEAD_DIM, (h + 1) * HEAD_DIM) for h in range(N_HEADS)]
    s = jnp.concatenate(
        [jnp.dot(q[bi * t:(bi + 1) * t, cols[h]], mk_ref[0, bi, h].astype(BF16), preferred_element_type=F32)
         for bi in range(b) for h in range(N_HEADS)], axis=0)
    s = s - jnp.max(s, axis=-1, keepdims=True)
    e = jnp.exp(s)
    pr = (e / jnp.sum(e, axis=-1, keepdims=True)).astype(BF16)
    o = jnp.concatenate(
        [jnp.concatenate(
            [_dot_nt(pr[(bi * N_HEADS + h) * t:(bi * N_HEADS + h + 1) * t], mv_ref[0, bi, h].astype(BF16))
             for h in range(N_HEADS)], axis=-1)
         for bi in range(b)], axis=0)
    o_ref[...] = x + _dot(o, wo_ref[...])


def cross_attention_batched(x, mem_k, mem_v, layer, g, wq, qg, wo):
    b, t, _ = x.shape
    m = b * t
    kern = functools.partial(_xattn_batched_kernel, b=b, t=t)
    mem_spec = pl.BlockSpec((1, b, N_HEADS, HEAD_DIM, N_MEM), lambda i: (layer, 0, 0, 0, 0))
    out = pl.pallas_call(
        kern,
        grid=(1,),
        in_specs=[_const_spec((m, D_MODEL)), mem_spec, mem_spec, _const_spec(g.shape),
                  _const_spec(wq.shape), _const_spec(qg.shape), _const_spec(wo.shape)],
        out_specs=_const_spec((m, D_MODEL)),
        out_shape=jax.ShapeDtypeStruct((m, D_MODEL), F32),
        compiler_params=_params(("arbitrary",)),
        name="cross_attention_batched",
    )(x.reshape(m, D_MODEL), mem_k, mem_v, g, wq, qg, wo)
    return out.reshape(b, t, D_MODEL)


def _memkv_kernel(mem_ref, g_ref, wk_ref, wv_ref, kg_ref, k_ref, v_ref, kt_ref, vt_ref):
    mn = _rms(mem_ref[...], g_ref[...])
    k = _head_rms(_dot(mn, wk_ref[...]), kg_ref[...])
    v = _dot(mn, wv_ref[...])
    k_ref[...] = k
    v_ref[...] = v
    kt_ref[...] = k.T
    vt_ref[...] = v.T


def memory_kv(mem, g, wk, wv, kg):
    m = mem.shape[0]
    args = (mem, g, wk, wv, kg)
    return pl.pallas_call(
        _memkv_kernel,
        grid=(1,),
        in_specs=[_const_spec(a.shape) for a in args],
        out_specs=[_const_spec((m, D_BR))] * 2 + [_const_spec((D_BR, m))] * 2,
        out_shape=[jax.ShapeDtypeStruct((m, D_BR), F32)] * 2 + [jax.ShapeDtypeStruct((D_BR, m), F32)] * 2,
        compiler_params=_params(("arbitrary",)),
        name="memory_kv",
    )(*args)


def _ffn_kernel(x_ref, g_ref, wg_ref, wu_ref, wd_ref, o_ref):
    x = x_ref[...]
    hn = _rms(x, g_ref[...]).astype(BF16)
    a = jnp.dot(hn, wg_ref[...], preferred_element_type=F32)
    bq = jnp.dot(hn, wu_ref[...], preferred_element_type=F32)
    act = a * _sigmoid(a) * bq
    o_ref[...] = x + _dot(act, wd_ref[...])


def ffn(x, g, w_gate, w_up, w_down):
    m = x.shape[0]
    tm = min(m, 256)
    row = pl.BlockSpec((tm, D_MODEL), lambda i: (i, 0))
    return pl.pallas_call(
        _ffn_kernel,
        grid=(m // tm,),
        in_specs=[row, _const_spec(g.shape), _const_spec(w_gate.shape), _const_spec(w_up.shape),
                  _const_spec(w_down.shape)],
        out_specs=row,
        out_shape=jax.ShapeDtypeStruct((m, D_MODEL), F32),
        compiler_params=_params(("arbitrary",)),
        name="ffn",
    )(x, g, w_gate, w_up, w_down)


def _block_diag(blocks):
    g, m, n = blocks.shape
    eye = jnp.eye(g, dtype=blocks.dtype)
    return (eye[:, None, :, None] * blocks[:, :, None, :]).reshape(g * m, g * n)


def _tile_heads(v):
    return jnp.tile(v, N_HEADS).reshape(1, N_HEADS * v.shape[0])


def _cmul(ar, ai, br, bi):
    return ar * br - ai * bi, ar * bi + ai * br


def _s5_constants(a_re, a_im, log_dt, b_re, b_im, c_re, c_im, tt):
    dt_g = jnp.exp(log_dt)[:, None]
    mag = jnp.exp(dt_g * a_re)
    ab_re, ab_im = mag * jnp.cos(dt_g * a_im), mag * jnp.sin(dt_g * a_im)
    den = a_re * a_re + a_im * a_im
    n_re = ab_re - 1.0
    f_re = (n_re * a_re + ab_im * a_im) / den
    f_im = (ab_im * a_re - n_re * a_im) / den
    bb_re = f_re[..., None] * b_re - f_im[..., None] * b_im
    bb_im = f_re[..., None] * b_im + f_im[..., None] * b_re
    bmat_re = _block_diag(jnp.swapaxes(bb_re, 1, 2))
    bmat_im = _block_diag(jnp.swapaxes(bb_im, 1, 2))
    cmat_re = _block_diag(jnp.swapaxes(c_re, 1, 2))
    cmat_im = _block_diag(jnp.swapaxes(c_im, 1, 2))
    lam_re = ab_re.reshape(1, S5_N)
    lam_im = ab_im.reshape(1, S5_N)
    pow_re, pow_im = lam_re, lam_im
    lv_re, lv_im = [lam_re], [lam_im]
    cur_re, cur_im = lam_re, lam_im
    n = 1
    while n < tt:
        nr, ni = _cmul(pow_re, pow_im, cur_re, cur_im)
        pow_re = jnp.concatenate([pow_re, nr], axis=0)
        pow_im = jnp.concatenate([pow_im, ni], axis=0)
        cur_re, cur_im = _cmul(cur_re, cur_im, cur_re, cur_im)
        lv_re.append(cur_re)
        lv_im.append(cur_im)
        n *= 2
    n_lv = max(len(lv_re) - 1, 1)
    lv_re = jnp.concatenate(lv_re[:n_lv], axis=0)
    lv_im = jnp.concatenate(lv_im[:n_lv], axis=0)
    return (bmat_re.astype(BF16), bmat_im.astype(BF16), cmat_re.astype(BF16), cmat_im.astype(BF16),
            pow_re, pow_im, lv_re, lv_im)


def _layer_weights(l, w, tt_prompt, tt_sample):
    w_in = w['w_in'][l]
    lw = {
        'norm_mix': w['norm_mix'][l].reshape(1, D_MODEL),
        'norm_cross': w['norm_cross'][l].reshape(1, D_MODEL),
        'norm_mem': w['norm_mem'][l].reshape(1, D_MODEL),
        'norm_ffn': w['norm_ffn'][l].reshape(1, D_MODEL),
        'w_pool_in': w_in[:, :OFF_RWKV].astype(BF16),
        'w_rwkv_in': w_in[:, OFF_RWKV:OFF_SB].astype(BF16),
        'w_sb_in': w_in[:, OFF_SB:OFF_S5].astype(BF16),
        'w_s5_in': w_in[:, OFF_S5:OFF_GATE].astype(BF16),
        'w_gate_in': w_in[:, OFF_GATE:].astype(BF16),
        'sb_q_norm': _tile_heads(w['sb_q_norm'][l]),
        'sb_k_norm': _tile_heads(w['sb_k_norm'][l]),
        'sb_bias': w['sb_bias'][l],
        'pool_w': _block_diag(w['pool_w'][l]),
        'pool_scale': w['pool_scale'][l].reshape(1, D_BR),
        'rwkv_mu': w['rwkv_mu'][l].reshape(1, RWKV_COLS),
        'rwkv_w0': w['rwkv_w0'][l].reshape(1, D_BR),
        'rwkv_w_up': w['rwkv_w_up'][l].astype(BF16),
        'rwkv_a0': w['rwkv_a0'][l].reshape(1, D_BR),
        'rwkv_a_up': w['rwkv_a_up'][l].astype(BF16),
        'rwkv_g_up': w['rwkv_g_up'][l].astype(BF16),
        'rwkv_k_k': w['rwkv_k_k'][l].reshape(1, D_BR),
        'rwkv_k_a': w['rwkv_k_a'][l].reshape(1, D_BR),
        'rwkv_r_k': w['rwkv_r_k'][l].reshape(1, D_BR),
        'rwkv_ln_w': w['rwkv_ln_w'][l].reshape(1, D_BR),
        'rwkv_ln_b': w['rwkv_ln_b'][l].reshape(1, D_BR),
        's5_d': w['s5_d'][l].reshape(1, D_BR),
        's5_w_glu': w['s5_w_glu'][l].astype(BF16),
        'w_branch': w['w_branch'][l].astype(BF16),
        'w_out': w['w_out'][l].astype(BF16),
        'xa_w_q': w['xa_w_q'][l].astype(BF16),
        'xa_w_k': w['xa_w_k'][l].astype(BF16),
        'xa_w_v': w['xa_w_v'][l].astype(BF16),
        'xa_q_norm': _tile_heads(w['xa_q_norm'][l]),
        'xa_k_norm': _tile_heads(w['xa_k_norm'][l]),
        'xa_w_o': w['xa_w_o'][l].astype(BF16),
        'ffn_w_gate': w['ffn_w_gate'][l].astype(BF16),
        'ffn_w_up': w['ffn_w_up'][l].astype(BF16),
        'ffn_w_down': w['ffn_w_down'][l].astype(BF16),
    }
    b2 = w['sb_bias'][l] * LOG2E
    p1 = b2.astype(BF16).astype(F32)
    p2 = (b2 - p1).astype(BF16).astype(F32)
    p3 = (b2 - p1 - p2).astype(BF16).astype(F32)
    k_tails = jnp.pad(jnp.stack([p1, p2, p3], axis=1), ((0, 0), (0, HEAD_DIM - 3)))
    q_tail = jnp.pad(jnp.ones((1, 3), F32), ((0, 0), (0, HEAD_DIM - 3)))
    lw['sb_tails'] = jnp.concatenate([k_tails, q_tail], axis=0)
    s5_args = (w['s5_a_re'][l], w['s5_a_im'][l], w['s5_log_dt'][l], w['s5_b_re'][l], w['s5_b_im'][l],
               w['s5_c_re'][l], w['s5_c_im'][l])
    lw['s5_prompt'] = lw['s5_sample'] = _s5_constants(*s5_args, 8)
    return lw


def _trunk_layer(x, pos0, mem_k, mem_v, mem_layer, sb_fn, pool_buf, shift, wkv, s5_re, s5_im, lw, s5c,
                 fresh):
    b, t, _ = x.shape
    m = b * t
    xf = x.reshape(m, D_MODEL)
    u_pool, p_rwkv, q, k, v, u_s5, *head_major = in_proj(
        xf, lw['norm_mix'], lw['w_pool_in'], lw['w_rwkv_in'], lw['w_sb_in'], lw['w_s5_in'],
        lw['sb_q_norm'], lw['sb_k_norm'], lw['sb_tails'] if fresh else None)
    o_pool, new_pool = pool_mixer(u_pool.reshape(b, t, D_BR), pool_buf, lw['pool_w'],
                                  lw['pool_scale'], pos0)
    p3 = p_rwkv.reshape(b, t, RWKV_COLS)
    pre = rwkv_pre(p3, shift, lw['rwkv_mu'], lw['rwkv_w0'], lw['rwkv_w_up'], lw['rwkv_a0'],
                   lw['rwkv_a_up'], lw['rwkv_g_up'], lw['rwkv_k_k'], lw['rwkv_k_a'])
    o_rwkv, new_wkv = rwkv_chunk(*pre, wkv, lw['rwkv_ln_w'], lw['rwkv_ln_b'], lw['rwkv_r_k'])
    new_shift = p3[:, -1]
    q3, k3, v3 = (z.reshape(b, t, D_BR) for z in (q, k, v))
    o_sb = sb_fn(*head_major) if fresh else sb_fn(q3, k3, v3).reshape(m, D_BR)
    o_s5, new_re, new_im = s5_mixer(u_s5.reshape(b, t, D_BR), s5_re, s5_im, s5c[0], s5c[1], s5c[2],
                                    s5c[3], lw['s5_d'], lw['s5_w_glu'], *s5c[4:])
    x1 = merge(xf, lw['norm_mix'], lw['w_gate_in'], o_pool.reshape(m, D_BR), o_rwkv.reshape(m, D_BR), o_sb,
               o_s5.reshape(m, D_BR), lw['w_branch'], lw['w_out'])
    xattn = cross_attention if fresh else cross_attention_batched
    x2 = xattn(x1.reshape(b, t, D_MODEL), mem_k, mem_v, mem_layer, lw['norm_cross'],
               lw['xa_w_q'], lw['xa_q_norm'], lw['xa_w_o'])
    x3 = ffn(x2.reshape(m, D_MODEL), lw['norm_ffn'], lw['ffn_w_gate'], lw['ffn_w_up'],
             lw['ffn_w_down'])
    return x3.reshape(b, t, D_MODEL), (k3, v3, new_pool, new_shift, new_wkv, new_re, new_im)


def kernel(x_prompt, x_sample, cache_sb_k, cache_sb_v, cache_mem_k, cache_mem_v, state_pool, state_rwkv_shift, state_rwkv_wkv, state_s5_re, state_s5_im, page_table, mem_prompt, norm_mix, norm_cross, norm_mem, norm_ffn, w_in, pool_w, pool_scale, rwkv_mu, rwkv_w0, rwkv_w_up, rwkv_a0, rwkv_a_up, rwkv_g_up, rwkv_k_k, rwkv_k_a, rwkv_r_k, rwkv_ln_w, rwkv_ln_b, sb_q_norm, sb_k_norm, sb_bias, s5_a_re, s5_a_im, s5_log_dt, s5_b_re, s5_b_im, s5_c_re, s5_c_im, s5_d, s5_w_glu, w_branch, w_out, xa_w_q, xa_w_k, xa_w_v, xa_q_norm, xa_k_norm, xa_w_o, ffn_w_gate, ffn_w_up, ffn_w_down):
    weights = dict(
        norm_mix=norm_mix, norm_cross=norm_cross, norm_mem=norm_mem, norm_ffn=norm_ffn, w_in=w_in,
        pool_w=pool_w, pool_scale=pool_scale, rwkv_mu=rwkv_mu, rwkv_w0=rwkv_w0, rwkv_w_up=rwkv_w_up,
        rwkv_a0=rwkv_a0, rwkv_a_up=rwkv_a_up, rwkv_g_up=rwkv_g_up, rwkv_k_k=rwkv_k_k,
        rwkv_k_a=rwkv_k_a, rwkv_r_k=rwkv_r_k, rwkv_ln_w=rwkv_ln_w, rwkv_ln_b=rwkv_ln_b,
        sb_q_norm=sb_q_norm, sb_k_norm=sb_k_norm, sb_bias=sb_bias, s5_a_re=s5_a_re, s5_a_im=s5_a_im,
        s5_log_dt=s5_log_dt, s5_b_re=s5_b_re, s5_b_im=s5_b_im, s5_c_re=s5_c_re, s5_c_im=s5_c_im,
        s5_d=s5_d, s5_w_glu=s5_w_glu, w_branch=w_branch, w_out=w_out, xa_w_q=xa_w_q, xa_w_k=xa_w_k,
        xa_w_v=xa_w_v, xa_q_norm=xa_q_norm, xa_k_norm=xa_k_norm, xa_w_o=xa_w_o,
        ffn_w_gate=ffn_w_gate, ffn_w_up=ffn_w_up, ffn_w_down=ffn_w_down)
    depth = w_in.shape[0]
    bp, tp, _ = x_prompt.shape
    bs, ts, _ = x_sample.shape
    n_pool = cache_sb_k.shape[1]
    cache_k = jnp.transpose(cache_sb_k, (0, 1, 3, 4, 2))
    cache_v = jnp.transpose(cache_sb_v, (0, 1, 3, 4, 2))
    mem_k_t = jnp.transpose(cache_mem_k, (0, 1, 3, 4, 2))
    mem_v_t = jnp.transpose(cache_mem_v, (0, 1, 3, 4, 2))
    past_len = page_table.shape[1] * PAGE_SIZE
    tt_s5_p = min(tp, 256)
    tt_s5_s = ts

    xp, xs = x_prompt, x_sample
    outs_p = [[] for _ in range(9)]
    outs_s = [[] for _ in range(7)]
    for l in range(depth):
        lw = _layer_weights(l, weights, tt_s5_p, tt_s5_s)
        mk_p, mv_p, mkt_p, mvt_p = memory_kv(mem_prompt.reshape(bp * N_MEM, D_MODEL), lw['norm_mem'],
                                             lw['xa_w_k'], lw['xa_w_v'], lw['xa_k_norm'])
        mem_t_shape = (1, bp, N_HEADS, HEAD_DIM, N_MEM)

        xp, st = _trunk_layer(
            xp, 0, mkt_p.reshape(mem_t_shape), mvt_p.reshape(mem_t_shape), 0, sb_prompt,
            jnp.zeros((bp, POOL_HDR, D_BR), F32), jnp.zeros((bp, 1, RWKV_COLS), F32),
            jnp.zeros((bp, N_HEADS, HEAD_DIM, HEAD_DIM), F32), jnp.zeros((bp, 1, S5_N), F32),
            jnp.zeros((bp, 1, S5_N), F32), lw, lw['s5_prompt'], True)
        for lst, val in zip(outs_p, (st[0], st[1], mk_p, mv_p) + st[2:]):
            lst.append(val)

        bias_rows = jnp.repeat(lw['sb_bias'] * LOG2E, ts).reshape(N_HEADS * ts, 1)

        def sb_sample_fn(q, k, v, l=l, bias_rows=bias_rows):
            pad = ((0, 0), (0, PAGE_SIZE - ts), (0, 0))
            return sb_paged(q, bias_rows, jnp.pad(k, pad), jnp.pad(v, pad), cache_k, cache_v,
                            page_table, l)

        pool_buf = jnp.pad(state_pool[l], ((0, 0), (1, 0), (0, 0)))
        xs, st = _trunk_layer(
            xs, past_len, mem_k_t, mem_v_t, l, sb_sample_fn,
            pool_buf, state_rwkv_shift[l].reshape(bs, 1, RWKV_COLS), state_rwkv_wkv[l],
            state_s5_re[l].reshape(bs, 1, S5_N), state_s5_im[l].reshape(bs, 1, S5_N),
            lw, lw['s5_sample'], False)
        for lst, val in zip(outs_s, st):
            lst.append(val)

    def stack(lst, shape):
        return jnp.stack(lst, 0).reshape((depth,) + shape)

    kv_p = (bp, tp, N_HEADS, HEAD_DIM)
    kv_s = (bs, ts, N_HEADS, HEAD_DIM)
    mem_shape = (bp, N_MEM, N_HEADS, HEAD_DIM)
    return (xp, xs,
            stack(outs_p[0], kv_p), stack(outs_p[1], kv_p),
            stack(outs_p[2], mem_shape), stack(outs_p[3], mem_shape),
            stack(outs_p[4], (bp, POOL_BUF, D_BR)), stack(outs_p[5], (bp, RWKV_COLS)),
            stack(outs_p[6], (bp, N_HEADS, HEAD_DIM, HEAD_DIM)),
            stack(outs_p[7], (bp, S5_GROUPS, S5_STATE)), stack(outs_p[8], (bp, S5_GROUPS, S5_STATE)),
            stack(outs_s[0], kv_s), stack(outs_s[1], kv_s),
            stack(outs_s[2], (bs, POOL_BUF, D_BR)), stack(outs_s[3], (bs, RWKV_COLS)),
            stack(outs_s[4], (bs, N_HEADS, HEAD_DIM, HEAD_DIM)),
            stack(outs_s[5], (bs, S5_GROUPS, S5_STATE)), stack(outs_s[6], (bs, S5_GROUPS, S5_STATE)))
```

```python
import functools
import math

import jax
import jax.numpy as jnp
from jax import lax
from jax.experimental import pallas as pl
from jax.experimental.pallas import tpu as pltpu

F32 = jnp.float32
BF16 = jnp.bfloat16
HIGHEST = lax.Precision.HIGHEST

D_MODEL = 1024
D_BR = 256
HEAD_DIM = 64
N_HEADS = 4
POOL_WINDOWS = (2, 4, 8, 16)
POOL_BUF = 15
POOL_HDR = 16
RWKV_COLS = 1024
RWKV_LN_EPS = 64e-5
S5_GROUPS = 16
S5_STATE = 64
S5_N = S5_GROUPS * S5_STATE
RWKV_CHUNK = 64
PAGE_SIZE = 128
N_MEM = 256
D_FF = 2816
RMS_EPS = 1e-6
OFF_RWKV = 256
OFF_SB = 1280
OFF_S5 = 2048
OFF_GATE = 2304

VMEM_LIMIT = 56 * 1024 * 1024


def _params(sem):
    return pltpu.CompilerParams(dimension_semantics=sem, vmem_limit_bytes=VMEM_LIMIT)


def _dot(a, b):
    return jnp.dot(a.astype(BF16), b.astype(BF16), preferred_element_type=F32)


def _dot_hi(a, b):
    return jnp.dot(a, b, precision=HIGHEST, preferred_element_type=F32)


def _dot_nt(a, b, precision=None):
    return lax.dot_general(a, b, (((1,), (1,)), ((), ())), precision=precision,
                           preferred_element_type=F32)


def _dot_tn(a, b, precision=None):
    return lax.dot_general(a, b, (((0,), (0,)), ((), ())), precision=precision,
                           preferred_element_type=F32)


def _rms(x, g):
    ms = jnp.mean(x * x, axis=-1, keepdims=True)
    return x * lax.rsqrt(ms + RMS_EPS) * g


def _head_rms(x, gain_row):
    parts = []
    for h in range(N_HEADS):
        xs = x[:, h * HEAD_DIM:(h + 1) * HEAD_DIM]
        ms = jnp.mean(xs * xs, axis=-1, keepdims=True)
        parts.append(xs * lax.rsqrt(ms + RMS_EPS))
    return jnp.concatenate(parts, axis=-1) * gain_row


def _sigmoid(x):
    return 1.0 / (1.0 + jnp.exp(-x))


def _log_sigmoid(z):
    return jnp.minimum(z, 0.0) - jnp.log1p(jnp.exp(-jnp.abs(z)))


def _const_spec(shape):
    nd = len(shape)
    return pl.BlockSpec(shape, lambda *_: (0,) * nd)


def _in_proj_kernel(x_ref, g_ref, wp_ref, wr_ref, wsb_ref, ws5_ref, wg_ref, qg_ref, kg_ref, *refs):
    head_major_refs = refs[:1] + refs[8:] if len(refs) > 7 else ()
    pool_ref, rwkv_ref, q_ref, k_ref, v_ref, s5_ref, gate_ref = refs[len(refs) > 7:][:7]
    h = _rms(x_ref[...], g_ref[...]).astype(BF16)
    pool_ref[...] = jnp.dot(h, wp_ref[...], preferred_element_type=F32)
    rwkv_ref[...] = jnp.dot(h, wr_ref[...], preferred_element_type=F32)
    sb = jnp.dot(h, wsb_ref[...], preferred_element_type=F32)
    q = _head_rms(sb[:, :D_BR], qg_ref[...])
    k = _head_rms(sb[:, D_BR:2 * D_BR], kg_ref[...])
    v = sb[:, 2 * D_BR:]
    q_ref[...] = q
    k_ref[...] = k
    v_ref[...] = v
    s5_ref[...] = jnp.dot(h, ws5_ref[...], preferred_element_type=F32)
    gate_ref[...] = jnp.dot(h, wg_ref[...], preferred_element_type=F32)
    if head_major_refs:
        tails_ref, qh_ref, kh_ref, vh_ref = head_major_refs
        rows = q.shape[0]
        for hh in range(N_HEADS):
            cols = slice(hh * HEAD_DIM, (hh + 1) * HEAD_DIM)
            q_tail = jnp.broadcast_to(tails_ref[N_HEADS:N_HEADS + 1, :], (rows, HEAD_DIM))
            k_tail = jnp.broadcast_to(tails_ref[hh:hh + 1, :], (rows, HEAD_DIM))
            qh_ref[hh] = jnp.concatenate([q[:, cols] * (LOG2E * HEAD_DIM ** -0.5), q_tail], axis=-1).astype(BF16)
            kh_ref[hh] = jnp.concatenate([k[:, cols], k_tail], axis=-1).astype(BF16)
            vh_ref[hh] = v[:, cols].astype(BF16)


def in_proj(x, g, wp, wr, wsb, ws5, wg, qg, kg, sb_tails=None):
    m = x.shape[0]
    tm = min(m, 256)
    row = lambda n: pl.BlockSpec((tm, n), lambda i: (i, 0))
    widths = (D_BR, RWKV_COLS, D_BR, D_BR, D_BR, D_BR, 4 * D_MODEL)
    in_specs = [row(D_MODEL), _const_spec(g.shape), _const_spec(wp.shape), _const_spec(wr.shape),
                _const_spec(wsb.shape), _const_spec(ws5.shape), _const_spec(wg.shape),
                _const_spec(qg.shape), _const_spec(kg.shape)]
    args = [x, g, wp, wr, wsb, ws5, wg, qg, kg]
    out_specs = [row(n) for n in widths]
    out_shape = [jax.ShapeDtypeStruct((m, n), F32) for n in widths]
    if sb_tails is not None:
        in_specs.append(_const_spec(sb_tails.shape))
        args.append(sb_tails)
        for n in (2 * HEAD_DIM, 2 * HEAD_DIM, HEAD_DIM):
            out_specs.append(pl.BlockSpec((N_HEADS, tm, n), lambda i: (0, i, 0)))
            out_shape.append(jax.ShapeDtypeStruct((N_HEADS, m, n), BF16))
    return pl.pallas_call(
        _in_proj_kernel,
        grid=(m // tm,),
        in_specs=in_specs,
        out_specs=out_specs,
        out_shape=out_shape,
        compiler_params=_params(("arbitrary",)),
        name="in_proj",
    )(*args)


def _pool_kernel(u_ref, buf_ref, w_ref, scale_ref, o_ref, new_ref, ext_ref, *, tt, pos0):
    t = pl.program_id(1)

    @pl.when(t == 0)
    def _():
        ext_ref[0:POOL_HDR, :] = buf_ref[0]

    @pl.when(t > 0)
    def _():
        ext_ref[0:POOL_HDR, :] = ext_ref[tt:tt + POOL_HDR, :]

    u = u_ref[0]
    ext_ref[POOL_HDR:POOL_HDR + tt, :] = u
    lane = lax.broadcasted_iota(jnp.int32, (tt, D_BR), 1)
    pos = pos0 + t * tt + lax.broadcasted_iota(jnp.int32, (tt, D_BR), 0)
    s = u
    win_sum = jnp.zeros((tt, D_BR), F32)
    count = jnp.zeros((tt, D_BR), F32)
    for i in range(1, POOL_WINDOWS[-1]):
        s = s + ext_ref[POOL_HDR - i:POOL_HDR - i + tt, :]
        w = i + 1
        if w in POOL_WINDOWS:
            gi = POOL_WINDOWS.index(w)
            sel = (lane >= gi * HEAD_DIM) & (lane < (gi + 1) * HEAD_DIM)
            win_sum = jnp.where(sel, s, win_sum)
            count = jnp.where(sel, jnp.minimum(pos + 1, w).astype(F32), count)
    pooled = win_sum / count - u
    o_ref[0] = _dot_hi(pooled, w_ref[...]) * scale_ref[...]
    new_ref[0] = ext_ref[tt + 1:tt + POOL_HDR, :]


def pool_mixer(u, buf, w_bd, scale, pos0):
    b, t, _ = u.shape
    tt = min(t, 512)
    kern = functools.partial(_pool_kernel, tt=tt, pos0=pos0)
    return pl.pallas_call(
        kern,
        grid=(b, t // tt),
        in_specs=[pl.BlockSpec((1, tt, D_BR), lambda i, j: (i, j, 0)),
                  pl.BlockSpec((1, POOL_HDR, D_BR), lambda i, j: (i, 0, 0)),
                  _const_spec(w_bd.shape), _const_spec(scale.shape)],
        out_specs=[pl.BlockSpec((1, tt, D_BR), lambda i, j: (i, j, 0)),
                   pl.BlockSpec((1, POOL_BUF, D_BR), lambda i, j: (i, 0, 0))],
        out_shape=[jax.ShapeDtypeStruct((b, t, D_BR), F32),
                   jax.ShapeDtypeStruct((b, POOL_BUF, D_BR), F32)],
        scratch_shapes=[pltpu.VMEM((POOL_HDR + tt, D_BR), F32)],
        compiler_params=_params(("arbitrary", "arbitrary")),
        name="pool_mixer",
    )(u, buf, w_bd, scale)


def _rwkv_pre_kernel(p_ref, shift_ref, mu_ref, w0_ref, wup_ref, a0_ref, aup_ref, gup_ref,
                     kk_ref, ka_ref,
                     r_out, k_out, v_out, ld_out, kk_out, b_out, g_out, li_out, last_ref, *, tt, c):
    t = pl.program_id(1)

    @pl.when(t == 0)
    def _():
        last_ref[...] = shift_ref[0]

    p = p_ref[0]
    rowi = lax.broadcasted_iota(jnp.int32, (tt, RWKV_COLS), 0)
    prev = jnp.where(rowi == 0, last_ref[...], pltpu.roll(p, 1, 0))
    last_ref[...] = p[tt - 1:tt, :]
    pm = p + (prev - p) * mu_ref[...]
    r = pm[:, 0:D_BR]
    k = pm[:, D_BR:2 * D_BR]
    v = pm[:, 2 * D_BR:3 * D_BR]
    w_lo = pm[:, 768:832]
    a_lo = pm[:, 832:896]
    g_lo = pm[:, 896:1024]
    wz = w0_ref[...] + _dot(jnp.tanh(w_lo), wup_ref[...])
    w_log = _log_sigmoid(wz) - 0.5
    a = _sigmoid(a0_ref[...] + _dot(a_lo, aup_ref[...]))
    g = _dot(_sigmoid(g_lo), gup_ref[...])
    kk = k * kk_ref[...]
    parts = []
    for h in range(N_HEADS):
        ks = kk[:, h * HEAD_DIM:(h + 1) * HEAD_DIM]
        ss = jnp.sum(ks * ks, axis=-1, keepdims=True)
        parts.append(ks * lax.rsqrt(jnp.maximum(ss, 1e-24)))
    kk = jnp.concatenate(parts, axis=-1)
    ld = -jnp.exp(w_log)
    rowc = lax.broadcasted_iota(jnp.int32, (tt, D_BR), 0) & (c - 1)
    l_inc = ld
    off = 1
    while off < c:
        l_inc = l_inc + jnp.where(rowc >= off, pltpu.roll(l_inc, off, 0), 0.0)
        off *= 2
    r_out[0] = r
    k_out[0] = k * (1.0 + (a - 1.0) * ka_ref[...])
    v_out[0] = v
    ld_out[0] = ld
    kk_out[0] = kk
    b_out[0] = kk * a
    g_out[0] = g
    li_out[0] = l_inc


def rwkv_pre(p, shift, mu, w0, wup, a0, aup, gup, k_k, k_a):
    b, t, _ = p.shape
    tt = min(t, 512)
    kern = functools.partial(_rwkv_pre_kernel, tt=tt, c=min(t, RWKV_CHUNK))
    blk = pl.BlockSpec((1, tt, D_BR), lambda i, j: (i, j, 0))
    consts = (mu, w0, wup, a0, aup, gup, k_k, k_a)
    return pl.pallas_call(
        kern,
        grid=(b, t // tt),
        in_specs=[pl.BlockSpec((1, tt, RWKV_COLS), lambda i, j: (i, j, 0)),
                  pl.BlockSpec((1, 1, RWKV_COLS), lambda i, j: (i, 0, 0))]
                 + [_const_spec(c.shape) for c in consts],
        out_specs=[blk] * 8,
        out_shape=[jax.ShapeDtypeStruct((b, t, D_BR), F32)] * 8,
        scratch_shapes=[pltpu.VMEM((1, RWKV_COLS), F32)],
        compiler_params=_params(("arbitrary", "arbitrary")),
        name="rwkv_pre",
    )(p, shift, *consts)


_BMM_DIMS = {
    'nn': (((2,), (1,)), ((0,), (0,))),
    'nt': (((2,), (2,)), ((0,), (0,))),
}


def _bmm(a, b, dims):
    return lax.dot_general(a.astype(BF16), b.astype(BF16), _BMM_DIMS[dims], preferred_element_type=F32)


def _rwkv_chunk_kernel(r_ref, k_ref, v_ref, ld_ref, kk_ref, b_ref, g_ref, li_ref, s0_ref,
                       lnw_ref, lnb_ref, rk_ref, o_ref, sout_ref, s_ref, *, tt, c):
    t = pl.program_id(1)

    @pl.when(t == 0)
    def _():
        s_ref[...] = s0_ref[0]

    nch = tt // c
    nb = nch * N_HEADS

    def heads(x):
        return jnp.stack([x[ch * c:(ch + 1) * c, h * HEAD_DIM:(h + 1) * HEAD_DIM]
                          for ch in range(nch) for h in range(N_HEADS)])

    def head_rows(ref):
        return jnp.stack([ref[:, h * HEAD_DIM:(h + 1) * HEAD_DIM] for h in range(N_HEADS)])

    ri = lax.broadcasted_iota(jnp.int32, (nb, c, c), 1)
    ci = lax.broadcasted_iota(jnp.int32, (nb, c, c), 2)
    incl = (ri >= ci).astype(F32)
    strict = (ri > ci).astype(F32)
    eye = (ri == ci).astype(F32)
    n_sq = int(round(math.log2(c))) - 1

    r_all = r_ref[0]
    k_all = k_ref[0]
    v_all = v_ref[0]
    b_all = b_ref[0]
    l_inc = li_ref[0]
    l_tot = jnp.concatenate(
        [jnp.broadcast_to(l_inc[(ch + 1) * c - 1:(ch + 1) * c, :], (c, D_BR)) for ch in range(nch)], axis=0)
    e_neg = jnp.exp(-l_inc)
    e_end = jnp.exp(l_tot - l_inc)
    a_t = heads(-kk_ref[0] * jnp.exp(l_inc - ld_ref[0]))
    r_t = heads(r_all * jnp.exp(l_inc))
    b_t = heads(b_all * e_neg)
    k_t = heads(k_all * e_neg)
    b_e = heads(b_all * e_end)
    k_e = heads(k_all * e_end)
    d_tot = heads(jnp.exp(l_tot))[:, 0:1, :]
    r = heads(r_all)
    k = heads(k_all)
    v = heads(v_all)

    gram = _bmm(jnp.concatenate([a_t, r_t], axis=1), jnp.concatenate([b_t, k_t], axis=1), 'nt')
    m_ab = gram[:, :c, :c] * strict
    m_ak = gram[:, :c, c:] * strict
    p_rb = gram[:, c:, :c] * incl
    p_rk = gram[:, c:, c:] * incl
    pw = m_ab
    tinv = eye + m_ab
    for _ in range(n_sq):
        pw = _bmm(pw, pw, 'nn')
        tinv = tinv + _bmm(tinv, pw, 'nn')
    wu = _bmm(tinv, jnp.concatenate([a_t, _bmm(m_ak, v, 'nn')], axis=2), 'nn')
    o0 = _bmm(p_rk, v, 'nn')
    uv_rhs = jnp.concatenate([b_e, k_e], axis=1)

    lnw = head_rows(lnw_ref)
    lnb = head_rows(lnb_ref)
    rk = head_rows(rk_ref)
    s = s_ref[...]
    for ch in range(nch):
        sl = slice(ch * N_HEADS, (ch + 1) * N_HEADS)
        y = _bmm(jnp.concatenate([wu[sl, :, :HEAD_DIM], r_t[sl]], axis=1), s, 'nt')
        u = y[:, :c] + wu[sl, :, HEAD_DIM:]
        o = y[:, c:] + _bmm(p_rb[sl], u, 'nn') + o0[sl]
        uv_t = jnp.swapaxes(jnp.concatenate([u, v[sl]], axis=1), 1, 2)
        s = s * d_tot[sl] + _bmm(uv_t, uv_rhs[sl], 'nn')
        mean = jnp.mean(o, axis=-1, keepdims=True)
        var = jnp.mean(jnp.square(o - mean), axis=-1, keepdims=True)
        o = (o - mean) * lax.rsqrt(var + RWKV_LN_EPS) * lnw + lnb
        o = o + jnp.sum(r[sl] * k[sl] * rk, axis=-1, keepdims=True) * v[sl]
        rows = slice(ch * c, (ch + 1) * c)
        o_ref[0, rows, :] = (jnp.concatenate([o[h] for h in range(N_HEADS)], axis=-1)
                             * g_ref[0, rows, :])
    s_ref[...] = s
    sout_ref[0] = s


def rwkv_chunk(r, k, v, ld, kk, bv, g, l_inc, s0, ln_w, ln_b, r_k):
    b, t, _ = r.shape
    c = min(t, RWKV_CHUNK)
    tt = min(t, 256)
    kern = functools.partial(_rwkv_chunk_kernel, tt=tt, c=c)
    blk = pl.BlockSpec((1, tt, D_BR), lambda i, j: (i, j, 0))
    sblk = pl.BlockSpec((1, N_HEADS, HEAD_DIM, HEAD_DIM), lambda i, j: (i, 0, 0, 0))
    return pl.pallas_call(
        kern,
        grid=(b, t // tt),
        in_specs=[blk] * 8 + [sblk, _const_spec(ln_w.shape), _const_spec(ln_b.shape),
                              _const_spec(r_k.shape)],
        out_specs=[blk, sblk],
        out_shape=[jax.ShapeDtypeStruct((b, t, D_BR), F32),
                   jax.ShapeDtypeStruct((b, N_HEADS, HEAD_DIM, HEAD_DIM), F32)],
        scratch_shapes=[pltpu.VMEM((N_HEADS, HEAD_DIM, HEAD_DIM), F32)],
        compiler_params=_params(("arbitrary", "arbitrary")),
        name="rwkv_chunk",
    )(r, k, v, ld, kk, bv, g, l_inc, s0, ln_w, ln_b, r_k)


SB_GROUP = 8


LOG2E = 1.4426950408889634


def _softplus2(z2):
    return jnp.maximum(z2, 0.0) + jnp.log2(1.0 + jnp.exp2(-jnp.abs(z2)))


def _sb_prompt_kernel(q_ref, k_ref, v_ref, o_ref, *, tq):
    i = pl.program_id(1)
    q = q_ref[0]
    ri = lax.broadcasted_iota(jnp.int32, (tq, tq), 0)
    ci = lax.broadcasted_iota(jnp.int32, (tq, tq), 1)
    later = (ri > ci).astype(BF16)

    def scores(j):
        start = pl.multiple_of(j * tq, tq)
        return _dot_nt(q, k_ref[0, pl.ds(start, tq), :]), v_ref[0, pl.ds(start, tq), :]

    def full_blocks(j, n, car, acc):
        zv = [scores(j - b) for b in range(n)]
        sps = [_softplus2(z) for z, _ in zv]
        afts = [jnp.dot(sp.astype(BF16), later, preferred_element_type=F32) for sp in sps]
        atts = [jnp.exp2(zv[b][0] - sps[b] - afts[b]).astype(BF16) for b in range(n)]
        for b in range(n):
            acc = acc + jnp.exp2(-car) * jnp.dot(atts[b], zv[b][1], preferred_element_type=F32)
            car = car + jnp.sum(sps[b], axis=-1, keepdims=True)
        return car, acc

    z, vb = scores(i)
    mask = ci < ri
    sp = jnp.where(mask, _softplus2(z), 0.0)
    aft = jnp.dot(sp.astype(BF16), later, preferred_element_type=F32)
    att = jnp.where(mask, jnp.exp2(z - sp - aft), 0.0)
    acc = jnp.dot(att.astype(BF16), vb, preferred_element_type=F32)
    car = jnp.sum(sp, axis=-1, keepdims=True)

    j = i - 1
    g = 1
    while g < SB_GROUP:
        n_g = lax.shift_right_logical(i, g.bit_length() - 1) & 1
        car, acc = lax.fori_loop(
            0, n_g, lambda p, ca, j=j, g=g: full_blocks(j, g, *ca), (car, acc))
        j = j - g * n_g
        g *= 2
    _, acc = lax.fori_loop(
        0, lax.shift_right_logical(i, SB_GROUP.bit_length() - 1),
        lambda p, ca: full_blocks(j - SB_GROUP * p, SB_GROUP, *ca), (car, acc))
    o_ref[0] = acc


def sb_prompt(q, k, v):
    h, t, _ = q.shape
    tq = min(t, 256)
    kern = functools.partial(_sb_prompt_kernel, tq=tq)
    return pl.pallas_call(
        kern,
        grid=(h, t // tq),
        in_specs=[pl.BlockSpec((1, tq, 2 * HEAD_DIM), lambda a, i: (a, i, 0)),
                  pl.BlockSpec((1, t, 2 * HEAD_DIM), lambda a, i: (a, 0, 0)),
                  pl.BlockSpec((1, t, HEAD_DIM), lambda a, i: (a, 0, 0))],
        out_specs=pl.BlockSpec((1, tq, HEAD_DIM), lambda a, i: (a, i, 0)),
        out_shape=jax.ShapeDtypeStruct((h, t, HEAD_DIM), F32),
        compiler_params=_params(("arbitrary", "arbitrary")),
        name="sb_prompt",
    )(q, k, v)


PAGES_PER_STEP = 32


def _sb_paged_kernel(pt_ref, q_ref, bias_ref, kn_ref, vn_ref, *rest, t_new, past_len):
    nk = PAGES_PER_STEP
    k_pages = rest[:nk]
    v_pages = rest[nk:2 * nk]
    o_ref = rest[2 * nk]
    acc_ref, car_ref = rest[2 * nk + 1:]
    s = pl.program_id(1)
    q = q_ref[0] * (LOG2E * HEAD_DIM ** -0.5)
    qh = [q[:, h * HEAD_DIM:(h + 1) * HEAD_DIM].astype(BF16) for h in range(N_HEADS)]
    bias = bias_ref[...]
    ri = lax.broadcasted_iota(jnp.int32, (PAGE_SIZE, PAGE_SIZE), 0)
    ci = lax.broadcasted_iota(jnp.int32, (PAGE_SIZE, PAGE_SIZE), 1)
    later = (ri > ci).astype(BF16)

    def head_rows(x):
        return [x[h * t_new:(h + 1) * t_new] for h in range(N_HEADS)]

    @pl.when(s == 0)
    def _():
        kn = kn_ref[0].astype(BF16)
        vn = vn_ref[0].astype(BF16)
        cols = [slice(h * HEAD_DIM, (h + 1) * HEAD_DIM) for h in range(N_HEADS)]
        z = jnp.concatenate([_dot_nt(qh[h], kn[:, cols[h]]) for h in range(N_HEADS)], axis=0) + bias
        qi = jnp.concatenate([lax.broadcasted_iota(jnp.int32, (t_new, PAGE_SIZE), 0)] * N_HEADS, axis=0)
        mask = lax.broadcasted_iota(jnp.int32, (N_HEADS * t_new, PAGE_SIZE), 1) < qi
        sp = jnp.where(mask, _softplus2(z), 0.0)
        aft = jnp.dot(sp.astype(BF16), later, preferred_element_type=F32)
        att = head_rows(jnp.where(mask, jnp.exp2(z - sp - aft), 0.0).astype(BF16))
        acc_ref[...] = jnp.concatenate(
            [jnp.dot(att[h], vn[:, cols[h]], preferred_element_type=F32) for h in range(N_HEADS)], axis=0)
        car_ref[...] = jnp.sum(sp, axis=-1, keepdims=True)

    def gather_head(pages, h):
        return jnp.concatenate([p[0, 0, h] for p in pages], axis=1).astype(BF16)

    z = jnp.concatenate(
        [jnp.dot(qh[h], gather_head(k_pages, h), preferred_element_type=F32) for h in range(N_HEADS)],
        axis=0) + bias
    sp = _softplus2(z)
    car = car_ref[...]
    afts = []
    for rr in range(nk):
        sp_rr = sp[:, rr * PAGE_SIZE:(rr + 1) * PAGE_SIZE]
        afts.append(jnp.dot(sp_rr.astype(BF16), later, preferred_element_type=F32) + car)
        car = car + jnp.sum(sp_rr, axis=-1, keepdims=True)
    att = head_rows(jnp.exp2(z - sp - jnp.concatenate(afts, axis=1)).astype(BF16))
    acc_ref[...] += jnp.concatenate(
        [_dot_nt(att[h], gather_head(v_pages, h)) for h in range(N_HEADS)], axis=0)
    car_ref[...] = car

    @pl.when(s == pl.num_programs(1) - 1)
    def _():
        o_ref[0] = jnp.concatenate(head_rows(acc_ref[...]), axis=-1)


def sb_paged(q, bias_rows, k_new, v_new, cache_k, cache_v, page_table, layer):
    b, t_new, _ = q.shape
    n_pages = page_table.shape[1]
    past_len = n_pages * PAGE_SIZE
    nk = PAGES_PER_STEP
    n_steps = n_pages // nk
    kern = functools.partial(_sb_paged_kernel, t_new=t_new, past_len=past_len)

    def page_spec(rr):
        return pl.BlockSpec(
            (1, 1, N_HEADS, HEAD_DIM, PAGE_SIZE),
            lambda i, s, pt: (layer, pt[i, n_pages - 1 - (s * nk + rr)], 0, 0, 0))

    n_rows = N_HEADS * t_new
    grid_spec = pltpu.PrefetchScalarGridSpec(
        num_scalar_prefetch=1,
        grid=(b, n_steps),
        in_specs=[pl.BlockSpec((1, t_new, D_BR), lambda i, s, pt: (i, 0, 0)),
                  pl.BlockSpec((n_rows, 1), lambda i, s, pt: (0, 0)),
                  pl.BlockSpec((1, PAGE_SIZE, D_BR), lambda i, s, pt: (i, 0, 0)),
                  pl.BlockSpec((1, PAGE_SIZE, D_BR), lambda i, s, pt: (i, 0, 0))]
                 + [page_spec(rr) for rr in range(nk)] * 2,
        out_specs=pl.BlockSpec((1, t_new, D_BR), lambda i, s, pt: (i, 0, 0)),
        scratch_shapes=[pltpu.VMEM((n_rows, HEAD_DIM), F32), pltpu.VMEM((n_rows, 1), F32)],
    )
    return pl.pallas_call(
        kern,
        grid_spec=grid_spec,
        out_shape=jax.ShapeDtypeStruct((b, t_new, D_BR), F32),
        compiler_params=_params(("arbitrary", "arbitrary")),
        name="sb_paged",
    )(page_table, q, bias_rows, k_new, v_new, *([cache_k] * nk), *([cache_v] * nk))


def _s5_kernel(u_ref, hre_ref, him_ref, bre_ref, bim_ref, cre_ref, cim_ref, d_ref, wglu_ref,
               powr_ref, powi_ref, lvr_ref, lvi_ref,
               o_ref, sre_ref, sim_ref, cr_ref, cim_s_ref, *, tt):
    t = pl.program_id(1)

    @pl.when(t == 0)
    def _():
        cr_ref[...] = hre_ref[0]
        cim_s_ref[...] = him_ref[0]

    u = u_ref[0]
    xr = _dot(u, bre_ref[...])
    xi = _dot(u, bim_ref[...])
    row_in_group = lax.broadcasted_iota(jnp.int32, (tt, S5_N), 0) & 7
    for lvl, off in enumerate((1, 2, 4)):
        lr = lvr_ref[lvl:lvl + 1, :]
        li = lvi_ref[lvl:lvl + 1, :]
        m = row_in_group >= off
        sr = jnp.where(m, pltpu.roll(xr, off, 0), 0.0)
        si = jnp.where(m, pltpu.roll(xi, off, 0), 0.0)
        xr, xi = xr + lr * sr - li * si, xi + lr * si + li * sr
    pr = powr_ref[0:8, :]
    pi = powi_ref[0:8, :]
    hr = cr_ref[...]
    hi = cim_s_ref[...]
    groups_r, groups_i = [], []
    for g in range(tt // 8):
        gr = xr[g * 8:(g + 1) * 8, :]
        gi = xi[g * 8:(g + 1) * 8, :]
        gr, gi = gr + pr * hr - pi * hi, gi + pr * hi + pi * hr
        groups_r.append(gr)
        groups_i.append(gi)
        hr = gr[7:8, :]
        hi = gi[7:8, :]
    xr = jnp.concatenate(groups_r, axis=0)
    xi = jnp.concatenate(groups_i, axis=0)
    cr_ref[...] = xr[tt - 1:tt, :]
    cim_s_ref[...] = xi[tt - 1:tt, :]
    sre_ref[0] = xr[tt - 1:tt, :]
    sim_ref[0] = xi[tt - 1:tt, :]
    y = _dot(xr, cre_ref[...]) - _dot(xi, cim_ref[...]) + d_ref[...] * u
    z = _dot(jax.nn.gelu(y), wglu_ref[...])
    o_ref[0] = z[:, :D_BR] * _sigmoid(z[:, D_BR:])


def s5_mixer(u, h_re, h_im, b_re, b_im, c_re, c_im, d, w_glu, pow_re, pow_im, lv_re, lv_im):
    b, t, _ = u.shape
    tt = min(t, 256)
    kern = functools.partial(_s5_kernel, tt=tt)
    consts = (b_re, b_im, c_re, c_im, d, w_glu, pow_re, pow_im, lv_re, lv_im)
    sblk = pl.BlockSpec((1, 1, S5_N), lambda i, j: (i, 0, 0))
    return pl.pallas_call(
        kern,
        grid=(b, t // tt),
        in_specs=[pl.BlockSpec((1, tt, D_BR), lambda i, j: (i, j, 0)), sblk, sblk]
                 + [_const_spec(c.shape) for c in consts],
        out_specs=[pl.BlockSpec((1, tt, D_BR), lambda i, j: (i, j, 0)), sblk, sblk],
        out_shape=[jax.ShapeDtypeStruct((b, t, D_BR), F32),
                   jax.ShapeDtypeStruct((b, 1, S5_N), F32),
                   jax.ShapeDtypeStruct((b, 1, S5_N), F32)],
        scratch_shapes=[pltpu.VMEM((1, S5_N), F32), pltpu.VMEM((1, S5_N), F32)],
        compiler_params=_params(("arbitrary", "arbitrary")),
        name="s5_mixer",
    )(u, h_re, h_im, *consts)


def _merge_kernel(x_ref, bp_ref, br_ref, bs_ref, b5_ref, gate_ref, wb_ref, wo_ref, o_ref):
    if len(bs_ref.shape) == 3:
        o_sb = jnp.concatenate([bs_ref[h] for h in range(N_HEADS)], axis=-1)
    else:
        o_sb = bs_ref[...]
    merged = None
    for n, br in enumerate((bp_ref[...], br_ref[...], o_sb, b5_ref[...])):
        lifted = _dot(br, wb_ref[n])
        term = _sigmoid(gate_ref[:, n * D_MODEL:(n + 1) * D_MODEL]) * lifted
        merged = term if merged is None else merged + term
    o_ref[...] = x_ref[...] + _dot(merged, wo_ref[...])


def merge(x, o_pool, o_rwkv, o_sb, o_s5, gates, w_branch, w_out):
    m = x.shape[0]
    tm = min(m, 256)
    row = lambda n: pl.BlockSpec((tm, n), lambda i: (i, 0))
    sb_spec = row(D_BR) if o_sb.ndim == 2 else pl.BlockSpec((N_HEADS, tm, HEAD_DIM), lambda i: (0, i, 0))
    return pl.pallas_call(
        _merge_kernel,
        grid=(m // tm,),
        in_specs=[row(D_MODEL), row(D_BR), row(D_BR), sb_spec, row(D_BR), row(4 * D_MODEL),
                  _const_spec(w_branch.shape), _const_spec(w_out.shape)],
        out_specs=row(D_MODEL),
        out_shape=jax.ShapeDtypeStruct((m, D_MODEL), F32),
        compiler_params=_params(("arbitrary",)),
        name="merge",
    )(x, o_pool, o_rwkv, o_sb, o_s5, gates, w_branch, w_out)


def _xattn_kernel(x_ref, mk_ref, mv_ref, g_ref, wq_ref, qg_ref, wo_ref, o_ref):
    x = x_ref[0]
    hn = _rms(x, g_ref[...])
    q = _head_rms(_dot(hn, wq_ref[...]), qg_ref[...]) * HEAD_DIM ** -0.5
    outs = []
    for h in range(N_HEADS):
        cols = slice(h * HEAD_DIM, (h + 1) * HEAD_DIM)
        s = _dot(q[:, cols], mk_ref[0, 0, h])
        s = s - jnp.max(s, axis=-1, keepdims=True)
        e = jnp.exp(s)
        pr = e / jnp.sum(e, axis=-1, keepdims=True)
        outs.append(_dot_nt(pr.astype(BF16), mv_ref[0, 0, h].astype(BF16)))
    o = jnp.concatenate(outs, axis=-1)
    o_ref[0] = x + _dot(o, wo_ref[...])


def cross_attention(x, mem_k, mem_v, layer, g, wq, qg, wo):
    b, t, _ = x.shape
    tt = min(t, 512)
    mem_spec = pl.BlockSpec((1, 1, N_HEADS, HEAD_DIM, N_MEM), lambda i, j: (layer, i, 0, 0, 0))
    return pl.pallas_call(
        _xattn_kernel,
        grid=(b, t // tt),
        in_specs=[pl.BlockSpec((1, tt, D_MODEL), lambda i, j: (i, j, 0)), mem_spec, mem_spec,
                  _const_spec(g.shape), _const_spec(wq.shape), _const_spec(qg.shape),
                  _const_spec(wo.shape)],
        out_specs=pl.BlockSpec((1, tt, D_MODEL), lambda i, j: (i, j, 0)),
        out_shape=jax.ShapeDtypeStruct((b, t, D_MODEL), F32),
        compiler_params=_params(("arbitrary", "arbitrary")),
        name="cross_attention",
    )(x, mem_k, mem_v, g, wq, qg, wo)


def _xattn_batched_kernel(x_ref, mk_ref, mv_ref, g_ref, wq_ref, qg_ref, wo_ref, o_ref, *, b, t):
    x = x_ref[...]
    hn = _rms(x, g_ref[...])
    q = (_head_rms(_dot(hn, wq_ref[...]), qg_ref[...]) * HEAD_DIM ** -0.5).astype(BF16)
    cols = [slice(h * HEAD_DIM, (h + 1) * HEAD_DIM) for h in range(N_HEADS)]
    s = jnp.concatenate(
        [jnp.dot(q[bi * t:(bi + 1) * t, cols[h]], mk_ref[0, bi, h].astype(BF16), preferred_element_type=F32)
         for bi in range(b) for h in range(N_HEADS)], axis=0)
    s = s - jnp.max(s, axis=-1, keepdims=True)
    e = jnp.exp(s)
    pr = (e / jnp.sum(e, axis=-1, keepdims=True)).astype(BF16)
    o = jnp.concatenate(
        [jnp.concatenate(
            [_dot_nt(pr[(bi * N_HEADS + h) * t:(bi * N_HEADS + h + 1) * t], mv_ref[0, bi, h].astype(BF16))
             for h in range(N_HEADS)], axis=-1)
         for bi in range(b)], axis=0)
    o_ref[...] = x + _dot(o, wo_ref[...])


def cross_attention_batched(x, mem_k, mem_v, layer, g, wq, qg, wo):
    b, t, _ = x.shape
    m = b * t
    kern = functools.partial(_xattn_batched_kernel, b=b, t=t)
    mem_spec = pl.BlockSpec((1, b, N_HEADS, HEAD_DIM, N_MEM), lambda i: (layer, 0, 0, 0, 0))
    out = pl.pallas_call(
        kern,
        grid=(1,),
        in_specs=[_const_spec((m, D_MODEL)), mem_spec, mem_spec, _const_spec(g.shape),
                  _const_spec(wq.shape), _const_spec(qg.shape), _const_spec(wo.shape)],
        out_specs=_const_spec((m, D_MODEL)),
        out_shape=jax.ShapeDtypeStruct((m, D_MODEL), F32),
        compiler_params=_params(("arbitrary",)),
        name="cross_attention_batched",
    )(x.reshape(m, D_MODEL), mem_k, mem_v, g, wq, qg, wo)
    return out.reshape(b, t, D_MODEL)


def _memkv_kernel(mem_ref, g_ref, wk_ref, wv_ref, kg_ref, k_ref, v_ref, kt_ref, vt_ref):
    mn = _rms(mem_ref[...], g_ref[...])
    k = _head_rms(_dot(mn, wk_ref[...]), kg_ref[...])
    v = _dot(mn, wv_ref[...])
    k_ref[...] = k
    v_ref[...] = v
    kt_ref[...] = k.T
    vt_ref[...] = v.T


def memory_kv(mem, g, wk, wv, kg):
    m = mem.shape[0]
    args = (mem, g, wk, wv, kg)
    return pl.pallas_call(
        _memkv_kernel,
        grid=(1,),
        in_specs=[_const_spec(a.shape) for a in args],
        out_specs=[_const_spec((m, D_BR))] * 2 + [_const_spec((D_BR, m))] * 2,
        out_shape=[jax.ShapeDtypeStruct((m, D_BR), F32)] * 2 + [jax.ShapeDtypeStruct((D_BR, m), F32)] * 2,
        compiler_params=_params(("arbitrary",)),
        name="memory_kv",
    )(*args)


def _ffn_kernel(x_ref, g_ref, wg_ref, wu_ref, wd_ref, o_ref):
    x = x_ref[...]
    hn = _rms(x, g_ref[...]).astype(BF16)
    a = jnp.dot(hn, wg_ref[...], preferred_element_type=F32)
    bq = jnp.dot(hn, wu_ref[...], preferred_element_type=F32)
    act = a * _sigmoid(a) * bq
    o_ref[...] = x + _dot(act, wd_ref[...])


def _resident_spec(shape):
    nd = len(shape)
    return pl.BlockSpec(shape, lambda *_: (0,) * nd, pipeline_mode=pl.Buffered(1))


def ffn(x, g, w_gate, w_up, w_down):
    m = x.shape[0]
    tm = min(m, 512)
    row = pl.BlockSpec((tm, D_MODEL), lambda i: (i, 0))
    return pl.pallas_call(
        _ffn_kernel,
        grid=(m // tm,),
        in_specs=[row, _const_spec(g.shape), _resident_spec(w_gate.shape), _resident_spec(w_up.shape),
                  _resident_spec(w_down.shape)],
        out_specs=row,
        out_shape=jax.ShapeDtypeStruct((m, D_MODEL), F32),
        compiler_params=_params(("arbitrary",)),
        name="ffn",
    )(x, g, w_gate, w_up, w_down)


def _block_diag(blocks):
    g, m, n = blocks.shape
    eye = jnp.eye(g, dtype=blocks.dtype)
    return (eye[:, None, :, None] * blocks[:, :, None, :]).reshape(g * m, g * n)


def _tile_heads(v):
    return jnp.tile(v, N_HEADS).reshape(1, N_HEADS * v.shape[0])


def _cmul(ar, ai, br, bi):
    return ar * br - ai * bi, ar * bi + ai * br


def _s5_constants(a_re, a_im, log_dt, b_re, b_im, c_re, c_im, tt):
    dt_g = jnp.exp(log_dt)[:, None]
    mag = jnp.exp(dt_g * a_re)
    ab_re, ab_im = mag * jnp.cos(dt_g * a_im), mag * jnp.sin(dt_g * a_im)
    den = a_re * a_re + a_im * a_im
    n_re = ab_re - 1.0
    f_re = (n_re * a_re + ab_im * a_im) / den
    f_im = (ab_im * a_re - n_re * a_im) / den
    bb_re = f_re[..., None] * b_re - f_im[..., None] * b_im
    bb_im = f_re[..., None] * b_im + f_im[..., None] * b_re
    bmat_re = _block_diag(jnp.swapaxes(bb_re, 1, 2))
    bmat_im = _block_diag(jnp.swapaxes(bb_im, 1, 2))
    cmat_re = _block_diag(jnp.swapaxes(c_re, 1, 2))
    cmat_im = _block_diag(jnp.swapaxes(c_im, 1, 2))
    lam_re = ab_re.reshape(1, S5_N)
    lam_im = ab_im.reshape(1, S5_N)
    pow_re, pow_im = lam_re, lam_im
    lv_re, lv_im = [lam_re], [lam_im]
    cur_re, cur_im = lam_re, lam_im
    n = 1
    while n < tt:
        nr, ni = _cmul(pow_re, pow_im, cur_re, cur_im)
        pow_re = jnp.concatenate([pow_re, nr], axis=0)
        pow_im = jnp.concatenate([pow_im, ni], axis=0)
        cur_re, cur_im = _cmul(cur_re, cur_im, cur_re, cur_im)
        lv_re.append(cur_re)
        lv_im.append(cur_im)
        n *= 2
    n_lv = max(len(lv_re) - 1, 1)
    lv_re = jnp.concatenate(lv_re[:n_lv], axis=0)
    lv_im = jnp.concatenate(lv_im[:n_lv], axis=0)
    return (bmat_re.astype(BF16), bmat_im.astype(BF16), cmat_re.astype(BF16), cmat_im.astype(BF16),
            pow_re, pow_im, lv_re, lv_im)


def _layer_weights(l, w, tt_prompt, tt_sample):
    w_in = w['w_in'][l]
    lw = {
        'norm_mix': w['norm_mix'][l].reshape(1, D_MODEL),
        'norm_cross': w['norm_cross'][l].reshape(1, D_MODEL),
        'norm_mem': w['norm_mem'][l].reshape(1, D_MODEL),
        'norm_ffn': w['norm_ffn'][l].reshape(1, D_MODEL),
        'w_pool_in': w_in[:, :OFF_RWKV].astype(BF16),
        'w_rwkv_in': w_in[:, OFF_RWKV:OFF_SB].astype(BF16),
        'w_sb_in': w_in[:, OFF_SB:OFF_S5].astype(BF16),
        'w_s5_in': w_in[:, OFF_S5:OFF_GATE].astype(BF16),
        'w_gate_in': w_in[:, OFF_GATE:].astype(BF16),
        'sb_q_norm': _tile_heads(w['sb_q_norm'][l]),
        'sb_k_norm': _tile_heads(w['sb_k_norm'][l]),
        'sb_bias': w['sb_bias'][l],
        'pool_w': _block_diag(w['pool_w'][l]),
        'pool_scale': w['pool_scale'][l].reshape(1, D_BR),
        'rwkv_mu': w['rwkv_mu'][l].reshape(1, RWKV_COLS),
        'rwkv_w0': w['rwkv_w0'][l].reshape(1, D_BR),
        'rwkv_w_up': w['rwkv_w_up'][l].astype(BF16),
        'rwkv_a0': w['rwkv_a0'][l].reshape(1, D_BR),
        'rwkv_a_up': w['rwkv_a_up'][l].astype(BF16),
        'rwkv_g_up': w['rwkv_g_up'][l].astype(BF16),
        'rwkv_k_k': w['rwkv_k_k'][l].reshape(1, D_BR),
        'rwkv_k_a': w['rwkv_k_a'][l].reshape(1, D_BR),
        'rwkv_r_k': w['rwkv_r_k'][l].reshape(1, D_BR),
        'rwkv_ln_w': w['rwkv_ln_w'][l].reshape(1, D_BR),
        'rwkv_ln_b': w['rwkv_ln_b'][l].reshape(1, D_BR),
        's5_d': w['s5_d'][l].reshape(1, D_BR),
        's5_w_glu': w['s5_w_glu'][l].astype(BF16),
        'w_branch': w['w_branch'][l].astype(BF16),
        'w_out': w['w_out'][l].astype(BF16),
        'xa_w_q': w['xa_w_q'][l].astype(BF16),
        'xa_w_k': w['xa_w_k'][l].astype(BF16),
        'xa_w_v': w['xa_w_v'][l].astype(BF16),
        'xa_q_norm': _tile_heads(w['xa_q_norm'][l]),
        'xa_k_norm': _tile_heads(w['xa_k_norm'][l]),
        'xa_w_o': w['xa_w_o'][l].astype(BF16),
        'ffn_w_gate': w['ffn_w_gate'][l].astype(BF16),
        'ffn_w_up': w['ffn_w_up'][l].astype(BF16),
        'ffn_w_down': w['ffn_w_down'][l].astype(BF16),
    }
    b2 = w['sb_bias'][l] * LOG2E
    p1 = b2.astype(BF16).astype(F32)
    p2 = (b2 - p1).astype(BF16).astype(F32)
    p3 = (b2 - p1 - p2).astype(BF16).astype(F32)
    k_tails = jnp.pad(jnp.stack([p1, p2, p3], axis=1), ((0, 0), (0, HEAD_DIM - 3)))
    q_tail = jnp.pad(jnp.ones((1, 3), F32), ((0, 0), (0, HEAD_DIM - 3)))
    lw['sb_tails'] = jnp.concatenate([k_tails, q_tail], axis=0)
    s5_args = (w['s5_a_re'][l], w['s5_a_im'][l], w['s5_log_dt'][l], w['s5_b_re'][l], w['s5_b_im'][l],
               w['s5_c_re'][l], w['s5_c_im'][l])
    lw['s5_prompt'] = lw['s5_sample'] = _s5_constants(*s5_args, 8)
    return lw


def _trunk_layer(x, pos0, mem_k, mem_v, mem_layer, sb_fn, pool_buf, shift, wkv, s5_re, s5_im, lw, s5c,
                 fresh):
    b, t, _ = x.shape
    m = b * t
    xf = x.reshape(m, D_MODEL)
    u_pool, p_rwkv, q, k, v, u_s5, gates, *head_major = in_proj(
        xf, lw['norm_mix'], lw['w_pool_in'], lw['w_rwkv_in'], lw['w_sb_in'], lw['w_s5_in'],
        lw['w_gate_in'], lw['sb_q_norm'], lw['sb_k_norm'], lw['sb_tails'] if fresh else None)
    o_pool, new_pool = pool_mixer(u_pool.reshape(b, t, D_BR), pool_buf, lw['pool_w'],
                                  lw['pool_scale'], pos0)
    p3 = p_rwkv.reshape(b, t, RWKV_COLS)
    pre = rwkv_pre(p3, shift, lw['rwkv_mu'], lw['rwkv_w0'], lw['rwkv_w_up'], lw['rwkv_a0'],
                   lw['rwkv_a_up'], lw['rwkv_g_up'], lw['rwkv_k_k'], lw['rwkv_k_a'])
    o_rwkv, new_wkv = rwkv_chunk(*pre, wkv, lw['rwkv_ln_w'], lw['rwkv_ln_b'], lw['rwkv_r_k'])
    new_shift = p3[:, -1]
    q3, k3, v3 = (z.reshape(b, t, D_BR) for z in (q, k, v))
    o_sb = sb_fn(*head_major) if fresh else sb_fn(q3, k3, v3).reshape(m, D_BR)
    o_s5, new_re, new_im = s5_mixer(u_s5.reshape(b, t, D_BR), s5_re, s5_im, s5c[0], s5c[1], s5c[2],
                                    s5c[3], lw['s5_d'], lw['s5_w_glu'], *s5c[4:])
    x1 = merge(xf, o_pool.reshape(m, D_BR), o_rwkv.reshape(m, D_BR), o_sb,
               o_s5.reshape(m, D_BR), gates, lw['w_branch'], lw['w_out'])
    xattn = cross_attention if fresh else cross_attention_batched
    x2 = xattn(x1.reshape(b, t, D_MODEL), mem_k, mem_v, mem_layer, lw['norm_cross'],
               lw['xa_w_q'], lw['xa_q_norm'], lw['xa_w_o'])
    x3 = ffn(x2.reshape(m, D_MODEL), lw['norm_ffn'], lw['ffn_w_gate'], lw['ffn_w_up'],
             lw['ffn_w_down'])
    return x3.reshape(b, t, D_MODEL), (k3, v3, new_pool, new_shift, new_wkv, new_re, new_im)


def kernel(x_prompt, x_sample, cache_sb_k, cache_sb_v, cache_mem_k, cache_mem_v, state_pool, state_rwkv_shift, state_rwkv_wkv, state_s5_re, state_s5_im, page_table, mem_prompt, norm_mix, norm_cross, norm_mem, norm_ffn, w_in, pool_w, pool_scale, rwkv_mu, rwkv_w0, rwkv_w_up, rwkv_a0, rwkv_a_up, rwkv_g_up, rwkv_k_k, rwkv_k_a, rwkv_r_k, rwkv_ln_w, rwkv_ln_b, sb_q_norm, sb_k_norm, sb_bias, s5_a_re, s5_a_im, s5_log_dt, s5_b_re, s5_b_im, s5_c_re, s5_c_im, s5_d, s5_w_glu, w_branch, w_out, xa_w_q, xa_w_k, xa_w_v, xa_q_norm, xa_k_norm, xa_w_o, ffn_w_gate, ffn_w_up, ffn_w_down):
    weights = dict(
        norm_mix=norm_mix, norm_cross=norm_cross, norm_mem=norm_mem, norm_ffn=norm_ffn, w_in=w_in,
        pool_w=pool_w, pool_scale=pool_scale, rwkv_mu=rwkv_mu, rwkv_w0=rwkv_w0, rwkv_w_up=rwkv_w_up,
        rwkv_a0=rwkv_a0, rwkv_a_up=rwkv_a_up, rwkv_g_up=rwkv_g_up, rwkv_k_k=rwkv_k_k,
        rwkv_k_a=rwkv_k_a, rwkv_r_k=rwkv_r_k, rwkv_ln_w=rwkv_ln_w, rwkv_ln_b=rwkv_ln_b,
        sb_q_norm=sb_q_norm, sb_k_norm=sb_k_norm, sb_bias=sb_bias, s5_a_re=s5_a_re, s5_a_im=s5_a_im,
        s5_log_dt=s5_log_dt, s5_b_re=s5_b_re, s5_b_im=s5_b_im, s5_c_re=s5_c_re, s5_c_im=s5_c_im,
        s5_d=s5_d, s5_w_glu=s5_w_glu, w_branch=w_branch, w_out=w_out, xa_w_q=xa_w_q, xa_w_k=xa_w_k,
        xa_w_v=xa_w_v, xa_q_norm=xa_q_norm, xa_k_norm=xa_k_norm, xa_w_o=xa_w_o,
        ffn_w_gate=ffn_w_gate, ffn_w_up=ffn_w_up, ffn_w_down=ffn_w_down)
    depth = w_in.shape[0]
    bp, tp, _ = x_prompt.shape
    bs, ts, _ = x_sample.shape
    n_pool = cache_sb_k.shape[1]
    cache_k = jnp.transpose(cache_sb_k, (0, 1, 3, 4, 2))
    cache_v = jnp.transpose(cache_sb_v, (0, 1, 3, 4, 2))
    mem_k_t = jnp.transpose(cache_mem_k, (0, 1, 3, 4, 2))
    mem_v_t = jnp.transpose(cache_mem_v, (0, 1, 3, 4, 2))
    past_len = page_table.shape[1] * PAGE_SIZE
    tt_s5_p = min(tp, 256)
    tt_s5_s = ts

    xp, xs = x_prompt, x_sample
    outs_p = [[] for _ in range(9)]
    outs_s = [[] for _ in range(7)]
    for l in range(depth):
        lw = _layer_weights(l, weights, tt_s5_p, tt_s5_s)
        mk_p, mv_p, mkt_p, mvt_p = memory_kv(mem_prompt.reshape(bp * N_MEM, D_MODEL), lw['norm_mem'],
                                             lw['xa_w_k'], lw['xa_w_v'], lw['xa_k_norm'])
        mem_t_shape = (1, bp, N_HEADS, HEAD_DIM, N_MEM)

        xp, st = _trunk_layer(
            xp, 0, mkt_p.reshape(mem_t_shape), mvt_p.reshape(mem_t_shape), 0, sb_prompt,
            jnp.zeros((bp, POOL_HDR, D_BR), F32), jnp.zeros((bp, 1, RWKV_COLS), F32),
            jnp.zeros((bp, N_HEADS, HEAD_DIM, HEAD_DIM), F32), jnp.zeros((bp, 1, S5_N), F32),
            jnp.zeros((bp, 1, S5_N), F32), lw, lw['s5_prompt'], True)
        for lst, val in zip(outs_p, (st[0], st[1], mk_p, mv_p) + st[2:]):
            lst.append(val)

        bias_rows = jnp.repeat(lw['sb_bias'] * LOG2E, ts).reshape(N_HEADS * ts, 1)

        def sb_sample_fn(q, k, v, l=l, bias_rows=bias_rows):
            pad = ((0, 0), (0, PAGE_SIZE - ts), (0, 0))
            return sb_paged(q, bias_rows, jnp.pad(k, pad), jnp.pad(v, pad), cache_k, cache_v,
                            page_table, l)

        pool_buf = jnp.pad(state_pool[l], ((0, 0), (1, 0), (0, 0)))
        xs, st = _trunk_layer(
            xs, past_len, mem_k_t, mem_v_t, l, sb_sample_fn,
            pool_buf, state_rwkv_shift[l].reshape(bs, 1, RWKV_COLS), state_rwkv_wkv[l],
            state_s5_re[l].reshape(bs, 1, S5_N), state_s5_im[l].reshape(bs, 1, S5_N),
            lw, lw['s5_sample'], False)
        for lst, val in zip(outs_s, st):
            lst.append(val)

    def stack(lst, shape):
        return jnp.stack(lst, 0).reshape((depth,) + shape)

    kv_p = (bp, tp, N_HEADS, HEAD_DIM)
    kv_s = (bs, ts, N_HEADS, HEAD_DIM)
    mem_shape = (bp, N_MEM, N_HEADS, HEAD_DIM)
    return (xp, xs,
            stack(outs_p[0], kv_p), stack(outs_p[1], kv_p),
            stack(outs_p[2], mem_shape), stack(outs_p[3], mem_shape),
            stack(outs_p[4], (bp, POOL_BUF, D_BR)), stack(outs_p[5], (bp, RWKV_COLS)),
            stack(outs_p[6], (bp, N_HEADS, HEAD_DIM, HEAD_DIM)),
            stack(outs_p[7], (bp, S5_GROUPS, S5_STATE)), stack(outs_p[8], (bp, S5_GROUPS, S5_STATE)),
            stack(outs_s[0], kv_s), stack(outs_s[1], kv_s),
            stack(outs_s[2], (bs, POOL_BUF, D_BR)), stack(outs_s[3], (bs, RWKV_COLS)),
            stack(outs_s[4], (bs, N_HEADS, HEAD_DIM, HEAD_DIM)),
            stack(outs_s[5], (bs, S5_GROUPS, S5_STATE)), stack(outs_s[6], (bs, S5_GROUPS, S5_STATE)))
```
